```python
import jax, jax.numpy as jnp
from jax import lax
import numpy as np

D_MODEL = 2048
BATCH = 1
SEQ = 8192
DEPTH = 1

CHUNK = 64
N_META = 16

D_MIX = D_MODEL
D_POOL = D_MIX // 2
POOL_WINDOWS = (2, 4, 8, 16)
POOL_GROUP = D_POOL // len(POOL_WINDOWS)
D_SSM = D_MIX - D_POOL
SSM_HEAD_DIM = 64
SSM_HEADS = D_SSM // SSM_HEAD_DIM
SSM_GROUPS = 2
SSM_STATE = 128
CONV_WIDTH = 4
D_CONV = D_SSM + 2 * SSM_GROUPS * SSM_STATE
D_IN_PROJ = D_POOL + D_SSM + D_CONV + SSM_HEADS

N_EXPERTS = 32
TOP_K = 4
D_EXPERT = D_MODEL
SWIGLU_LIMIT = 7.0
SWIGLU_ALPHA = 1.702

NORM_EPS = 1e-5

kernel_name = "hymba_pool_ssd_moe_block"


def rms_norm(x, g):
    x32 = x.astype(jnp.float32)
    y = x32 * lax.rsqrt(jnp.mean(x32 * x32, axis=-1, keepdims=True) + NORM_EPS)
    return (y * g.astype(jnp.float32)).astype(x.dtype)


def multiscale_pool(u, pool_w, pool_scale):
    b, L, _ = u.shape
    u32 = u.astype(jnp.float32)
    cs = jnp.pad(jnp.cumsum(u32, axis=1), ((0, 0), (1, 0), (0, 0)))
    t = jnp.arange(L)
    diffs = []
    for g, w in enumerate(POOL_WINDOWS):
        sl = slice(g * POOL_GROUP, (g + 1) * POOL_GROUP)
        lo = jnp.maximum(t + 1 - w, 0)
        cnt = (t + 1 - lo).astype(jnp.float32)
        csg = cs[..., sl]
        mean = (csg[:, t + 1] - csg[:, lo]) / cnt[None, :, None]
        diffs.append(mean - u32[..., sl])
    d = jnp.stack(diffs, axis=2)
    y = jnp.einsum('blgc,gcd->blgd', d, pool_w.astype(jnp.float32))
    y = y.reshape(b, L, D_POOL) * pool_scale.astype(jnp.float32)
    return y.astype(u.dtype)


def causal_depthwise_conv(u, w, bias):
    C = u.shape[-1]
    out = lax.conv_general_dilated(
        u, w[:, None, :].astype(u.dtype), window_strides=(1,),
        padding=[(CONV_WIDTH - 1, 0)], dimension_numbers=('NWC', 'WIO', 'NWC'),
        feature_group_count=C)
    return out + bias.astype(u.dtype)


def ssd_chunked(xh, dt, A, Bh, Ch):
    b, L, H, P = xh.shape
    N = Bh.shape[-1]
    lead = (-N_META) % CHUNK
    tail = (-(lead + L)) % CHUNK
    pad4 = ((0, 0), (lead, tail), (0, 0), (0, 0))
    xh = jnp.pad(xh, pad4)
    Bh = jnp.pad(Bh, pad4)
    Ch = jnp.pad(Ch, pad4)
    dt = jnp.pad(dt, ((0, 0), (lead, tail), (0, 0)))
    Lp = L + lead + tail
    nc = Lp // CHUNK

    xdt = (xh * dt[..., None]).reshape(b, nc, CHUNK, H, P)
    Bc = Bh.reshape(b, nc, CHUNK, H, N)
    Cc = Ch.reshape(b, nc, CHUNK, H, N)
    dA = (dt * A).reshape(b, nc, CHUNK, H).transpose(0, 3, 1, 2)
    Acs = jnp.cumsum(dA, axis=-1)

    causal = jnp.tril(jnp.ones((CHUNK, CHUNK), dtype=bool))
    seg = Acs[..., :, None] - Acs[..., None, :]
    Lmat = jnp.exp(jnp.where(causal, seg, -jnp.inf))
    scores = jnp.einsum('bclhn,bcshn->bhcls', Cc, Bc) * Lmat
    y_diag = jnp.einsum('bhcls,bcshp->bclhp', scores, xdt)

    decay = jnp.exp(Acs[..., -1:] - Acs).transpose(0, 2, 3, 1)
    states = jnp.einsum('bclhn,bclhp->bchpn', Bc * decay[..., None], xdt)
    chunk_decay = jnp.exp(Acs[..., -1])

    def step(carry, inp):
        s_c, dec_c = inp
        return carry * dec_c[..., None, None] + s_c, carry

    init = jnp.zeros((b, H, P, N), dtype=xdt.dtype)
    _, prev = lax.scan(step, init, (states.transpose(1, 0, 2, 3, 4),
                                    chunk_decay.transpose(2, 0, 1)))
    prev = prev.transpose(1, 0, 2, 3, 4)

    in_decay = jnp.exp(Acs).transpose(0, 2, 3, 1)
    y_off = jnp.einsum('bclhn,bchpn->bclhp', Cc, prev) * in_decay[..., None]

    y = (y_diag + y_off).reshape(b, Lp, H, P)
    return y[:, lead:lead + L]


def mamba2_mixer(z, xbc, dt_raw, conv_w, conv_b, dt_bias, a_log, d_skip, norm_g):
    f32 = jnp.float32
    xbc = jax.nn.silu(causal_depthwise_conv(xbc, conv_w, conv_b))
    xs, Bm, Cm = jnp.split(xbc, [D_SSM, D_SSM + SSM_GROUPS * SSM_STATE], axis=-1)
    b, L, _ = xs.shape
    rep = SSM_HEADS // SSM_GROUPS
    dt = jax.nn.softplus(dt_raw.astype(f32) + dt_bias.astype(f32))
    A = -jnp.exp(a_log.astype(f32))
    xh = xs.astype(f32).reshape(b, L, SSM_HEADS, SSM_HEAD_DIM)
    Bh = jnp.repeat(Bm.astype(f32).reshape(b, L, SSM_GROUPS, SSM_STATE), rep, axis=2)
    Ch = jnp.repeat(Cm.astype(f32).reshape(b, L, SSM_GROUPS, SSM_STATE), rep, axis=2)
    y = ssd_chunked(xh, dt, A, Bh, Ch) + d_skip.astype(f32)[:, None] * xh
    y = y.reshape(b, L, D_SSM) * jax.nn.silu(z.astype(f32))
    yg = y.reshape(b, L, SSM_GROUPS, D_SSM // SSM_GROUPS)
    yg = yg * lax.rsqrt(jnp.mean(yg * yg, axis=-1, keepdims=True) + NORM_EPS)
    y = yg.reshape(b, L, D_SSM) * norm_g.astype(f32)
    return y.astype(z.dtype)


def moe_ffn(hn, router_w, router_b, w_gate, b_gate, w_up, b_up, w_down, b_down):
    b, S, D = hn.shape
    t = hn.reshape(-1, D)
    n = t.shape[0]
    logits = (t @ router_w + router_b).astype(jnp.float32)
    top_v, top_e = lax.top_k(logits, TOP_K)
    gates = jax.nn.softmax(top_v, axis=-1)
    flat_e = top_e.reshape(-1)
    order = jnp.argsort(flat_e)
    e_sorted = flat_e[order]
    tok = order // TOP_K
    xs = t[tok]
    sizes = jnp.bincount(flat_e, length=N_EXPERTS).astype(jnp.int32)
    g = lax.ragged_dot(xs, w_gate, sizes) + b_gate[e_sorted]
    u = lax.ragged_dot(xs, w_up, sizes) + b_up[e_sorted]
    g = jnp.minimum(g, SWIGLU_LIMIT)
    u = jnp.clip(u, -SWIGLU_LIMIT, SWIGLU_LIMIT)
    a = (u + 1.0) * (g * jax.nn.sigmoid(SWIGLU_ALPHA * g))
    o = lax.ragged_dot(a, w_down, sizes) + b_down[e_sorted]
    o = o * gates.reshape(-1)[order][:, None].astype(o.dtype)
    y = jnp.zeros((n, D), dtype=o.dtype).at[tok].add(o)
    return y.reshape(b, S, D).astype(hn.dtype)


def setup_inputs(seed: int = 0) -> dict:
    key = jax.random.key(seed)
    ks = jax.random.split(key, 24)
    f32 = jnp.float32

    def nrm(k, shape, scale):
        return jax.random.normal(k, shape, f32) * scale

    def gain(k, shape):
        return 1.0 + 0.02 * jax.random.normal(k, shape, f32)

    dt0 = jnp.exp(jax.random.uniform(ks[9], (DEPTH, SSM_HEADS), f32)
                  * (np.log(0.1) - np.log(0.001)) + np.log(0.001)).astype(f32)
    dt_bias = dt0 + jnp.log(-jnp.expm1(-dt0))
    a_log = jnp.log(jax.random.uniform(ks[10], (DEPTH, SSM_HEADS), f32, 1.0, 16.0))

    return {
        "x": nrm(ks[0], (BATCH, SEQ, D_MODEL), 1.0),
        "meta_tokens": nrm(ks[1], (N_META, D_MODEL), 1.0),
        "norm1_g": gain(ks[2], (DEPTH, D_MODEL)),
        "w_in": nrm(ks[3], (DEPTH, D_MODEL, D_IN_PROJ), D_MODEL ** -0.5),
        "pool_w": nrm(ks[4], (DEPTH, len(POOL_WINDOWS), POOL_GROUP, POOL_GROUP), POOL_GROUP ** -0.5),
        "pool_scale": gain(ks[5], (DEPTH, D_POOL)),
        "conv_w": nrm(ks[6], (DEPTH, CONV_WIDTH, D_CONV), CONV_WIDTH ** -0.5),
        "conv_b": nrm(ks[7], (DEPTH, D_CONV), 0.02),
        "dt_bias": dt_bias,
        "a_log": a_log,
        "d_skip": gain(ks[11], (DEPTH, SSM_HEADS)),
        "ssm_norm_g": gain(ks[12], (DEPTH, D_SSM)),
        "w_out": nrm(ks[13], (DEPTH, D_MIX, D_MODEL), D_MIX ** -0.5),
        "norm2_g": gain(ks[14], (DEPTH, D_MODEL)),
        "router_w": nrm(ks[15], (DEPTH, D_MODEL, N_EXPERTS), D_MODEL ** -0.5),
        "router_b": nrm(ks[16], (DEPTH, N_EXPERTS), 0.01),
        "w_gate": nrm(ks[17], (DEPTH, N_EXPERTS, D_MODEL, D_EXPERT), D_MODEL ** -0.5),
        "b_gate": nrm(ks[18], (DEPTH, N_EXPERTS, D_EXPERT), 0.02),
        "w_up": nrm(ks[19], (DEPTH, N_EXPERTS, D_MODEL, D_EXPERT), D_MODEL ** -0.5),
        "b_up": nrm(ks[20], (DEPTH, N_EXPERTS, D_EXPERT), 0.02),
        "w_down": nrm(ks[21], (DEPTH, N_EXPERTS, D_EXPERT, D_MODEL), D_EXPERT ** -0.5),
        "b_down": nrm(ks[22], (DEPTH, N_EXPERTS, D_MODEL), 0.02),
        "final_g": gain(ks[23], (D_MODEL,)),
    }


def reference(x, meta_tokens, norm1_g, w_in, pool_w, pool_scale, conv_w, conv_b,
              dt_bias, a_log, d_skip, ssm_norm_g, w_out, norm2_g, router_w, router_b,
              w_gate, b_gate, w_up, b_up, w_down, b_down, final_g):
    b = x.shape[0]
    meta = jnp.broadcast_to(meta_tokens[None].astype(x.dtype), (b, N_META, D_MODEL))
    h = jnp.concatenate([meta, x], axis=1)
    splits = [D_POOL, D_POOL + D_SSM, D_POOL + D_SSM + D_CONV]
    for i in range(DEPTH):
        hn = rms_norm(h, norm1_g[i])
        proj = hn @ w_in[i]
        u_pool, z, xbc, dt_raw = jnp.split(proj, splits, axis=-1)
        y_pool = multiscale_pool(u_pool, pool_w[i], pool_scale[i])
        y_ssm = mamba2_mixer(z, xbc, dt_raw, conv_w[i], conv_b[i], dt_bias[i],
                             a_log[i], d_skip[i], ssm_norm_g[i])
        h = h + jnp.concatenate([y_pool, y_ssm], axis=-1) @ w_out[i]
        if i == DEPTH - 1:
            h = h[:, N_META:]
        h = h + moe_ffn(rms_norm(h, norm2_g[i]), router_w[i], router_b[i], w_gate[i],
                        b_gate[i], w_up[i], b_up[i], w_down[i], b_down[i])
    return rms_norm(h, final_g)
```

```python
import functools

import jax
import jax.numpy as jnp
from jax import lax
from jax.experimental import pallas as pl
from jax.experimental.pallas import tpu as pltpu

f32 = jnp.float32
bf16 = jnp.bfloat16
i32 = jnp.int32

D_MODEL = 2048
SEQ = 8192
CHUNK = 64
N_META = 16
D_POOL = 1024
POOL_WINDOWS = (2, 4, 8, 16)
POOL_GROUP = 256
D_SSM = 1024
HEAD_DIM = 64
N_HEADS = 16
N_GROUPS = 2
HEADS_PER_GROUP = N_HEADS // N_GROUPS
N_STATE = 128
GROUP_COLS = HEADS_PER_GROUP * HEAD_DIM
D_CONV = D_SSM + 2 * N_GROUPS * N_STATE
D_MAIN = D_POOL + D_SSM + D_CONV
N_EXPERTS = 32
TOP_K = 4
D_EXPERT = 2048
SWIGLU_LIMIT = 7.0
SWIGLU_ALPHA = 1.702
NORM_EPS = 1e-5

POOL_HALO = 16
CONV_HALO = 8

MOE_SUB = 256
MOE_ROWS = 2048
MOE_SUBS = MOE_ROWS // MOE_SUB
EXPERT_CAP = SEQ + MOE_ROWS
MOE_TN = 256
MOE_NJ = D_EXPERT // MOE_TN
MOE_WORK = (SEQ * TOP_K) // MOE_ROWS + N_EXPERTS
ZERO_ROWS = MOE_SUB + 8

VMEM_LIMIT = 56 * 1024 * 1024


def _cparams(sem):
    return pltpu.CompilerParams(dimension_semantics=sem, vmem_limit_bytes=VMEM_LIMIT)


def _inproj_kernel(x_ref, g_ref, w_ref, wdt_ref, proj_ref, dt_ref, hn_ref):
    j = pl.program_id(1)

    @pl.when(j == 0)
    def _():
        x = x_ref[...]
        y = x * lax.rsqrt(jnp.mean(x * x, axis=-1, keepdims=True) + NORM_EPS)
        hn = (y * g_ref[...]).astype(bf16)
        hn_ref[...] = hn
        dt_ref[...] = jnp.dot(hn, wdt_ref[...].astype(bf16), preferred_element_type=f32)

    proj_ref[...] = jnp.dot(hn_ref[...], w_ref[...].astype(bf16),
                            preferred_element_type=f32)


def _inproj(x2d, g, w_in, w_dt, tm, tn=512):
    m = x2d.shape[0]
    return pl.pallas_call(
        _inproj_kernel,
        grid=(m // tm, D_MAIN // tn),
        in_specs=[
            pl.BlockSpec((tm, D_MODEL), lambda i, j: (i, 0)),
            pl.BlockSpec((1, D_MODEL), lambda i, j: (0, 0)),
            pl.BlockSpec((D_MODEL, tn), lambda i, j: (0, j)),
            pl.BlockSpec((D_MODEL, N_HEADS), lambda i, j: (0, 0)),
        ],
        out_specs=[
            pl.BlockSpec((tm, tn), lambda i, j: (i, j)),
            pl.BlockSpec((tm, N_HEADS), lambda i, j: (i, 0)),
        ],
        out_shape=[
            jax.ShapeDtypeStruct((m, D_MAIN), f32),
            jax.ShapeDtypeStruct((m, N_HEADS), f32),
        ],
        scratch_shapes=[pltpu.VMEM((tm, D_MODEL), bf16)],
        compiler_params=_cparams(("arbitrary", "arbitrary")),
        name="inproj",
    )(x2d, g, w_in, w_dt)


def _split_dot(q, e):
    hi = q.astype(bf16)
    lo = (q - hi.astype(f32)).astype(bf16)
    return (jnp.dot(hi, e, preferred_element_type=f32)
            + jnp.dot(lo, e, preferred_element_type=f32))


def _nt_dot(a, b):
    return lax.dot_general(a, b, (((1,), (1,)), ((), ())), preferred_element_type=f32)


def _mixer_kernel(u_ref, z_ref, xs_ref, bc_ref, dt_ref,
                  poolw_ref, pscale_ref, convw_ref, convb_ref, dtb_ref, alog_ref,
                  dskip_ref, ng_ref, expand_ref, state0_ref, uhalo0_ref, chalo0_ref,
                  y_ref, state_out_ref, utail_ref, ctail_ref,
                  state, uext, cext, *, tm, lead):
    i = pl.program_id(0)
    nch = tm // CHUNK

    @pl.when(i == 0)
    def _():
        state[...] = state0_ref[...]
        uext[0:POOL_HALO, :] = uhalo0_ref[...]
        cext[0:CONV_HALO, :] = chalo0_ref[...]

    uext[POOL_HALO:, :] = u_ref[...]
    cext[CONV_HALO:, 0:D_SSM] = xs_ref[...]
    cext[CONV_HALO:, D_SSM:] = bc_ref[...]

    for g, w in enumerate(POOL_WINDOWS):
        cols = slice(g * POOL_GROUP, (g + 1) * POOL_GROUP)
        ue = uext[:, cols]
        s = ue
        sh = 1
        while sh < w:
            s = s + pltpu.roll(s, sh, axis=0)
            sh *= 2
        d = s[POOL_HALO:, :] / float(w) - ue[POOL_HALO:, :]
        yp = jnp.dot(d.astype(bf16), poolw_ref[g].astype(bf16), preferred_element_type=f32)
        y_ref[:, cols] = (yp * pscale_ref[:, cols]).astype(bf16)

    ce = cext[...]
    cw = convw_ref[...]
    acc = (ce * cw[3:4, :] + pltpu.roll(ce, 1, axis=0) * cw[2:3, :]
           + pltpu.roll(ce, 2, axis=0) * cw[1:2, :] + pltpu.roll(ce, 3, axis=0) * cw[0:1, :])
    acc = acc[CONV_HALO:, :] + convb_ref[...]
    xbc = acc * jax.nn.sigmoid(acc)
    xs = xbc[:, 0:D_SSM]
    bm = xbc[:, D_SSM:D_SSM + N_GROUPS * N_STATE]
    cm = xbc[:, D_SSM + N_GROUPS * N_STATE:]

    uext[0:POOL_HALO, :] = uext[tm:tm + POOL_HALO, :]
    cext[0:CONV_HALO, :] = cext[tm:tm + CONV_HALO, :]

    v = dt_ref[...] + dtb_ref[...]
    dt = jnp.maximum(v, 0.0) + jnp.log1p(jnp.exp(-jnp.abs(v)))
    row = lax.broadcasted_iota(i32, (tm, N_HEADS), 0)
    if lead:
        dt = jnp.where(row >= lead, dt, 0.0)
    a = -jnp.exp(alog_ref[...])
    acs = dt * a
    rin = row % CHUNK
    sh = 1
    while sh < CHUNK:
        acs = acs + jnp.where(rin >= sh, pltpu.roll(acs, sh, axis=0), 0.0)
        sh *= 2
    alast = jnp.concatenate(
        [jnp.broadcast_to(acs[c * CHUNK + CHUNK - 1:c * CHUNK + CHUNK, :], (CHUNK, N_HEADS))
         for c in range(nch)], axis=0)
    dec_st = jnp.exp(alast - acs)
    dec_in = jnp.exp(acs)

    expand = expand_ref[...]
    dt_x = _split_dot(dt, expand)
    dtds_x = _split_dot(dt * dec_st, expand)
    decin_x = _split_dot(dec_in, expand)

    eye_h = (lax.broadcasted_iota(i32, (N_HEADS, N_HEADS), 0)
             == lax.broadcasted_iota(i32, (N_HEADS, N_HEADS), 1)).astype(bf16)
    p1 = acs.astype(bf16)
    r1 = acs - p1.astype(f32)
    p2 = r1.astype(bf16)
    p3 = (r1 - p2.astype(f32)).astype(bf16)
    acs_t = _nt_dot(eye_h, p1) + _nt_dot(eye_h, p2) + _nt_dot(eye_h, p3)

    xdt = (xs * dt_x).astype(bf16)
    xds = (xs * dtds_x).astype(bf16)

    eye_n = (lax.broadcasted_iota(i32, (N_STATE, N_STATE), 0)
             == lax.broadcasted_iota(i32, (N_STATE, N_STATE), 1)).astype(bf16)
    causal = (lax.broadcasted_iota(i32, (CHUNK, CHUNK), 0)
              >= lax.broadcasted_iota(i32, (CHUNK, CHUNK), 1))
    col_head = lax.broadcasted_iota(i32, (CHUNK, GROUP_COLS), 1) // HEAD_DIM

    y_chunks = []
    for c in range(nch):
        r0 = c * CHUNK
        y_groups = []
        for g in range(N_GROUPS):
            gc = slice(g * GROUP_COLS, (g + 1) * GROUP_COLS)
            cg = cm[r0:r0 + CHUNK, g * N_STATE:(g + 1) * N_STATE].astype(bf16)
            bg = bm[r0:r0 + CHUNK, g * N_STATE:(g + 1) * N_STATE].astype(bf16)
            gmat = _nt_dot(cg, bg)
            st = state[g]
            y_off = (jnp.dot(cg, st.astype(bf16), preferred_element_type=f32)
                     * decin_x[r0:r0 + CHUNK, gc])
            scores = []
            for h in range(HEADS_PER_GROUP):
                hh = g * HEADS_PER_GROUP + h
                a_col = acs[r0:r0 + CHUNK, hh:hh + 1]
                a_row = acs_t[hh:hh + 1, r0:r0 + CHUNK]
                lmat = jnp.exp(jnp.where(causal, a_col - a_row, -jnp.inf))
                scores.append((gmat * lmat).astype(bf16))
            sc = jnp.concatenate(scores, axis=0)
            full = jnp.dot(sc, xdt[r0:r0 + CHUNK, gc], preferred_element_type=f32)
            y_diag = jnp.zeros((CHUNK, GROUP_COLS), f32)
            for h in range(HEADS_PER_GROUP):
                y_diag = jnp.where(col_head == h, full[h * CHUNK:(h + 1) * CHUNK, :], y_diag)
            y_groups.append(y_diag + y_off)
            bg_t = _nt_dot(eye_n, bg).astype(bf16)
            upd = jnp.dot(bg_t, xds[r0:r0 + CHUNK, gc], preferred_element_type=f32)
            state[g] = st * decin_x[r0 + CHUNK - 1:r0 + CHUNK, gc] + upd
        y_chunks.append(jnp.concatenate(y_groups, axis=1))
    y = jnp.concatenate(y_chunks, axis=0) if nch > 1 else y_chunks[0]

    y = y + dskip_ref[...] * xs
    zz = z_ref[...]
    y = y * (zz * jax.nn.sigmoid(zz))
    for g in range(N_GROUPS):
        gc = slice(g * GROUP_COLS, (g + 1) * GROUP_COLS)
        yg = y[:, gc]
        yg = yg * lax.rsqrt(jnp.mean(yg * yg, axis=-1, keepdims=True) + NORM_EPS)
        y_ref[:, D_POOL + g * GROUP_COLS:D_POOL + (g + 1) * GROUP_COLS] = (
            yg * ng_ref[:, gc]).astype(bf16)

    state_out_ref[...] = state[...]
    utail_ref[...] = uext[0:POOL_HALO, :]
    ctail_ref[...] = cext[0:CONV_HALO, :]


def _mixer(proj, dt_raw, params, state0, uhalo0, chalo0, tm, lead):
    m = proj.shape[0]
    (pool_w, pool_scale, conv_w, conv_b, dt_bias, a_log, dskip_x, norm_g, expand) = params
    full = lambda shape: pl.BlockSpec(shape, lambda i: (0,) * len(shape))
    return pl.pallas_call(
        functools.partial(_mixer_kernel, tm=tm, lead=lead),
        grid=(m // tm,),
        in_specs=[
            pl.BlockSpec((tm, D_POOL), lambda i: (i, 0)),
            pl.BlockSpec((tm, D_SSM), lambda i: (i, 1)),
            pl.BlockSpec((tm, D_SSM), lambda i: (i, 2)),
            pl.BlockSpec((tm, 2 * N_GROUPS * N_STATE), lambda i: (i, 6)),
            pl.BlockSpec((tm, N_HEADS), lambda i: (i, 0)),
            full((len(POOL_WINDOWS), POOL_GROUP, POOL_GROUP)),
            full((1, D_POOL)),
            full((4, D_CONV)),
            full((1, D_CONV)),
            full((1, N_HEADS)),
            full((1, N_HEADS)),
            full((1, D_SSM)),
            full((1, D_SSM)),
            full((N_HEADS, D_SSM)),
            full((N_GROUPS, N_STATE, GROUP_COLS)),
            full((POOL_HALO, D_POOL)),
            full((CONV_HALO, D_CONV)),
        ],
        out_specs=[
            pl.BlockSpec((tm, D_MODEL), lambda i: (i, 0)),
            full((N_GROUPS, N_STATE, GROUP_COLS)),
            full((POOL_HALO, D_POOL)),
            full((CONV_HALO, D_CONV)),
        ],
        out_shape=[
            jax.ShapeDtypeStruct((m, D_MODEL), bf16),
            jax.ShapeDtypeStruct((N_GROUPS, N_STATE, GROUP_COLS), f32),
            jax.ShapeDtypeStruct((POOL_HALO, D_POOL), f32),
            jax.ShapeDtypeStruct((CONV_HALO, D_CONV), f32),
        ],
        scratch_shapes=[
            pltpu.VMEM((N_GROUPS, N_STATE, GROUP_COLS), f32),
            pltpu.VMEM((tm + POOL_HALO, D_POOL), f32),
            pltpu.VMEM((tm + CONV_HALO, D_CONV), f32),
        ],
        compiler_params=_cparams(("arbitrary",)),
        name="mixer",
    )(proj, proj, proj, proj, dt_raw, pool_w, pool_scale, conv_w, conv_b, dt_bias, a_log,
      dskip_x, norm_g, expand, state0, uhalo0, chalo0)


def _outproj_kernel(y_ref, w_ref, x_ref, g2_ref, rw_ref, rb_ref,
                    h2_ref, hn_ref, pos_ref, gate_ref, cnt_ref,
                    hbuf, cnt, *, tm, tn):
    i = pl.program_id(0)
    j = pl.program_id(1)
    nj = D_MODEL // tn

    @pl.when((i == 0) & (j == 0))
    def _():
        cnt[...] = jnp.zeros_like(cnt)

    h = x_ref[...] + jnp.dot(y_ref[...], w_ref[...].astype(bf16), preferred_element_type=f32)
    h2_ref[...] = h
    hbuf[j] = h

    @pl.when(j == nj - 1)
    def _():
        ss = jnp.zeros((tm, 1), f32)
        for q in range(nj):
            hq = hbuf[q]
            ss = ss + jnp.sum(hq * hq, axis=-1, keepdims=True)
        inv = lax.rsqrt(ss / float(D_MODEL) + NORM_EPS)
        logits = jnp.zeros((tm, N_EXPERTS), f32) + rb_ref[...]
        for q in range(nj):
            hn = hbuf[q] * inv * g2_ref[:, q * tn:(q + 1) * tn]
            hn_ref[:, q * tn:(q + 1) * tn] = hn
            logits = logits + jnp.dot(hn.astype(bf16),
                                      rw_ref[q * tn:(q + 1) * tn, :].astype(bf16),
                                      preferred_element_type=f32)

        lane = lax.broadcasted_iota(i32, (tm, N_EXPERTS), 1).astype(f32)
        vals = logits
        picked = jnp.zeros((tm, N_EXPERTS), f32)
        tops, idxs = [], []
        for _ in range(TOP_K):
            mx = jnp.max(vals, axis=-1, keepdims=True)
            idx = jnp.min(jnp.where(vals == mx, lane, float(N_EXPERTS)), axis=-1, keepdims=True)
            hit = lane == idx
            tops.append(mx)
            idxs.append(idx)
            picked = picked + hit.astype(f32)
            vals = jnp.where(hit, -jnp.inf, vals)
        exps = [jnp.exp(t - tops[0]) for t in tops]
        denom = exps[0] + exps[1] + exps[2] + exps[3]

        tri = (lax.broadcasted_iota(i32, (tm, tm), 0)
               > lax.broadcasted_iota(i32, (tm, tm), 1)).astype(bf16)
        cum = jnp.dot(tri, picked.astype(bf16), preferred_element_type=f32) + cnt[...]
        lane4 = lax.broadcasted_iota(i32, (tm, TOP_K), 1)
        pos = jnp.zeros((tm, TOP_K), i32)
        gates = jnp.zeros((tm, TOP_K), f32)
        for k in range(TOP_K):
            rank = jnp.sum(jnp.where(lane == idxs[k], cum, 0.0), axis=-1, keepdims=True)
            p = idxs[k].astype(i32) * EXPERT_CAP + rank.astype(i32)
            pos = jnp.where(lane4 == k, p, pos)
            gates = jnp.where(lane4 == k, exps[k] / denom, gates)
        pos_ref[...] = pos
        gate_ref[...] = gates
        cnt[...] = cnt[...] + jnp.sum(picked, axis=0, keepdims=True)
        cnt_ref[...] = cnt[...]


def _outproj(ymix, w_out, x2d, g2, rw, rb, tm=1024, tn=512):
    m = x2d.shape[0]
    return pl.pallas_call(
        functools.partial(_outproj_kernel, tm=tm, tn=tn),
        grid=(m // tm, D_MODEL // tn),
        in_specs=[
            pl.BlockSpec((tm, D_MODEL), lambda i, j: (i, 0)),
            pl.BlockSpec((D_MODEL, tn), lambda i, j: (0, j)),
            pl.BlockSpec((tm, tn), lambda i, j: (i, j)),
            pl.BlockSpec((1, D_MODEL), lambda i, j: (0, 0)),
            pl.BlockSpec((D_MODEL, N_EXPERTS), lambda i, j: (0, 0)),
            pl.BlockSpec((1, N_EXPERTS), lambda i, j: (0, 0)),
        ],
        out_specs=[
            pl.BlockSpec((tm, tn), lambda i, j: (i, j)),
            pl.BlockSpec((tm, D_MODEL), lambda i, j: (i, 0)),
            pl.BlockSpec((tm, TOP_K), lambda i, j: (i, 0)),
            pl.BlockSpec((tm, TOP_K), lambda i, j: (i, 0)),
            pl.BlockSpec((1, N_EXPERTS), lambda i, j: (0, 0)),
        ],
        out_shape=[
            jax.ShapeDtypeStruct((m, D_MODEL), f32),
            jax.ShapeDtypeStruct((m, D_MODEL), f32),
            jax.ShapeDtypeStruct((m, TOP_K), i32),
            jax.ShapeDtypeStruct((m, TOP_K), f32),
            jax.ShapeDtypeStruct((1, N_EXPERTS), f32),
        ],
        scratch_shapes=[
            pltpu.VMEM((D_MODEL // tn, tm, tn), f32),
            pltpu.VMEM((1, N_EXPERTS), f32),
        ],
        compiler_params=_cparams(("arbitrary", "arbitrary")),
        name="outproj_router",
    )(ymix, w_out, x2d, g2, rw, rb)


def _scatter_kernel(pos_ref, cnt_ref, hn_ref, xs_ref, zbuf, sem, zsem, *, tb):
    i = pl.program_id(0)

    @pl.when(i == 0)
    def _():
        zbuf[...] = jnp.zeros_like(zbuf)
        for e in range(N_EXPERTS):
            start = pl.multiple_of(e * EXPERT_CAP + (cnt_ref[e] // 8) * 8, 8)
            pltpu.make_async_copy(zbuf, xs_ref.at[pl.ds(start, ZERO_ROWS)], zsem).start()
        for e in range(N_EXPERTS):
            start = pl.multiple_of(e * EXPERT_CAP + (cnt_ref[e] // 8) * 8, 8)
            pltpu.make_async_copy(zbuf, xs_ref.at[pl.ds(start, ZERO_ROWS)], zsem).wait()

    def body(t, carry):
        tok = i * tb + t
        for k in range(TOP_K):
            p = pos_ref[tok * TOP_K + k]
            pltpu.make_async_copy(hn_ref.at[pl.ds(tok, 1)], xs_ref.at[pl.ds(p, 1)], sem).start()
        return carry

    lax.fori_loop(0, tb, body, 0)

    def wbody(t, carry):
        tok = i * tb + t
        for k in range(TOP_K):
            p = pos_ref[tok * TOP_K + k]
            pltpu.make_async_copy(hn_ref.at[pl.ds(tok, 1)], xs_ref.at[pl.ds(p, 1)], sem).wait()
        return carry

    lax.fori_loop(0, tb, wbody, 0)


def _scatter(pos_flat, cnt_i, hn2, tb=512):
    m = hn2.shape[0]
    return pl.pallas_call(
        functools.partial(_scatter_kernel, tb=tb),
        grid_spec=pltpu.PrefetchScalarGridSpec(
            num_scalar_prefetch=2,
            grid=(m // tb,),
            in_specs=[pl.BlockSpec(memory_space=pl.ANY)],
            out_specs=pl.BlockSpec(memory_space=pl.ANY),
            scratch_shapes=[
                pltpu.VMEM((ZERO_ROWS, D_MODEL), f32),
                pltpu.SemaphoreType.DMA(()),
                pltpu.SemaphoreType.DMA(()),
            ],
        ),
        out_shape=jax.ShapeDtypeStruct((N_EXPERTS * EXPERT_CAP, D_MODEL), f32),
        compiler_params=_cparams(("arbitrary",)),
        name="scatter_rows",
    )(pos_flat, cnt_i, hn2)


def _moe_kernel(we_ref, wq_ref, wn_ref,
                xs_ref, wg_ref, wu_ref, wd_ref, bg_ref, bu_ref, bd_ref,
                o_ref, stage, xb, abuf, ws_a, ws_b, sem):
    w = pl.program_id(0)
    j = pl.program_id(1)
    n = wn_ref[w]
    base = we_ref[w] * EXPERT_CAP + wq_ref[w] * MOE_ROWS

    def x_copy(s):
        return pltpu.make_async_copy(
            xs_ref.at[pl.ds(pl.multiple_of(base + s * MOE_SUB, MOE_SUB), MOE_SUB)],
            stage.at[s % 2], sem.at[s % 2])

    @pl.when((j == 0) & (n > 0))
    def _():
        for s in range(2):
            @pl.when(s < n)
            def _():
                x_copy(s).start()
        for s in range(MOE_SUBS):
            @pl.when(s < n)
            def _():
                x_copy(s).wait()
                xb[pl.ds(s * MOE_SUB, MOE_SUB), :] = stage[s % 2].astype(bf16)
                if s + 2 < MOE_SUBS:
                    @pl.when(s + 2 < n)
                    def _():
                        x_copy(s + 2).start()

    @pl.when((j < MOE_NJ) & (n > 0))
    def _():
        ws_a[...] = wg_ref[0].astype(bf16)
        ws_b[...] = wu_ref[0].astype(bf16)
        bgv = bg_ref[0]
        buv = bu_ref[0]

        def body(s, carry):
            rows = pl.ds(pl.multiple_of(s * MOE_SUB, MOE_SUB), MOE_SUB)
            x = xb[rows, :]
            g = jnp.dot(x, ws_a[...], preferred_element_type=f32) + bgv
            u = jnp.dot(x, ws_b[...], preferred_element_type=f32) + buv
            g = jnp.minimum(g, SWIGLU_LIMIT)
            u = jnp.clip(u, -SWIGLU_LIMIT, SWIGLU_LIMIT)
            act = (u + 1.0) * (g * jax.nn.sigmoid(SWIGLU_ALPHA * g))
            abuf[j, rows, :] = act.astype(bf16)
            return carry

        lax.fori_loop(0, n, body, 0)

    @pl.when((j >= MOE_NJ) & (n > 0))
    def _():
        ws_a[...] = wd_ref[0].astype(bf16)
        bdv = bd_ref[0]

        def body(s, carry):
            rows = pl.ds(pl.multiple_of(s * MOE_SUB, MOE_SUB), MOE_SUB)
            acc = jnp.zeros((MOE_SUB, MOE_TN), f32) + bdv
            for q in range(MOE_NJ):
                acc = acc + jnp.dot(abuf[q, rows, :], ws_a[q * MOE_TN:(q + 1) * MOE_TN, :],
                                    preferred_element_type=f32)
            o_ref[rows, :] = acc
            return carry

        lax.fori_loop(0, n, body, 0)


def _moe(we, wq, wn, xs, w_gate, w_up, w_down, b_gate, b_up, b_down):
    blocks_per_expert = EXPERT_CAP // MOE_ROWS

    def w_a(w, j, we, wq, wn):
        return (we[w], 0, jnp.minimum(j, MOE_NJ - 1))

    def w_b(w, j, we, wq, wn):
        return (we[w], 0, jnp.maximum(j - MOE_NJ, 0))

    def o_map(w, j, we, wq, wn):
        jj = jnp.where(wn[w] > 0, jnp.maximum(j - MOE_NJ, 0), MOE_NJ - 1)
        return (we[w] * blocks_per_expert + wq[w], jj)

    return pl.pallas_call(
        _moe_kernel,
        grid_spec=pltpu.PrefetchScalarGridSpec(
            num_scalar_prefetch=3,
            grid=(MOE_WORK, 2 * MOE_NJ),
            in_specs=[
                pl.BlockSpec(memory_space=pl.ANY),
                pl.BlockSpec((1, D_MODEL, MOE_TN), w_a),
                pl.BlockSpec((1, D_MODEL, MOE_TN), w_a),
                pl.BlockSpec((1, D_EXPERT, MOE_TN), w_b),
                pl.BlockSpec((1, 1, MOE_TN), w_a),
                pl.BlockSpec((1, 1, MOE_TN), w_a),
                pl.BlockSpec((1, 1, MOE_TN), w_b),
            ],
            out_specs=pl.BlockSpec((MOE_ROWS, MOE_TN), o_map),
            scratch_shapes=[
                pltpu.VMEM((2, MOE_SUB, D_MODEL), f32),
                pltpu.VMEM((MOE_ROWS, D_MODEL), bf16),
                pltpu.VMEM((MOE_NJ, MOE_ROWS, MOE_TN), bf16),
                pltpu.VMEM((D_MODEL, MOE_TN), bf16),
                pltpu.VMEM((D_MODEL, MOE_TN), bf16),
                pltpu.SemaphoreType.DMA((2,)),
            ],
        ),
        out_shape=jax.ShapeDtypeStruct((N_EXPERTS * EXPERT_CAP, D_MODEL), f32),
        compiler_params=_cparams(("arbitrary", "arbitrary")),
        name="moe_experts",
    )(we, wq, wn, xs, w_gate, w_up, w_down,
      b_gate.reshape(N_EXPERTS, 1, D_EXPERT), b_up.reshape(N_EXPERTS, 1, D_EXPERT),
      b_down.reshape(N_EXPERTS, 1, D_MODEL))


def _work_list(cnt_i):
    nq = (cnt_i + MOE_ROWS - 1) // MOE_ROWS
    ends = jnp.cumsum(nq)
    total = ends[-1]
    w = jnp.arange(MOE_WORK, dtype=i32)
    wc = jnp.minimum(w, total - 1)
    e = jnp.minimum(jnp.sum((ends[None, :] <= wc[:, None]).astype(i32), axis=1), N_EXPERTS - 1)
    q = wc - (ends[e] - nq[e])
    rows = jnp.clip(cnt_i[e] - q * MOE_ROWS, 0, MOE_ROWS)
    n = jnp.where(w < total, (rows + MOE_SUB - 1) // MOE_SUB, 0)
    return e.astype(i32), q.astype(i32), n.astype(i32)


def _combine_kernel(pos_ref, o_ref, gate_ref, h2_ref, fg_ref, out_ref, obuf, sem, *, tm):
    i = pl.program_id(0)

    def body(t, carry):
        tok = i * tm + t
        for k in range(TOP_K):
            p = pos_ref[tok * TOP_K + k]
            pltpu.make_async_copy(o_ref.at[pl.ds(p, 1)], obuf.at[pl.ds(k * tm + t, 1)], sem).start()
        return carry

    lax.fori_loop(0, tm, body, 0)

    def wbody(t, carry):
        tok = i * tm + t
        for k in range(TOP_K):
            p = pos_ref[tok * TOP_K + k]
            pltpu.make_async_copy(o_ref.at[pl.ds(p, 1)], obuf.at[pl.ds(k * tm + t, 1)], sem).wait()
        return carry

    lax.fori_loop(0, tm, wbody, 0)

    gates = gate_ref[...]
    y = h2_ref[...]
    for k in range(TOP_K):
        y = y + gates[:, k:k + 1] * obuf[k * tm:(k + 1) * tm, :]
    out = y * lax.rsqrt(jnp.mean(y * y, axis=-1, keepdims=True) + NORM_EPS)
    out_ref[...] = out * fg_ref[...]


def _combine(pos_flat, o, gates, h2, final_g, tm=256):
    m = h2.shape[0]
    return pl.pallas_call(
        functools.partial(_combine_kernel, tm=tm),
        grid_spec=pltpu.PrefetchScalarGridSpec(
            num_scalar_prefetch=1,
            grid=(m // tm,),
            in_specs=[
                pl.BlockSpec(memory_space=pl.ANY),
                pl.BlockSpec((tm, TOP_K), lambda i, pos: (i, 0)),
                pl.BlockSpec((tm, D_MODEL), lambda i, pos: (i, 0)),
                pl.BlockSpec((1, D_MODEL), lambda i, pos: (0, 0)),
            ],
            out_specs=pl.BlockSpec((tm, D_MODEL), lambda i, pos: (i, 0)),
            scratch_shapes=[
                pltpu.VMEM((TOP_K * tm, D_MODEL), f32),
                pltpu.SemaphoreType.DMA(()),
            ],
        ),
        out_shape=jax.ShapeDtypeStruct((m, D_MODEL), f32),
        compiler_params=_cparams(("arbitrary",)),
        name="combine",
    )(pos_flat, o, gates, h2, final_g)


def kernel(x, meta_tokens, norm1_g, w_in, pool_w, pool_scale, conv_w, conv_b, dt_bias, a_log,
           d_skip, ssm_norm_g, w_out, norm2_g, router_w, router_b, w_gate, b_gate, w_up, b_up,
           w_down, b_down, final_g):
    assert x.shape == (1, SEQ, D_MODEL)
    x2d = x.reshape(SEQ, D_MODEL)
    lead = CHUNK - N_META
    meta_chunk = jnp.concatenate([jnp.zeros((lead, D_MODEL), f32), meta_tokens.astype(f32)], 0)

    g1 = norm1_g[0].reshape(1, D_MODEL)
    w_in0 = w_in[0]
    w_dt = w_in0[:, D_MAIN:]
    head_of_col = jnp.arange(D_SSM, dtype=i32) // HEAD_DIM
    expand = (head_of_col[None, :] == jnp.arange(N_HEADS, dtype=i32)[:, None]).astype(bf16)
    mix_params = (
        pool_w[0], pool_scale[0].reshape(1, D_POOL), conv_w[0], conv_b[0].reshape(1, D_CONV),
        dt_bias[0].reshape(1, N_HEADS), a_log[0].reshape(1, N_HEADS),
        jnp.repeat(d_skip[0], HEAD_DIM).reshape(1, D_SSM), ssm_norm_g[0].reshape(1, D_SSM),
        expand)

    proj_m, dt_m = _inproj(meta_chunk, g1, w_in0, w_dt, tm=CHUNK)
    zeros_state = jnp.zeros((N_GROUPS, N_STATE, GROUP_COLS), f32)
    _, state_m, utail_m, ctail_m = _mixer(
        proj_m, dt_m, mix_params, zeros_state, jnp.zeros((POOL_HALO, D_POOL), f32),
        jnp.zeros((CONV_HALO, D_CONV), f32), tm=CHUNK, lead=lead)

    proj, dt_raw = _inproj(x2d, g1, w_in0, w_dt, tm=1024)
    ymix, _, _, _ = _mixer(proj, dt_raw, mix_params, state_m, utail_m, ctail_m, tm=256, lead=0)

    h2, hn2, pos, gates, cnt = _outproj(
        ymix, w_out[0], x2d, norm2_g[0].reshape(1, D_MODEL), router_w[0],
        router_b[0].reshape(1, N_EXPERTS))

    pos_flat = pos.reshape(SEQ * TOP_K)
    cnt_i = cnt.reshape(N_EXPERTS).astype(i32)
    xs = _scatter(pos_flat, cnt_i, hn2)
    we, wq, wn = _work_list(cnt_i)
    o = _moe(we, wq, wn, xs, w_gate[0], w_up[0], w_down[0], b_gate[0], b_up[0], b_down[0])
    out = _combine(pos_flat, o, gates, h2, final_g.reshape(1, D_MODEL))
    return out.reshape(1, SEQ, D_MODEL)
```

```python
import functools

import jax
import jax.numpy as jnp
from jax import lax
from jax.experimental import pallas as pl
from jax.experimental.pallas import tpu as pltpu

f32 = jnp.float32
bf16 = jnp.bfloat16
i32 = jnp.int32

D_MODEL = 2048
SEQ = 8192
CHUNK = 64
N_META = 16
D_POOL = 1024
POOL_WINDOWS = (2, 4, 8, 16)
POOL_GROUP = 256
D_SSM = 1024
HEAD_DIM = 64
N_HEADS = 16
N_GROUPS = 2
HEADS_PER_GROUP = N_HEADS // N_GROUPS
N_STATE = 128
GROUP_COLS = HEADS_PER_GROUP * HEAD_DIM
D_CONV = D_SSM + 2 * N_GROUPS * N_STATE
D_MAIN = D_POOL + D_SSM + D_CONV
N_EXPERTS = 32
TOP_K = 4
D_EXPERT = 2048
SWIGLU_LIMIT = 7.0
SWIGLU_ALPHA = 1.702
NORM_EPS = 1e-5

POOL_HALO = 16
CONV_HALO = 8

MOE_SUB = 256
MOE_ROWS = 2048
MOE_SUBS = MOE_ROWS // MOE_SUB
EXPERT_CAP = SEQ + MOE_ROWS
MOE_TN = 256
MOE_NJ = D_EXPERT // MOE_TN
MOE_WORK = (SEQ * TOP_K) // MOE_ROWS + N_EXPERTS
ZERO_ROWS = MOE_SUB + 8

VMEM_LIMIT = 56 * 1024 * 1024


def _cparams(sem):
    return pltpu.CompilerParams(dimension_semantics=sem, vmem_limit_bytes=VMEM_LIMIT)


def _inproj_kernel(x_ref, g_ref, w_ref, wdt_ref, proj_ref, dt_ref, hn_ref):
    j = pl.program_id(1)

    @pl.when(j == 0)
    def _():
        x = x_ref[...]
        y = x * lax.rsqrt(jnp.mean(x * x, axis=-1, keepdims=True) + NORM_EPS)
        hn = (y * g_ref[...]).astype(bf16)
        hn_ref[...] = hn
        dt_ref[...] = jnp.dot(hn, wdt_ref[...].astype(bf16), preferred_element_type=f32)

    proj_ref[...] = jnp.dot(hn_ref[...], w_ref[...].astype(bf16),
                            preferred_element_type=f32)


def _inproj(x2d, g, w_in, w_dt, tm, tn=512):
    m = x2d.shape[0]
    return pl.pallas_call(
        _inproj_kernel,
        grid=(m // tm, D_MAIN // tn),
        in_specs=[
            pl.BlockSpec((tm, D_MODEL), lambda i, j: (i, 0)),
            pl.BlockSpec((1, D_MODEL), lambda i, j: (0, 0)),
            pl.BlockSpec((D_MODEL, tn), lambda i, j: (0, j)),
            pl.BlockSpec((D_MODEL, N_HEADS), lambda i, j: (0, 0)),
        ],
        out_specs=[
            pl.BlockSpec((tm, tn), lambda i, j: (i, j)),
            pl.BlockSpec((tm, N_HEADS), lambda i, j: (i, 0)),
        ],
        out_shape=[
            jax.ShapeDtypeStruct((m, D_MAIN), f32),
            jax.ShapeDtypeStruct((m, N_HEADS), f32),
        ],
        scratch_shapes=[pltpu.VMEM((tm, D_MODEL), bf16)],
        compiler_params=_cparams(("arbitrary", "arbitrary")),
        name="inproj",
    )(x2d, g, w_in, w_dt)


def _split_dot(q, e):
    hi = q.astype(bf16)
    lo = (q - hi.astype(f32)).astype(bf16)
    return (jnp.dot(hi, e, preferred_element_type=f32)
            + jnp.dot(lo, e, preferred_element_type=f32))


def _nt_dot(a, b):
    return lax.dot_general(a, b, (((1,), (1,)), ((), ())), preferred_element_type=f32)


def _mixer_kernel(u_ref, z_ref, xs_ref, bc_ref, dt_ref,
                  poolw_ref, pscale_ref, convw_ref, convb_ref, dtb_ref, alog_ref,
                  dskip_ref, ng_ref, expand_ref, state0_ref, uhalo0_ref, chalo0_ref,
                  y_ref, state_out_ref, utail_ref, ctail_ref,
                  state, uext, cext, *, tm, lead):
    i = pl.program_id(0)
    nch = tm // CHUNK

    @pl.when(i == 0)
    def _():
        state[...] = state0_ref[...]
        uext[0:POOL_HALO, :] = uhalo0_ref[...]
        cext[0:CONV_HALO, :] = chalo0_ref[...]

    uext[POOL_HALO:, :] = u_ref[...]
    cext[CONV_HALO:, 0:D_SSM] = xs_ref[...]
    cext[CONV_HALO:, D_SSM:] = bc_ref[...]

    for g, w in enumerate(POOL_WINDOWS):
        cols = slice(g * POOL_GROUP, (g + 1) * POOL_GROUP)
        ue = uext[:, cols]
        s = ue
        sh = 1
        while sh < w:
            s = s + pltpu.roll(s, sh, axis=0)
            sh *= 2
        d = s[POOL_HALO:, :] / float(w) - ue[POOL_HALO:, :]
        yp = jnp.dot(d.astype(bf16), poolw_ref[g].astype(bf16), preferred_element_type=f32)
        y_ref[:, cols] = (yp * pscale_ref[:, cols]).astype(bf16)

    ce = cext[...]
    cw = convw_ref[...]
    acc = (ce * cw[3:4, :] + pltpu.roll(ce, 1, axis=0) * cw[2:3, :]
           + pltpu.roll(ce, 2, axis=0) * cw[1:2, :] + pltpu.roll(ce, 3, axis=0) * cw[0:1, :])
    acc = acc[CONV_HALO:, :] + convb_ref[...]
    xbc = acc * jax.nn.sigmoid(acc)
    xs = xbc[:, 0:D_SSM]
    bm = xbc[:, D_SSM:D_SSM + N_GROUPS * N_STATE]
    cm = xbc[:, D_SSM + N_GROUPS * N_STATE:]

    uext[0:POOL_HALO, :] = uext[tm:tm + POOL_HALO, :]
    cext[0:CONV_HALO, :] = cext[tm:tm + CONV_HALO, :]

    v = dt_ref[...] + dtb_ref[...]
    dt = jnp.maximum(v, 0.0) + jnp.log1p(jnp.exp(-jnp.abs(v)))
    row = lax.broadcasted_iota(i32, (tm, N_HEADS), 0)
    if lead:
        dt = jnp.where(row >= lead, dt, 0.0)
    a = -jnp.exp(alog_ref[...])
    acs = dt * a
    rin = row % CHUNK
    sh = 1
    while sh < CHUNK:
        acs = acs + jnp.where(rin >= sh, pltpu.roll(acs, sh, axis=0), 0.0)
        sh *= 2
    alast = jnp.concatenate(
        [jnp.broadcast_to(acs[c * CHUNK + CHUNK - 1:c * CHUNK + CHUNK, :], (CHUNK, N_HEADS))
         for c in range(nch)], axis=0)
    dec_st = jnp.exp(alast - acs)
    dec_in = jnp.exp(acs)

    expand = expand_ref[...]
    dt_x = _split_dot(dt, expand)
    dtds_x = _split_dot(dt * dec_st, expand)
    decin_x = _split_dot(dec_in, expand)

    eye_h = (lax.broadcasted_iota(i32, (N_HEADS, N_HEADS), 0)
             == lax.broadcasted_iota(i32, (N_HEADS, N_HEADS), 1)).astype(bf16)
    p1 = acs.astype(bf16)
    r1 = acs - p1.astype(f32)
    p2 = r1.astype(bf16)
    p3 = (r1 - p2.astype(f32)).astype(bf16)
    acs_t = _nt_dot(eye_h, p1) + _nt_dot(eye_h, p2) + _nt_dot(eye_h, p3)

    xdt = (xs * dt_x).astype(bf16)
    xds = (xs * dtds_x).astype(bf16)

    eye_n = (lax.broadcasted_iota(i32, (N_STATE, N_STATE), 0)
             == lax.broadcasted_iota(i32, (N_STATE, N_STATE), 1)).astype(bf16)
    causal = (lax.broadcasted_iota(i32, (CHUNK, CHUNK), 0)
              >= lax.broadcasted_iota(i32, (CHUNK, CHUNK), 1))
    col_head = lax.broadcasted_iota(i32, (CHUNK, GROUP_COLS), 1) // HEAD_DIM

    y_chunks = []
    for c in range(nch):
        r0 = c * CHUNK
        y_groups = []
        for g in range(N_GROUPS):
            gc = slice(g * GROUP_COLS, (g + 1) * GROUP_COLS)
            cg = cm[r0:r0 + CHUNK, g * N_STATE:(g + 1) * N_STATE].astype(bf16)
            bg = bm[r0:r0 + CHUNK, g * N_STATE:(g + 1) * N_STATE].astype(bf16)
            gmat = _nt_dot(cg, bg)
            st = state[g]
            y_off = (jnp.dot(cg, st.astype(bf16), preferred_element_type=f32)
                     * decin_x[r0:r0 + CHUNK, gc])
            scores = []
            for h in range(HEADS_PER_GROUP):
                hh = g * HEADS_PER_GROUP + h
                a_col = acs[r0:r0 + CHUNK, hh:hh + 1]
                a_row = acs_t[hh:hh + 1, r0:r0 + CHUNK]
                lmat = jnp.exp(jnp.where(causal, a_col - a_row, -jnp.inf))
                scores.append((gmat * lmat).astype(bf16))
            sc = jnp.concatenate(scores, axis=0)
            full = jnp.dot(sc, xdt[r0:r0 + CHUNK, gc], preferred_element_type=f32)
            y_diag = jnp.zeros((CHUNK, GROUP_COLS), f32)
            for h in range(HEADS_PER_GROUP):
                y_diag = jnp.where(col_head == h, full[h * CHUNK:(h + 1) * CHUNK, :], y_diag)
            y_groups.append(y_diag + y_off)
            bg_t = _nt_dot(eye_n, bg).astype(bf16)
            upd = jnp.dot(bg_t, xds[r0:r0 + CHUNK, gc], preferred_element_type=f32)
            state[g] = st * decin_x[r0 + CHUNK - 1:r0 + CHUNK, gc] + upd
        y_chunks.append(jnp.concatenate(y_groups, axis=1))
    y = jnp.concatenate(y_chunks, axis=0) if nch > 1 else y_chunks[0]

    y = y + dskip_ref[...] * xs
    zz = z_ref[...]
    y = y * (zz * jax.nn.sigmoid(zz))
    for g in range(N_GROUPS):
        gc = slice(g * GROUP_COLS, (g + 1) * GROUP_COLS)
        yg = y[:, gc]
        yg = yg * lax.rsqrt(jnp.mean(yg * yg, axis=-1, keepdims=True) + NORM_EPS)
        y_ref[:, D_POOL + g * GROUP_COLS:D_POOL + (g + 1) * GROUP_COLS] = (
            yg * ng_ref[:, gc]).astype(bf16)

    state_out_ref[...] = state[...]
    utail_ref[...] = uext[0:POOL_HALO, :]
    ctail_ref[...] = cext[0:CONV_HALO, :]


def _mixer(proj, dt_raw, params, state0, uhalo0, chalo0, tm, lead):
    m = proj.shape[0]
    (pool_w, pool_scale, conv_w, conv_b, dt_bias, a_log, dskip_x, norm_g, expand) = params
    full = lambda shape: pl.BlockSpec(shape, lambda i: (0,) * len(shape))
    return pl.pallas_call(
        functools.partial(_mixer_kernel, tm=tm, lead=lead),
        grid=(m // tm,),
        in_specs=[
            pl.BlockSpec((tm, D_POOL), lambda i: (i, 0)),
            pl.BlockSpec((tm, D_SSM), lambda i: (i, 1)),
            pl.BlockSpec((tm, D_SSM), lambda i: (i, 2)),
            pl.BlockSpec((tm, 2 * N_GROUPS * N_STATE), lambda i: (i, 6)),
            pl.BlockSpec((tm, N_HEADS), lambda i: (i, 0)),
            full((len(POOL_WINDOWS), POOL_GROUP, POOL_GROUP)),
            full((1, D_POOL)),
            full((4, D_CONV)),
            full((1, D_CONV)),
            full((1, N_HEADS)),
            full((1, N_HEADS)),
            full((1, D_SSM)),
            full((1, D_SSM)),
            full((N_HEADS, D_SSM)),
            full((N_GROUPS, N_STATE, GROUP_COLS)),
            full((POOL_HALO, D_POOL)),
            full((CONV_HALO, D_CONV)),
        ],
        out_specs=[
            pl.BlockSpec((tm, D_MODEL), lambda i: (i, 0)),
            full((N_GROUPS, N_STATE, GROUP_COLS)),
            full((POOL_HALO, D_POOL)),
            full((CONV_HALO, D_CONV)),
        ],
        out_shape=[
            jax.ShapeDtypeStruct((m, D_MODEL), bf16),
            jax.ShapeDtypeStruct((N_GROUPS, N_STATE, GROUP_COLS), f32),
            jax.ShapeDtypeStruct((POOL_HALO, D_POOL), f32),
            jax.ShapeDtypeStruct((CONV_HALO, D_CONV), f32),
        ],
        scratch_shapes=[
            pltpu.VMEM((N_GROUPS, N_STATE, GROUP_COLS), f32),
            pltpu.VMEM((tm + POOL_HALO, D_POOL), f32),
            pltpu.VMEM((tm + CONV_HALO, D_CONV), f32),
        ],
        compiler_params=_cparams(("arbitrary",)),
        name="mixer",
    )(proj, proj, proj, proj, dt_raw, pool_w, pool_scale, conv_w, conv_b, dt_bias, a_log,
      dskip_x, norm_g, expand, state0, uhalo0, chalo0)


def _outproj_kernel(y_ref, w_ref, x_ref, g2_ref, rw_ref, rb_ref,
                    h2_ref, hn_ref, pos_ref, gate_ref, cnt_ref,
                    hbuf, cnt, *, tm, tn):
    i = pl.program_id(0)
    j = pl.program_id(1)
    nj = D_MODEL // tn

    @pl.when((i == 0) & (j == 0))
    def _():
        cnt[...] = jnp.zeros_like(cnt)

    h = x_ref[...] + jnp.dot(y_ref[...], w_ref[...].astype(bf16), preferred_element_type=f32)
    h2_ref[...] = h
    hbuf[j] = h

    @pl.when(j == nj - 1)
    def _():
        ss = jnp.zeros((tm, 1), f32)
        for q in range(nj):
            hq = hbuf[q]
            ss = ss + jnp.sum(hq * hq, axis=-1, keepdims=True)
        inv = lax.rsqrt(ss / float(D_MODEL) + NORM_EPS)
        logits = jnp.zeros((tm, N_EXPERTS), f32) + rb_ref[...]
        for q in range(nj):
            hn = hbuf[q] * inv * g2_ref[:, q * tn:(q + 1) * tn]
            hn_ref[:, q * tn:(q + 1) * tn] = hn
            logits = logits + jnp.dot(hn.astype(bf16),
                                      rw_ref[q * tn:(q + 1) * tn, :].astype(bf16),
                                      preferred_element_type=f32)

        lane = lax.broadcasted_iota(i32, (tm, N_EXPERTS), 1).astype(f32)
        vals = logits
        picked = jnp.zeros((tm, N_EXPERTS), f32)
        tops, idxs = [], []
        for _ in range(TOP_K):
            mx = jnp.max(vals, axis=-1, keepdims=True)
            idx = jnp.min(jnp.where(vals == mx, lane, float(N_EXPERTS)), axis=-1, keepdims=True)
            hit = lane == idx
            tops.append(mx)
            idxs.append(idx)
            picked = picked + hit.astype(f32)
            vals = jnp.where(hit, -jnp.inf, vals)
        exps = [jnp.exp(t - tops[0]) for t in tops]
        denom = exps[0] + exps[1] + exps[2] + exps[3]

        tri = (lax.broadcasted_iota(i32, (tm, tm), 0)
               > lax.broadcasted_iota(i32, (tm, tm), 1)).astype(bf16)
        cum = jnp.dot(tri, picked.astype(bf16), preferred_element_type=f32) + cnt[...]
        lane4 = lax.broadcasted_iota(i32, (tm, TOP_K), 1)
        pos = jnp.zeros((tm, TOP_K), i32)
        gates = jnp.zeros((tm, TOP_K), f32)
        for k in range(TOP_K):
            rank = jnp.sum(jnp.where(lane == idxs[k], cum, 0.0), axis=-1, keepdims=True)
            p = idxs[k].astype(i32) * EXPERT_CAP + rank.astype(i32)
            pos = jnp.where(lane4 == k, p, pos)
            gates = jnp.where(lane4 == k, exps[k] / denom, gates)
        pos_ref[...] = pos
        gate_ref[...] = gates
        cnt[...] = cnt[...] + jnp.sum(picked, axis=0, keepdims=True)
        cnt_ref[...] = cnt[...]


def _outproj(ymix, w_out, x2d, g2, rw, rb, tm=1024, tn=512):
    m = x2d.shape[0]
    return pl.pallas_call(
        functools.partial(_outproj_kernel, tm=tm, tn=tn),
        grid=(m // tm, D_MODEL // tn),
        in_specs=[
            pl.BlockSpec((tm, D_MODEL), lambda i, j: (i, 0)),
            pl.BlockSpec((D_MODEL, tn), lambda i, j: (0, j)),
            pl.BlockSpec((tm, tn), lambda i, j: (i, j)),
            pl.BlockSpec((1, D_MODEL), lambda i, j: (0, 0)),
            pl.BlockSpec((D_MODEL, N_EXPERTS), lambda i, j: (0, 0)),
            pl.BlockSpec((1, N_EXPERTS), lambda i, j: (0, 0)),
        ],
        out_specs=[
            pl.BlockSpec((tm, tn), lambda i, j: (i, j)),
            pl.BlockSpec((tm, D_MODEL), lambda i, j: (i, 0)),
            pl.BlockSpec((tm, TOP_K), lambda i, j: (i, 0)),
            pl.BlockSpec((tm, TOP_K), lambda i, j: (i, 0)),
            pl.BlockSpec((1, N_EXPERTS), lambda i, j: (0, 0)),
        ],
        out_shape=[
            jax.ShapeDtypeStruct((m, D_MODEL), f32),
            jax.ShapeDtypeStruct((m, D_MODEL), f32),
            jax.ShapeDtypeStruct((m, TOP_K), i32),
            jax.ShapeDtypeStruct((m, TOP_K), f32),
            jax.ShapeDtypeStruct((1, N_EXPERTS), f32),
        ],
        scratch_shapes=[
            pltpu.VMEM((D_MODEL // tn, tm, tn), f32),
            pltpu.VMEM((1, N_EXPERTS), f32),
        ],
        compiler_params=_cparams(("arbitrary", "arbitrary")),
        name="outproj_router",
    )(ymix, w_out, x2d, g2, rw, rb)


def _scatter_kernel(pos_ref, cnt_ref, hn_ref, xs_ref, zbuf, sem, zsem, *, tb):
    i = pl.program_id(0)

    @pl.when(i == 0)
    def _():
        zbuf[...] = jnp.zeros_like(zbuf)
        for e in range(N_EXPERTS):
            start = pl.multiple_of(e * EXPERT_CAP + (cnt_ref[e] // 8) * 8, 8)
            pltpu.make_async_copy(zbuf, xs_ref.at[pl.ds(start, ZERO_ROWS)], zsem).start()
        for e in range(N_EXPERTS):
            start = pl.multiple_of(e * EXPERT_CAP + (cnt_ref[e] // 8) * 8, 8)
            pltpu.make_async_copy(zbuf, xs_ref.at[pl.ds(start, ZERO_ROWS)], zsem).wait()

    def body(t, carry):
        for k in range(TOP_K):
            p = pos_ref[(i * tb + t) * TOP_K + k]
            pltpu.make_async_copy(hn_ref.at[pl.ds(t, 1)], xs_ref.at[pl.ds(p, 1)], sem).start()
        return carry

    lax.fori_loop(0, tb, body, 0, unroll=8)
    done = xs_ref.at[pl.ds(0, tb * TOP_K)]
    pltpu.make_async_copy(done, done, sem).wait()


def _scatter(pos_flat, cnt_i, hn2, tb=512):
    m = hn2.shape[0]
    return pl.pallas_call(
        functools.partial(_scatter_kernel, tb=tb),
        grid_spec=pltpu.PrefetchScalarGridSpec(
            num_scalar_prefetch=2,
            grid=(m // tb,),
            in_specs=[pl.BlockSpec((tb, D_MODEL), lambda i, pos, cnt: (i, 0))],
            out_specs=pl.BlockSpec(memory_space=pl.ANY),
            scratch_shapes=[
                pltpu.VMEM((ZERO_ROWS, D_MODEL), f32),
                pltpu.SemaphoreType.DMA(()),
                pltpu.SemaphoreType.DMA(()),
            ],
        ),
        out_shape=jax.ShapeDtypeStruct((N_EXPERTS * EXPERT_CAP, D_MODEL), f32),
        compiler_params=_cparams(("arbitrary",)),
        name="scatter_rows",
    )(pos_flat, cnt_i, hn2)


def _moe_kernel(we_ref, wq_ref, wn_ref,
                xs_ref, wg_ref, wu_ref, wd_ref, bg_ref, bu_ref, bd_ref,
                o_ref, stage, xb, abuf, ws_a, ws_b, sem):
    w = pl.program_id(0)
    j = pl.program_id(1)
    n = wn_ref[w]
    base = we_ref[w] * EXPERT_CAP + wq_ref[w] * MOE_ROWS

    def x_copy(s):
        return pltpu.make_async_copy(
            xs_ref.at[pl.ds(pl.multiple_of(base + s * MOE_SUB, MOE_SUB), MOE_SUB)],
            stage.at[s % 2], sem.at[s % 2])

    @pl.when((j == 0) & (n > 0))
    def _():
        for s in range(2):
            @pl.when(s < n)
            def _():
                x_copy(s).start()
        for s in range(MOE_SUBS):
            @pl.when(s < n)
            def _():
                x_copy(s).wait()
                xb[pl.ds(s * MOE_SUB, MOE_SUB), :] = stage[s % 2].astype(bf16)
                if s + 2 < MOE_SUBS:
                    @pl.when(s + 2 < n)
                    def _():
                        x_copy(s + 2).start()

    @pl.when((j < MOE_NJ) & (n > 0))
    def _():
        ws_a[...] = wg_ref[0].astype(bf16)
        ws_b[...] = wu_ref[0].astype(bf16)
        bgv = bg_ref[0]
        buv = bu_ref[0]

        def body(s, carry):
            rows = pl.ds(pl.multiple_of(s * MOE_SUB, MOE_SUB), MOE_SUB)
            x = xb[rows, :]
            g = jnp.dot(x, ws_a[...], preferred_element_type=f32) + bgv
            u = jnp.dot(x, ws_b[...], preferred_element_type=f32) + buv
            g = jnp.minimum(g, SWIGLU_LIMIT)
            u = jnp.clip(u, -SWIGLU_LIMIT, SWIGLU_LIMIT)
            act = (u + 1.0) * (g * jax.nn.sigmoid(SWIGLU_ALPHA * g))
            abuf[j, rows, :] = act.astype(bf16)
            return carry

        lax.fori_loop(0, n, body, 0)

    @pl.when((j >= MOE_NJ) & (n > 0))
    def _():
        ws_a[...] = wd_ref[0].astype(bf16)
        bdv = bd_ref[0]

        def body(s, carry):
            rows = pl.ds(pl.multiple_of(s * MOE_SUB, MOE_SUB), MOE_SUB)
            acc = jnp.zeros((MOE_SUB, MOE_TN), f32) + bdv
            for q in range(MOE_NJ):
                acc = acc + jnp.dot(abuf[q, rows, :], ws_a[q * MOE_TN:(q + 1) * MOE_TN, :],
                                    preferred_element_type=f32)
            o_ref[rows, :] = acc
            return carry

        lax.fori_loop(0, n, body, 0)


def _moe(we, wq, wn, xs, w_gate, w_up, w_down, b_gate, b_up, b_down):
    blocks_per_expert = EXPERT_CAP // MOE_ROWS

    def w_a(w, j, we, wq, wn):
        return (we[w], 0, jnp.minimum(j, MOE_NJ - 1))

    def w_b(w, j, we, wq, wn):
        return (we[w], 0, jnp.maximum(j - MOE_NJ, 0))

    def o_map(w, j, we, wq, wn):
        jj = jnp.where(wn[w] > 0, jnp.maximum(j - MOE_NJ, 0), MOE_NJ - 1)
        return (we[w] * blocks_per_expert + wq[w], jj)

    return pl.pallas_call(
        _moe_kernel,
        grid_spec=pltpu.PrefetchScalarGridSpec(
            num_scalar_prefetch=3,
            grid=(MOE_WORK, 2 * MOE_NJ),
            in_specs=[
                pl.BlockSpec(memory_space=pl.ANY),
                pl.BlockSpec((1, D_MODEL, MOE_TN), w_a),
                pl.BlockSpec((1, D_MODEL, MOE_TN), w_a),
                pl.BlockSpec((1, D_EXPERT, MOE_TN), w_b),
                pl.BlockSpec((1, 1, MOE_TN), w_a),
                pl.BlockSpec((1, 1, MOE_TN), w_a),
                pl.BlockSpec((1, 1, MOE_TN), w_b),
            ],
            out_specs=pl.BlockSpec((MOE_ROWS, MOE_TN), o_map),
            scratch_shapes=[
                pltpu.VMEM((2, MOE_SUB, D_MODEL), f32),
                pltpu.VMEM((MOE_ROWS, D_MODEL), bf16),
                pltpu.VMEM((MOE_NJ, MOE_ROWS, MOE_TN), bf16),
                pltpu.VMEM((D_MODEL, MOE_TN), bf16),
                pltpu.VMEM((D_MODEL, MOE_TN), bf16),
                pltpu.SemaphoreType.DMA((2,)),
            ],
        ),
        out_shape=jax.ShapeDtypeStruct((N_EXPERTS * EXPERT_CAP, D_MODEL), f32),
        compiler_params=_cparams(("arbitrary", "arbitrary")),
        name="moe_experts",
    )(we, wq, wn, xs, w_gate, w_up, w_down,
      b_gate.reshape(N_EXPERTS, 1, D_EXPERT), b_up.reshape(N_EXPERTS, 1, D_EXPERT),
      b_down.reshape(N_EXPERTS, 1, D_MODEL))


def _work_list(cnt_i):
    nq = (cnt_i + MOE_ROWS - 1) // MOE_ROWS
    ends = jnp.cumsum(nq)
    total = ends[-1]
    w = jnp.arange(MOE_WORK, dtype=i32)
    wc = jnp.minimum(w, total - 1)
    e = jnp.minimum(jnp.sum((ends[None, :] <= wc[:, None]).astype(i32), axis=1), N_EXPERTS - 1)
    q = wc - (ends[e] - nq[e])
    rows = jnp.clip(cnt_i[e] - q * MOE_ROWS, 0, MOE_ROWS)
    n = jnp.where(w < total, (rows + MOE_SUB - 1) // MOE_SUB, 0)
    return e.astype(i32), q.astype(i32), n.astype(i32)


def _combine_kernel(pos_ref, o_ref, gate_ref, h2_ref, fg_ref, out_ref, obuf, sem, *, tm):
    i = pl.program_id(0)

    def body(t, carry):
        tok = i * tm + t
        for k in range(TOP_K):
            p = pos_ref[tok * TOP_K + k]
            pltpu.make_async_copy(o_ref.at[pl.ds(p, 1)], obuf.at[pl.ds(k * tm + t, 1)], sem).start()
        return carry

    lax.fori_loop(0, tm, body, 0, unroll=8)
    pltpu.make_async_copy(obuf, obuf, sem).wait()

    gates = gate_ref[...]
    y = h2_ref[...]
    for k in range(TOP_K):
        y = y + gates[:, k:k + 1] * obuf[k * tm:(k + 1) * tm, :]
    out = y * lax.rsqrt(jnp.mean(y * y, axis=-1, keepdims=True) + NORM_EPS)
    out_ref[...] = out * fg_ref[...]


def _combine(pos_flat, o, gates, h2, final_g, tm=256):
    m = h2.shape[0]
    return pl.pallas_call(
        functools.partial(_combine_kernel, tm=tm),
        grid_spec=pltpu.PrefetchScalarGridSpec(
            num_scalar_prefetch=1,
            grid=(m // tm,),
            in_specs=[
                pl.BlockSpec(memory_space=pl.ANY),
                pl.BlockSpec((tm, TOP_K), lambda i, pos: (i, 0)),
                pl.BlockSpec((tm, D_MODEL), lambda i, pos: (i, 0)),
                pl.BlockSpec((1, D_MODEL), lambda i, pos: (0, 0)),
            ],
            out_specs=pl.BlockSpec((tm, D_MODEL), lambda i, pos: (i, 0)),
            scratch_shapes=[
                pltpu.VMEM((TOP_K * tm, D_MODEL), f32),
                pltpu.SemaphoreType.DMA(()),
            ],
        ),
        out_shape=jax.ShapeDtypeStruct((m, D_MODEL), f32),
        compiler_params=_cparams(("arbitrary",)),
        name="combine",
    )(pos_flat, o, gates, h2, final_g)


def kernel(x, meta_tokens, norm1_g, w_in, pool_w, pool_scale, conv_w, conv_b, dt_bias, a_log,
           d_skip, ssm_norm_g, w_out, norm2_g, router_w, router_b, w_gate, b_gate, w_up, b_up,
           w_down, b_down, final_g):
    assert x.shape == (1, SEQ, D_MODEL)
    x2d = x.reshape(SEQ, D_MODEL)
    lead = CHUNK - N_META
    meta_chunk = jnp.concatenate([jnp.zeros((lead, D_MODEL), f32), meta_tokens.astype(f32)], 0)

    g1 = norm1_g[0].reshape(1, D_MODEL)
    w_in0 = w_in[0]
    w_dt = w_in0[:, D_MAIN:]
    head_of_col = jnp.arange(D_SSM, dtype=i32) // HEAD_DIM
    expand = (head_of_col[None, :] == jnp.arange(N_HEADS, dtype=i32)[:, None]).astype(bf16)
    mix_params = (
        pool_w[0], pool_scale[0].reshape(1, D_POOL), conv_w[0], conv_b[0].reshape(1, D_CONV),
        dt_bias[0].reshape(1, N_HEADS), a_log[0].reshape(1, N_HEADS),
        jnp.repeat(d_skip[0], HEAD_DIM).reshape(1, D_SSM), ssm_norm_g[0].reshape(1, D_SSM),
        expand)

    proj_m, dt_m = _inproj(meta_chunk, g1, w_in0, w_dt, tm=CHUNK)
    zeros_state = jnp.zeros((N_GROUPS, N_STATE, GROUP_COLS), f32)
    _, state_m, utail_m, ctail_m = _mixer(
        proj_m, dt_m, mix_params, zeros_state, jnp.zeros((POOL_HALO, D_POOL), f32),
        jnp.zeros((CONV_HALO, D_CONV), f32), tm=CHUNK, lead=lead)

    proj, dt_raw = _inproj(x2d, g1, w_in0, w_dt, tm=1024)
    ymix, _, _, _ = _mixer(proj, dt_raw, mix_params, state_m, utail_m, ctail_m, tm=256, lead=0)

    h2, hn2, pos, gates, cnt = _outproj(
        ymix, w_out[0], x2d, norm2_g[0].reshape(1, D_MODEL), router_w[0],
        router_b[0].reshape(1, N_EXPERTS))

    pos_flat = pos.reshape(SEQ * TOP_K)
    cnt_i = cnt.reshape(N_EXPERTS).astype(i32)
    xs = _scatter(pos_flat, cnt_i, hn2)
    we, wq, wn = _work_list(cnt_i)
    o = _moe(we, wq, wn, xs, w_gate[0], w_up[0], w_down[0], b_gate[0], b_up[0], b_down[0])
    out = _combine(pos_flat, o, gates, h2, final_g.reshape(1, D_MODEL))
    return out.reshape(1, SEQ, D_MODEL)
```

```python
import functools

import jax
import jax.numpy as jnp
from jax import lax
from jax.experimental import pallas as pl
from jax.experimental.pallas import tpu as pltpu

f32 = jnp.float32
bf16 = jnp.bfloat16
i32 = jnp.int32

D_MODEL = 2048
SEQ = 8192
CHUNK = 64
N_META = 16
D_POOL = 1024
POOL_WINDOWS = (2, 4, 8, 16)
POOL_GROUP = 256
D_SSM = 1024
HEAD_DIM = 64
N_HEADS = 16
N_GROUPS = 2
HEADS_PER_GROUP = N_HEADS // N_GROUPS
N_STATE = 128
GROUP_COLS = HEADS_PER_GROUP * HEAD_DIM
D_CONV = D_SSM + 2 * N_GROUPS * N_STATE
D_MAIN = D_POOL + D_SSM + D_CONV
N_EXPERTS = 32
TOP_K = 4
D_EXPERT = 2048
SWIGLU_LIMIT = 7.0
SWIGLU_ALPHA = 1.702
NORM_EPS = 1e-5

POOL_HALO = 16
CONV_HALO = 8

MOE_SUB = 256
MOE_ROWS = 1280
MOE_SUBS = MOE_ROWS // MOE_SUB
ZERO_ROWS = MOE_SUB + 8
EXPERT_CAP = -(-(SEQ + ZERO_ROWS) // MOE_ROWS) * MOE_ROWS
MOE_TN = 512
MOE_NJ = D_EXPERT // MOE_TN
MOE_WORK = (SEQ * TOP_K) // MOE_ROWS + N_EXPERTS

VMEM_LIMIT = 56 * 1024 * 1024


def _cparams(sem):
    return pltpu.CompilerParams(dimension_semantics=sem, vmem_limit_bytes=VMEM_LIMIT)


def _inproj_kernel(x_ref, g_ref, w_ref, wdt_ref, proj_ref, dt_ref, hn_ref):
    j = pl.program_id(1)

    @pl.when(j == 0)
    def _():
        x = x_ref[...]
        y = x * lax.rsqrt(jnp.mean(x * x, axis=-1, keepdims=True) + NORM_EPS)
        hn = (y * g_ref[...]).astype(bf16)
        hn_ref[...] = hn
        dt_ref[...] = jnp.dot(hn, wdt_ref[...].astype(bf16), preferred_element_type=f32)

    proj_ref[...] = jnp.dot(hn_ref[...], w_ref[...].astype(bf16),
                            preferred_element_type=f32)


def _inproj(x2d, g, w_in, w_dt, tm, tn=512):
    m = x2d.shape[0]
    return pl.pallas_call(
        _inproj_kernel,
        grid=(m // tm, D_MAIN // tn),
        in_specs=[
            pl.BlockSpec((tm, D_MODEL), lambda i, j: (i, 0)),
            pl.BlockSpec((1, D_MODEL), lambda i, j: (0, 0)),
            pl.BlockSpec((D_MODEL, tn), lambda i, j: (0, j)),
            pl.BlockSpec((D_MODEL, N_HEADS), lambda i, j: (0, 0)),
        ],
        out_specs=[
            pl.BlockSpec((tm, tn), lambda i, j: (i, j)),
            pl.BlockSpec((tm, N_HEADS), lambda i, j: (i, 0)),
        ],
        out_shape=[
            jax.ShapeDtypeStruct((m, D_MAIN), f32),
            jax.ShapeDtypeStruct((m, N_HEADS), f32),
        ],
        scratch_shapes=[pltpu.VMEM((tm, D_MODEL), bf16)],
        compiler_params=_cparams(("arbitrary", "arbitrary")),
        name="inproj",
    )(x2d, g, w_in, w_dt)


def _split_dot(q, e):
    hi = q.astype(bf16)
    lo = (q - hi.astype(f32)).astype(bf16)
    return (jnp.dot(hi, e, preferred_element_type=f32)
            + jnp.dot(lo, e, preferred_element_type=f32))


def _nt_dot(a, b):
    return lax.dot_general(a, b, (((1,), (1,)), ((), ())), preferred_element_type=f32)


def _mixer_kernel(u_ref, z_ref, xs_ref, bc_ref, dt_ref,
                  poolw_ref, pscale_ref, convw_ref, convb_ref, dtb_ref, alog_ref,
                  dskip_ref, ng_ref, expand_ref, state0_ref, uhalo0_ref, chalo0_ref,
                  y_ref, state_out_ref, utail_ref, ctail_ref,
                  state, uext, cext, *, tm, lead):
    i = pl.program_id(0)
    nch = tm // CHUNK

    @pl.when(i == 0)
    def _():
        state[...] = state0_ref[...]
        uext[0:POOL_HALO, :] = uhalo0_ref[...]
        cext[0:CONV_HALO, :] = chalo0_ref[...]

    uext[POOL_HALO:, :] = u_ref[...]
    cext[CONV_HALO:, 0:D_SSM] = xs_ref[...]
    cext[CONV_HALO:, D_SSM:] = bc_ref[...]

    for g, w in enumerate(POOL_WINDOWS):
        cols = slice(g * POOL_GROUP, (g + 1) * POOL_GROUP)
        ue = uext[:, cols]
        s = ue
        sh = 1
        while sh < w:
            s = s + pltpu.roll(s, sh, axis=0)
            sh *= 2
        d = s[POOL_HALO:, :] / float(w) - ue[POOL_HALO:, :]
        yp = jnp.dot(d.astype(bf16), poolw_ref[g].astype(bf16), preferred_element_type=f32)
        y_ref[:, cols] = (yp * pscale_ref[:, cols]).astype(bf16)

    ce = cext[...]
    cw = convw_ref[...]
    acc = (ce * cw[3:4, :] + pltpu.roll(ce, 1, axis=0) * cw[2:3, :]
           + pltpu.roll(ce, 2, axis=0) * cw[1:2, :] + pltpu.roll(ce, 3, axis=0) * cw[0:1, :])
    acc = acc[CONV_HALO:, :] + convb_ref[...]
    xbc = acc * jax.nn.sigmoid(acc)
    xs = xbc[:, 0:D_SSM]
    bm = xbc[:, D_SSM:D_SSM + N_GROUPS * N_STATE]
    cm = xbc[:, D_SSM + N_GROUPS * N_STATE:]

    uext[0:POOL_HALO, :] = uext[tm:tm + POOL_HALO, :]
    cext[0:CONV_HALO, :] = cext[tm:tm + CONV_HALO, :]

    v = dt_ref[...] + dtb_ref[...]
    dt = jnp.maximum(v, 0.0) + jnp.log1p(jnp.exp(-jnp.abs(v)))
    row = lax.broadcasted_iota(i32, (tm, N_HEADS), 0)
    if lead:
        dt = jnp.where(row >= lead, dt, 0.0)
    a = -jnp.exp(alog_ref[...])
    acs = dt * a
    rin = row % CHUNK
    sh = 1
    while sh < CHUNK:
        acs = acs + jnp.where(rin >= sh, pltpu.roll(acs, sh, axis=0), 0.0)
        sh *= 2
    alast = jnp.concatenate(
        [jnp.broadcast_to(acs[c * CHUNK + CHUNK - 1:c * CHUNK + CHUNK, :], (CHUNK, N_HEADS))
         for c in range(nch)], axis=0)
    dec_st = jnp.exp(alast - acs)
    dec_in = jnp.exp(acs)

    expand = expand_ref[...]
    dt_x = _split_dot(dt, expand)
    dtds_x = _split_dot(dt * dec_st, expand)
    decin_x = _split_dot(dec_in, expand)

    eye_h = (lax.broadcasted_iota(i32, (N_HEADS, N_HEADS), 0)
             == lax.broadcasted_iota(i32, (N_HEADS, N_HEADS), 1)).astype(bf16)
    p1 = acs.astype(bf16)
    r1 = acs - p1.astype(f32)
    p2 = r1.astype(bf16)
    p3 = (r1 - p2.astype(f32)).astype(bf16)
    acs_t = _nt_dot(eye_h, p1) + _nt_dot(eye_h, p2) + _nt_dot(eye_h, p3)

    xdt = (xs * dt_x).astype(bf16)
    xds = (xs * dtds_x).astype(bf16)

    eye_n = (lax.broadcasted_iota(i32, (N_STATE, N_STATE), 0)
             == lax.broadcasted_iota(i32, (N_STATE, N_STATE), 1)).astype(bf16)
    causal = (lax.broadcasted_iota(i32, (CHUNK, CHUNK), 0)
              >= lax.broadcasted_iota(i32, (CHUNK, CHUNK), 1))
    col_head = lax.broadcasted_iota(i32, (CHUNK, GROUP_COLS), 1) // HEAD_DIM

    y_chunks = []
    for c in range(nch):
        r0 = c * CHUNK
        y_groups = []
        for g in range(N_GROUPS):
            gc = slice(g * GROUP_COLS, (g + 1) * GROUP_COLS)
            cg = cm[r0:r0 + CHUNK, g * N_STATE:(g + 1) * N_STATE].astype(bf16)
            bg = bm[r0:r0 + CHUNK, g * N_STATE:(g + 1) * N_STATE].astype(bf16)
            gmat = _nt_dot(cg, bg)
            st = state[g]
            y_off = (jnp.dot(cg, st.astype(bf16), preferred_element_type=f32)
                     * decin_x[r0:r0 + CHUNK, gc])
            scores = []
            for h in range(HEADS_PER_GROUP):
                hh = g * HEADS_PER_GROUP + h
                a_col = acs[r0:r0 + CHUNK, hh:hh + 1]
                a_row = acs_t[hh:hh + 1, r0:r0 + CHUNK]
                lmat = jnp.exp(jnp.where(causal, a_col - a_row, -jnp.inf))
                scores.append((gmat * lmat).astype(bf16))
            sc = jnp.concatenate(scores, axis=0)
            full = jnp.dot(sc, xdt[r0:r0 + CHUNK, gc], preferred_element_type=f32)
            y_diag = jnp.zeros((CHUNK, GROUP_COLS), f32)
            for h in range(HEADS_PER_GROUP):
                y_diag = jnp.where(col_head == h, full[h * CHUNK:(h + 1) * CHUNK, :], y_diag)
            y_groups.append(y_diag + y_off)
            bg_t = _nt_dot(eye_n, bg).astype(bf16)
            upd = jnp.dot(bg_t, xds[r0:r0 + CHUNK, gc], preferred_element_type=f32)
            state[g] = st * decin_x[r0 + CHUNK - 1:r0 + CHUNK, gc] + upd
        y_chunks.append(jnp.concatenate(y_groups, axis=1))
    y = jnp.concatenate(y_chunks, axis=0) if nch > 1 else y_chunks[0]

    y = y + dskip_ref[...] * xs
    zz = z_ref[...]
    y = y * (zz * jax.nn.sigmoid(zz))
    for g in range(N_GROUPS):
        gc = slice(g * GROUP_COLS, (g + 1) * GROUP_COLS)
        yg = y[:, gc]
        yg = yg * lax.rsqrt(jnp.mean(yg * yg, axis=-1, keepdims=True) + NORM_EPS)
        y_ref[:, D_POOL + g * GROUP_COLS:D_POOL + (g + 1) * GROUP_COLS] = (
            yg * ng_ref[:, gc]).astype(bf16)

    state_out_ref[...] = state[...]
    utail_ref[...] = uext[0:POOL_HALO, :]
    ctail_ref[...] = cext[0:CONV_HALO, :]


def _mixer(proj, dt_raw, params, state0, uhalo0, chalo0, tm, lead):
    m = proj.shape[0]
    (pool_w, pool_scale, conv_w, conv_b, dt_bias, a_log, dskip_x, norm_g, expand) = params
    full = lambda shape: pl.BlockSpec(shape, lambda i: (0,) * len(shape))
    return pl.pallas_call(
        functools.partial(_mixer_kernel, tm=tm, lead=lead),
        grid=(m // tm,),
        in_specs=[
            pl.BlockSpec((tm, D_POOL), lambda i: (i, 0)),
            pl.BlockSpec((tm, D_SSM), lambda i: (i, 1)),
            pl.BlockSpec((tm, D_SSM), lambda i: (i, 2)),
            pl.BlockSpec((tm, 2 * N_GROUPS * N_STATE), lambda i: (i, 6)),
            pl.BlockSpec((tm, N_HEADS), lambda i: (i, 0)),
            full((len(POOL_WINDOWS), POOL_GROUP, POOL_GROUP)),
            full((1, D_POOL)),
            full((4, D_CONV)),
            full((1, D_CONV)),
            full((1, N_HEADS)),
            full((1, N_HEADS)),
            full((1, D_SSM)),
            full((1, D_SSM)),
            full((N_HEADS, D_SSM)),
            full((N_GROUPS, N_STATE, GROUP_COLS)),
            full((POOL_HALO, D_POOL)),
            full((CONV_HALO, D_CONV)),
        ],
        out_specs=[
            pl.BlockSpec((tm, D_MODEL), lambda i: (i, 0)),
            full((N_GROUPS, N_STATE, GROUP_COLS)),
            full((POOL_HALO, D_POOL)),
            full((CONV_HALO, D_CONV)),
        ],
        out_shape=[
            jax.ShapeDtypeStruct((m, D_MODEL), bf16),
            jax.ShapeDtypeStruct((N_GROUPS, N_STATE, GROUP_COLS), f32),
            jax.ShapeDtypeStruct((POOL_HALO, D_POOL), f32),
            jax.ShapeDtypeStruct((CONV_HALO, D_CONV), f32),
        ],
        scratch_shapes=[
            pltpu.VMEM((N_GROUPS, N_STATE, GROUP_COLS), f32),
            pltpu.VMEM((tm + POOL_HALO, D_POOL), f32),
            pltpu.VMEM((tm + CONV_HALO, D_CONV), f32),
        ],
        compiler_params=_cparams(("arbitrary",)),
        name="mixer",
    )(proj, proj, proj, proj, dt_raw, pool_w, pool_scale, conv_w, conv_b, dt_bias, a_log,
      dskip_x, norm_g, expand, state0, uhalo0, chalo0)


def _outproj_kernel(y_ref, w_ref, x_ref, g2_ref, rw_ref, rb_ref,
                    h2_ref, hn_ref, pos_ref, gate_ref, cnt_ref,
                    hbuf, cnt, *, tm, tn):
    i = pl.program_id(0)
    j = pl.program_id(1)
    nj = D_MODEL // tn

    @pl.when((i == 0) & (j == 0))
    def _():
        cnt[...] = jnp.zeros_like(cnt)

    h = x_ref[...] + jnp.dot(y_ref[...], w_ref[...].astype(bf16), preferred_element_type=f32)
    h2_ref[...] = h
    hbuf[j] = h

    @pl.when(j == nj - 1)
    def _():
        ss = jnp.zeros((tm, 1), f32)
        for q in range(nj):
            hq = hbuf[q]
            ss = ss + jnp.sum(hq * hq, axis=-1, keepdims=True)
        inv = lax.rsqrt(ss / float(D_MODEL) + NORM_EPS)
        logits = jnp.zeros((tm, N_EXPERTS), f32) + rb_ref[...]
        for q in range(nj):
            hn = hbuf[q] * inv * g2_ref[:, q * tn:(q + 1) * tn]
            hn_ref[:, q * tn:(q + 1) * tn] = hn
            logits = logits + jnp.dot(hn.astype(bf16),
                                      rw_ref[q * tn:(q + 1) * tn, :].astype(bf16),
                                      preferred_element_type=f32)

        lane = lax.broadcasted_iota(i32, (tm, N_EXPERTS), 1).astype(f32)
        vals = logits
        picked = jnp.zeros((tm, N_EXPERTS), f32)
        tops, idxs = [], []
        for _ in range(TOP_K):
            mx = jnp.max(vals, axis=-1, keepdims=True)
            idx = jnp.min(jnp.where(vals == mx, lane, float(N_EXPERTS)), axis=-1, keepdims=True)
            hit = lane == idx
            tops.append(mx)
            idxs.append(idx)
            picked = picked + hit.astype(f32)
            vals = jnp.where(hit, -jnp.inf, vals)
        exps = [jnp.exp(t - tops[0]) for t in tops]
        denom = exps[0] + exps[1] + exps[2] + exps[3]

        tri = (lax.broadcasted_iota(i32, (tm, tm), 0)
               > lax.broadcasted_iota(i32, (tm, tm), 1)).astype(bf16)
        cum = jnp.dot(tri, picked.astype(bf16), preferred_element_type=f32) + cnt[...]
        lane4 = lax.broadcasted_iota(i32, (tm, TOP_K), 1)
        pos = jnp.zeros((tm, TOP_K), i32)
        gates = jnp.zeros((tm, TOP_K), f32)
        for k in range(TOP_K):
            rank = jnp.sum(jnp.where(lane == idxs[k], cum, 0.0), axis=-1, keepdims=True)
            p = idxs[k].astype(i32) * EXPERT_CAP + rank.astype(i32)
            pos = jnp.where(lane4 == k, p, pos)
            gates = jnp.where(lane4 == k, exps[k] / denom, gates)
        pos_ref[...] = pos
        gate_ref[...] = gates
        cnt[...] = cnt[...] + jnp.sum(picked, axis=0, keepdims=True)
        cnt_ref[...] = cnt[...]


def _outproj(ymix, w_out, x2d, g2, rw, rb, tm=1024, tn=512):
    m = x2d.shape[0]
    return pl.pallas_call(
        functools.partial(_outproj_kernel, tm=tm, tn=tn),
        grid=(m // tm, D_MODEL // tn),
        in_specs=[
            pl.BlockSpec((tm, D_MODEL), lambda i, j: (i, 0)),
            pl.BlockSpec((D_MODEL, tn), lambda i, j: (0, j)),
            pl.BlockSpec((tm, tn), lambda i, j: (i, j)),
            pl.BlockSpec((1, D_MODEL), lambda i, j: (0, 0)),
            pl.BlockSpec((D_MODEL, N_EXPERTS), lambda i, j: (0, 0)),
            pl.BlockSpec((1, N_EXPERTS), lambda i, j: (0, 0)),
        ],
        out_specs=[
            pl.BlockSpec((tm, tn), lambda i, j: (i, j)),
            pl.BlockSpec((tm, D_MODEL), lambda i, j: (i, 0)),
            pl.BlockSpec((tm, TOP_K), lambda i, j: (i, 0)),
            pl.BlockSpec((tm, TOP_K), lambda i, j: (i, 0)),
            pl.BlockSpec((1, N_EXPERTS), lambda i, j: (0, 0)),
        ],
        out_shape=[
            jax.ShapeDtypeStruct((m, D_MODEL), f32),
            jax.ShapeDtypeStruct((m, D_MODEL), f32),
            jax.ShapeDtypeStruct((m, TOP_K), i32),
            jax.ShapeDtypeStruct((m, TOP_K), f32),
            jax.ShapeDtypeStruct((1, N_EXPERTS), f32),
        ],
        scratch_shapes=[
            pltpu.VMEM((D_MODEL // tn, tm, tn), f32),
            pltpu.VMEM((1, N_EXPERTS), f32),
        ],
        compiler_params=_cparams(("arbitrary", "arbitrary")),
        name="outproj_router",
    )(ymix, w_out, x2d, g2, rw, rb)


def _scatter_kernel(pos_ref, cnt_ref, hn_ref, xs_ref, zbuf, sem, zsem, *, tb):
    i = pl.program_id(0)

    @pl.when(i == 0)
    def _():
        zbuf[...] = jnp.zeros_like(zbuf)
        for e in range(N_EXPERTS):
            start = pl.multiple_of(e * EXPERT_CAP + (cnt_ref[e] // 8) * 8, 8)
            pltpu.make_async_copy(zbuf, xs_ref.at[pl.ds(start, ZERO_ROWS)], zsem).start()
        for e in range(N_EXPERTS):
            start = pl.multiple_of(e * EXPERT_CAP + (cnt_ref[e] // 8) * 8, 8)
            pltpu.make_async_copy(zbuf, xs_ref.at[pl.ds(start, ZERO_ROWS)], zsem).wait()

    def body(t, carry):
        for k in range(TOP_K):
            p = pos_ref[(i * tb + t) * TOP_K + k]
            pltpu.make_async_copy(hn_ref.at[pl.ds(t, 1)], xs_ref.at[pl.ds(p, 1)], sem).start()
        return carry

    lax.fori_loop(0, tb, body, 0, unroll=8)
    done = xs_ref.at[pl.ds(0, tb * TOP_K)]
    pltpu.make_async_copy(done, done, sem).wait()


def _scatter(pos_flat, cnt_i, hn2, tb=512):
    m = hn2.shape[0]
    return pl.pallas_call(
        functools.partial(_scatter_kernel, tb=tb),
        grid_spec=pltpu.PrefetchScalarGridSpec(
            num_scalar_prefetch=2,
            grid=(m // tb,),
            in_specs=[pl.BlockSpec((tb, D_MODEL), lambda i, pos, cnt: (i, 0))],
            out_specs=pl.BlockSpec(memory_space=pl.ANY),
            scratch_shapes=[
                pltpu.VMEM((ZERO_ROWS, D_MODEL), f32),
                pltpu.SemaphoreType.DMA(()),
                pltpu.SemaphoreType.DMA(()),
            ],
        ),
        out_shape=jax.ShapeDtypeStruct((N_EXPERTS * EXPERT_CAP, D_MODEL), f32),
        compiler_params=_cparams(("arbitrary",)),
        name="scatter_rows",
    )(pos_flat, cnt_i, hn2)


def _moe_kernel(we_ref, wq_ref, wn_ref,
                xs_ref, wg_ref, wu_ref, wd_ref, bg_ref, bu_ref, bd_ref,
                o_ref, stage, xb, abuf, ws_a, ws_b, sem):
    w = pl.program_id(0)
    j = pl.program_id(1)
    n = wn_ref[w]
    wnext = jnp.minimum(w + 1, MOE_WORK - 1)
    n_next = jnp.where(w + 1 < MOE_WORK, wn_ref[wnext], 0)

    def x_copy(item, s):
        base = we_ref[item] * EXPERT_CAP + wq_ref[item] * MOE_ROWS
        return pltpu.make_async_copy(
            xs_ref.at[pl.ds(pl.multiple_of(base + s * MOE_SUB, MOE_SUB), MOE_SUB)],
            stage.at[s % 2], sem.at[s % 2])

    def x_start(item, count, s):
        @pl.when(s < count)
        def _():
            x_copy(item, s).start()

    def x_finish(item, count, s):
        @pl.when(s < count)
        def _():
            x_copy(item, s).wait()
            xb[pl.ds(s * MOE_SUB, MOE_SUB), :] = stage[s % 2].astype(bf16)

    @pl.when((w == 0) & (j == 0))
    def _():
        x_start(w, n, 0)
        x_start(w, n, 1)
        for s in range(MOE_SUBS):
            x_finish(w, n, s)
            if s + 2 < MOE_SUBS:
                x_start(w, n, s + 2)

    def per_subtile(fn):
        def pair(p, carry):
            r0 = p * (2 * MOE_SUB)
            fn(r0)
            fn(r0 + MOE_SUB)
            return carry

        lax.fori_loop(0, n // 2, pair, 0)

        @pl.when(n % 2 == 1)
        def _():
            fn((n - 1) * MOE_SUB)

    @pl.when((j < MOE_NJ) & (n > 0))
    def _():
        ws_a[...] = wg_ref[0].astype(bf16)
        ws_b[...] = wu_ref[0].astype(bf16)
        bgv = bg_ref[0]
        buv = bu_ref[0]

        def gate_up(r0):
            rows = pl.ds(pl.multiple_of(r0, MOE_SUB), MOE_SUB)
            x = xb[rows, :]
            g = jnp.dot(x, ws_a[...], preferred_element_type=f32) + bgv
            u = jnp.dot(x, ws_b[...], preferred_element_type=f32) + buv
            g = jnp.minimum(g, SWIGLU_LIMIT)
            u = jnp.clip(u, -SWIGLU_LIMIT, SWIGLU_LIMIT)
            act = (u + 1.0) * (g * jax.nn.sigmoid(SWIGLU_ALPHA * g))
            abuf[j, rows, :] = act.astype(bf16)

        per_subtile(gate_up)

    @pl.when((j >= MOE_NJ) & (n > 0))
    def _():
        ws_a[...] = wd_ref[0].astype(bf16)
        bdv = bd_ref[0]

        def down(r0):
            rows = pl.ds(pl.multiple_of(r0, MOE_SUB), MOE_SUB)
            acc = jnp.zeros((MOE_SUB, MOE_TN), f32) + bdv
            for q in range(MOE_NJ):
                acc = acc + jnp.dot(abuf[q, rows, :], ws_a[q * MOE_TN:(q + 1) * MOE_TN, :],
                                    preferred_element_type=f32)
            o_ref[rows, :] = acc

        per_subtile(down)

    assert 2 * (MOE_NJ - 1) >= MOE_SUBS
    for k in range(MOE_NJ):
        @pl.when(j == MOE_NJ + k)
        def _():
            if k == 0:
                x_start(wnext, n_next, 0)
                x_start(wnext, n_next, 1)
            else:
                for s in (2 * k - 2, 2 * k - 1):
                    if s < MOE_SUBS:
                        x_finish(wnext, n_next, s)
                        if s + 2 < MOE_SUBS:
                            x_start(wnext, n_next, s + 2)


def _moe(we, wq, wn, xs, w_gate, w_up, w_down, b_gate, b_up, b_down):
    blocks_per_expert = EXPERT_CAP // MOE_ROWS

    def w_a(w, j, we, wq, wn):
        return (we[w], 0, jnp.minimum(j, MOE_NJ - 1))

    def w_b(w, j, we, wq, wn):
        return (we[w], 0, jnp.maximum(j - MOE_NJ, 0))

    def o_map(w, j, we, wq, wn):
        jj = jnp.where(wn[w] > 0, jnp.maximum(j - MOE_NJ, 0), MOE_NJ - 1)
        return (we[w] * blocks_per_expert + wq[w], jj)

    return pl.pallas_call(
        _moe_kernel,
        grid_spec=pltpu.PrefetchScalarGridSpec(
            num_scalar_prefetch=3,
            grid=(MOE_WORK, 2 * MOE_NJ),
            in_specs=[
                pl.BlockSpec(memory_space=pl.ANY),
                pl.BlockSpec((1, D_MODEL, MOE_TN), w_a),
                pl.BlockSpec((1, D_MODEL, MOE_TN), w_a),
                pl.BlockSpec((1, D_EXPERT, MOE_TN), w_b),
                pl.BlockSpec((1, 1, MOE_TN), w_a),
                pl.BlockSpec((1, 1, MOE_TN), w_a),
                pl.BlockSpec((1, 1, MOE_TN), w_b),
            ],
            out_specs=pl.BlockSpec((MOE_ROWS, MOE_TN), o_map),
            scratch_shapes=[
                pltpu.VMEM((2, MOE_SUB, D_MODEL), f32),
                pltpu.VMEM((MOE_ROWS, D_MODEL), bf16),
                pltpu.VMEM((MOE_NJ, MOE_ROWS, MOE_TN), bf16),
                pltpu.VMEM((D_MODEL, MOE_TN), bf16),
                pltpu.VMEM((D_MODEL, MOE_TN), bf16),
                pltpu.SemaphoreType.DMA((2,)),
            ],
        ),
        out_shape=jax.ShapeDtypeStruct((N_EXPERTS * EXPERT_CAP, D_MODEL), f32),
        compiler_params=_cparams(("arbitrary", "arbitrary")),
        name="moe_experts",
    )(we, wq, wn, xs, w_gate, w_up, w_down,
      b_gate.reshape(N_EXPERTS, 1, D_EXPERT), b_up.reshape(N_EXPERTS, 1, D_EXPERT),
      b_down.reshape(N_EXPERTS, 1, D_MODEL))


def _work_list(cnt_i):
    nq = (cnt_i + MOE_ROWS - 1) // MOE_ROWS
    ends = jnp.cumsum(nq)
    total = ends[-1]
    w = jnp.arange(MOE_WORK, dtype=i32)
    wc = jnp.minimum(w, total - 1)
    e = jnp.minimum(jnp.sum((ends[None, :] <= wc[:, None]).astype(i32), axis=1), N_EXPERTS - 1)
    q = wc - (ends[e] - nq[e])
    rows = jnp.clip(cnt_i[e] - q * MOE_ROWS, 0, MOE_ROWS)
    n = jnp.where(w < total, (rows + MOE_SUB - 1) // MOE_SUB, 0)
    return e.astype(i32), q.astype(i32), n.astype(i32)


def _combine_kernel(pos_ref, o_ref, gate_ref, h2_ref, fg_ref, out_ref, obuf, sem, *, tm):
    i = pl.program_id(0)

    def body(t, carry):
        tok = i * tm + t
        for k in range(TOP_K):
            p = pos_ref[tok * TOP_K + k]
            pltpu.make_async_copy(o_ref.at[pl.ds(p, 1)], obuf.at[pl.ds(k * tm + t, 1)], sem).start()
        return carry

    lax.fori_loop(0, tm, body, 0, unroll=8)
    pltpu.make_async_copy(obuf, obuf, sem).wait()

    gates = gate_ref[...]
    y = h2_ref[...]
    for k in range(TOP_K):
        y = y + gates[:, k:k + 1] * obuf[k * tm:(k + 1) * tm, :]
    out = y * lax.rsqrt(jnp.mean(y * y, axis=-1, keepdims=True) + NORM_EPS)
    out_ref[...] = out * fg_ref[...]


def _combine(pos_flat, o, gates, h2, final_g, tm=256):
    m = h2.shape[0]
    return pl.pallas_call(
        functools.partial(_combine_kernel, tm=tm),
        grid_spec=pltpu.PrefetchScalarGridSpec(
            num_scalar_prefetch=1,
            grid=(m // tm,),
            in_specs=[
                pl.BlockSpec(memory_space=pl.ANY),
                pl.BlockSpec((tm, TOP_K), lambda i, pos: (i, 0)),
                pl.BlockSpec((tm, D_MODEL), lambda i, pos: (i, 0)),
                pl.BlockSpec((1, D_MODEL), lambda i, pos: (0, 0)),
            ],
            out_specs=pl.BlockSpec((tm, D_MODEL), lambda i, pos: (i, 0)),
            scratch_shapes=[
                pltpu.VMEM((TOP_K * tm, D_MODEL), f32),
                pltpu.SemaphoreType.DMA(()),
            ],
        ),
        out_shape=jax.ShapeDtypeStruct((m, D_MODEL), f32),
        compiler_params=_cparams(("arbitrary",)),
        name="combine",
    )(pos_flat, o, gates, h2, final_g)


def kernel(x, meta_tokens, norm1_g, w_in, pool_w, pool_scale, conv_w, conv_b, dt_bias, a_log,
           d_skip, ssm_norm_g, w_out, norm2_g, router_w, router_b, w_gate, b_gate, w_up, b_up,
           w_down, b_down, final_g):
    assert x.shape == (1, SEQ, D_MODEL)
    x2d = x.reshape(SEQ, D_MODEL)
    lead = CHUNK - N_META
    meta_chunk = jnp.concatenate([jnp.zeros((lead, D_MODEL), f32), meta_tokens.astype(f32)], 0)

    g1 = norm1_g[0].reshape(1, D_MODEL)
    w_in0 = w_in[0]
    w_dt = w_in0[:, D_MAIN:]
    head_of_col = jnp.arange(D_SSM, dtype=i32) // HEAD_DIM
    expand = (head_of_col[None, :] == jnp.arange(N_HEADS, dtype=i32)[:, None]).astype(bf16)
    mix_params = (
        pool_w[0], pool_scale[0].reshape(1, D_POOL), conv_w[0], conv_b[0].reshape(1, D_CONV),
        dt_bias[0].reshape(1, N_HEADS), a_log[0].reshape(1, N_HEADS),
        jnp.repeat(d_skip[0], HEAD_DIM).reshape(1, D_SSM), ssm_norm_g[0].reshape(1, D_SSM),
        expand)

    proj_m, dt_m = _inproj(meta_chunk, g1, w_in0, w_dt, tm=CHUNK)
    zeros_state = jnp.zeros((N_GROUPS, N_STATE, GROUP_COLS), f32)
    _, state_m, utail_m, ctail_m = _mixer(
        proj_m, dt_m, mix_params, zeros_state, jnp.zeros((POOL_HALO, D_POOL), f32),
        jnp.zeros((CONV_HALO, D_CONV), f32), tm=CHUNK, lead=lead)

    proj, dt_raw = _inproj(x2d, g1, w_in0, w_dt, tm=1024)
    ymix, _, _, _ = _mixer(proj, dt_raw, mix_params, state_m, utail_m, ctail_m, tm=256, lead=0)

    h2, hn2, pos, gates, cnt = _outproj(
        ymix, w_out[0], x2d, norm2_g[0].reshape(1, D_MODEL), router_w[0],
        router_b[0].reshape(1, N_EXPERTS))

    pos_flat = pos.reshape(SEQ * TOP_K)
    cnt_i = cnt.reshape(N_EXPERTS).astype(i32)
    xs = _scatter(pos_flat, cnt_i, hn2)
    we, wq, wn = _work_list(cnt_i)
    o = _moe(we, wq, wn, xs, w_gate[0], w_up[0], w_down[0], b_gate[0], b_up[0], b_down[0])
    out = _combine(pos_flat, o, gates, h2, final_g.reshape(1, D_MODEL))
    return out.reshape(1, SEQ, D_MODEL)
```

```python
import functools

import jax
import jax.numpy as jnp
from jax import lax
from jax.experimental import pallas as pl
from jax.experimental.pallas import tpu as pltpu

f32 = jnp.float32
bf16 = jnp.bfloat16
i32 = jnp.int32

D_MODEL = 2048
SEQ = 8192
CHUNK = 64
N_META = 16
D_POOL = 1024
POOL_WINDOWS = (2, 4, 8, 16)
POOL_GROUP = 256
D_SSM = 1024
HEAD_DIM = 64
N_HEADS = 16
N_GROUPS = 2
HEADS_PER_GROUP = N_HEADS // N_GROUPS
N_STATE = 128
GROUP_COLS = HEADS_PER_GROUP * HEAD_DIM
D_CONV = D_SSM + 2 * N_GROUPS * N_STATE
D_MAIN = D_POOL + D_SSM + D_CONV
N_EXPERTS = 32
TOP_K = 4
D_EXPERT = 2048
SWIGLU_LIMIT = 7.0
SWIGLU_ALPHA = 1.702
NORM_EPS = 1e-5

POOL_HALO = 16
CONV_HALO = 8

MOE_SUB = 256
MOE_ROWS = 1280
MOE_SUBS = MOE_ROWS // MOE_SUB
ZERO_ROWS = MOE_SUB + 8
EXPERT_CAP = -(-(SEQ + ZERO_ROWS) // MOE_ROWS) * MOE_ROWS
MOE_TN = 512
MOE_NJ = D_EXPERT // MOE_TN
MOE_WORK = (SEQ * TOP_K) // MOE_ROWS + N_EXPERTS
MOE_TILES = 3 * MOE_NJ
MOE_RING = 6
assert MOE_TILES % MOE_RING == 0

VMEM_LIMIT = 56 * 1024 * 1024


def _cparams(sem):
    return pltpu.CompilerParams(dimension_semantics=sem, vmem_limit_bytes=VMEM_LIMIT)


def _inproj_kernel(x_ref, g_ref, w_ref, wdt_ref, proj_ref, dt_ref, hn_ref):
    j = pl.program_id(1)

    @pl.when(j == 0)
    def _():
        x = x_ref[...]
        y = x * lax.rsqrt(jnp.mean(x * x, axis=-1, keepdims=True) + NORM_EPS)
        hn = (y * g_ref[...]).astype(bf16)
        hn_ref[...] = hn
        dt_ref[...] = jnp.dot(hn, wdt_ref[...].astype(bf16), preferred_element_type=f32)

    proj_ref[...] = jnp.dot(hn_ref[...], w_ref[...].astype(bf16),
                            preferred_element_type=f32)


def _inproj(x2d, g, w_in, w_dt, tm, tn=512):
    m = x2d.shape[0]
    return pl.pallas_call(
        _inproj_kernel,
        grid=(m // tm, D_MAIN // tn),
        in_specs=[
            pl.BlockSpec((tm, D_MODEL), lambda i, j: (i, 0)),
            pl.BlockSpec((1, D_MODEL), lambda i, j: (0, 0)),
            pl.BlockSpec((D_MODEL, tn), lambda i, j: (0, j)),
            pl.BlockSpec((D_MODEL, N_HEADS), lambda i, j: (0, 0)),
        ],
        out_specs=[
            pl.BlockSpec((tm, tn), lambda i, j: (i, j)),
            pl.BlockSpec((tm, N_HEADS), lambda i, j: (i, 0)),
        ],
        out_shape=[
            jax.ShapeDtypeStruct((m, D_MAIN), f32),
            jax.ShapeDtypeStruct((m, N_HEADS), f32),
        ],
        scratch_shapes=[pltpu.VMEM((tm, D_MODEL), bf16)],
        compiler_params=_cparams(("arbitrary", "arbitrary")),
        name="inproj",
    )(x2d, g, w_in, w_dt)


def _split_dot(q, e):
    hi = q.astype(bf16)
    lo = (q - hi.astype(f32)).astype(bf16)
    return (jnp.dot(hi, e, preferred_element_type=f32)
            + jnp.dot(lo, e, preferred_element_type=f32))


def _nt_dot(a, b):
    return lax.dot_general(a, b, (((1,), (1,)), ((), ())), preferred_element_type=f32)


def _mixer_kernel(u_ref, z_ref, xs_ref, bc_ref, dt_ref,
                  poolw_ref, pscale_ref, convw_ref, convb_ref, dtb_ref, alog_ref,
                  dskip_ref, ng_ref, expand_ref, state0_ref, uhalo0_ref, chalo0_ref,
                  y_ref, state_out_ref, utail_ref, ctail_ref,
                  state, uext, cext, *, tm, lead):
    i = pl.program_id(0)
    nch = tm // CHUNK

    @pl.when(i == 0)
    def _():
        state[...] = state0_ref[...]
        uext[0:POOL_HALO, :] = uhalo0_ref[...]
        cext[0:CONV_HALO, :] = chalo0_ref[...]

    uext[POOL_HALO:, :] = u_ref[...]
    cext[CONV_HALO:, 0:D_SSM] = xs_ref[...]
    cext[CONV_HALO:, D_SSM:] = bc_ref[...]

    for g, w in enumerate(POOL_WINDOWS):
        cols = slice(g * POOL_GROUP, (g + 1) * POOL_GROUP)
        ue = uext[:, cols]
        s = ue
        sh = 1
        while sh < w:
            s = s + pltpu.roll(s, sh, axis=0)
            sh *= 2
        d = s[POOL_HALO:, :] / float(w) - ue[POOL_HALO:, :]
        yp = jnp.dot(d.astype(bf16), poolw_ref[g].astype(bf16), preferred_element_type=f32)
        y_ref[:, cols] = (yp * pscale_ref[:, cols]).astype(bf16)

    ce = cext[...]
    cw = convw_ref[...]
    acc = (ce * cw[3:4, :] + pltpu.roll(ce, 1, axis=0) * cw[2:3, :]
           + pltpu.roll(ce, 2, axis=0) * cw[1:2, :] + pltpu.roll(ce, 3, axis=0) * cw[0:1, :])
    acc = acc[CONV_HALO:, :] + convb_ref[...]
    xbc = acc * jax.nn.sigmoid(acc)
    xs = xbc[:, 0:D_SSM]
    bm = xbc[:, D_SSM:D_SSM + N_GROUPS * N_STATE]
    cm = xbc[:, D_SSM + N_GROUPS * N_STATE:]

    uext[0:POOL_HALO, :] = uext[tm:tm + POOL_HALO, :]
    cext[0:CONV_HALO, :] = cext[tm:tm + CONV_HALO, :]

    v = dt_ref[...] + dtb_ref[...]
    dt = jnp.maximum(v, 0.0) + jnp.log1p(jnp.exp(-jnp.abs(v)))
    row = lax.broadcasted_iota(i32, (tm, N_HEADS), 0)
    if lead:
        dt = jnp.where(row >= lead, dt, 0.0)
    a = -jnp.exp(alog_ref[...])
    acs = dt * a
    rin = row % CHUNK
    sh = 1
    while sh < CHUNK:
        acs = acs + jnp.where(rin >= sh, pltpu.roll(acs, sh, axis=0), 0.0)
        sh *= 2
    alast = jnp.concatenate(
        [jnp.broadcast_to(acs[c * CHUNK + CHUNK - 1:c * CHUNK + CHUNK, :], (CHUNK, N_HEADS))
         for c in range(nch)], axis=0)
    dec_st = jnp.exp(alast - acs)
    dec_in = jnp.exp(acs)

    expand = expand_ref[...]
    dt_x = _split_dot(dt, expand)
    dtds_x = _split_dot(dt * dec_st, expand)
    decin_x = _split_dot(dec_in, expand)

    eye_h = (lax.broadcasted_iota(i32, (N_HEADS, N_HEADS), 0)
             == lax.broadcasted_iota(i32, (N_HEADS, N_HEADS), 1)).astype(bf16)
    p1 = acs.astype(bf16)
    r1 = acs - p1.astype(f32)
    p2 = r1.astype(bf16)
    p3 = (r1 - p2.astype(f32)).astype(bf16)
    acs_t = _nt_dot(eye_h, p1) + _nt_dot(eye_h, p2) + _nt_dot(eye_h, p3)

    xdt = (xs * dt_x).astype(bf16)
    xds = (xs * dtds_x).astype(bf16)

    eye_n = (lax.broadcasted_iota(i32, (N_STATE, N_STATE), 0)
             == lax.broadcasted_iota(i32, (N_STATE, N_STATE), 1)).astype(bf16)
    causal = (lax.broadcasted_iota(i32, (CHUNK, CHUNK), 0)
              >= lax.broadcasted_iota(i32, (CHUNK, CHUNK), 1))
    col_head = lax.broadcasted_iota(i32, (CHUNK, GROUP_COLS), 1) // HEAD_DIM

    y_chunks = []
    for c in range(nch):
        r0 = c * CHUNK
        y_groups = []
        for g in range(N_GROUPS):
            gc = slice(g * GROUP_COLS, (g + 1) * GROUP_COLS)
            cg = cm[r0:r0 + CHUNK, g * N_STATE:(g + 1) * N_STATE].astype(bf16)
            bg = bm[r0:r0 + CHUNK, g * N_STATE:(g + 1) * N_STATE].astype(bf16)
            gmat = _nt_dot(cg, bg)
            st = state[g]
            y_off = (jnp.dot(cg, st.astype(bf16), preferred_element_type=f32)
                     * decin_x[r0:r0 + CHUNK, gc])
            scores = []
            for h in range(HEADS_PER_GROUP):
                hh = g * HEADS_PER_GROUP + h
                a_col = acs[r0:r0 + CHUNK, hh:hh + 1]
                a_row = acs_t[hh:hh + 1, r0:r0 + CHUNK]
                lmat = jnp.exp(jnp.where(causal, a_col - a_row, -jnp.inf))
                scores.append((gmat * lmat).astype(bf16))
            sc = jnp.concatenate(scores, axis=0)
            full = jnp.dot(sc, xdt[r0:r0 + CHUNK, gc], preferred_element_type=f32)
            y_diag = jnp.zeros((CHUNK, GROUP_COLS), f32)
            for h in range(HEADS_PER_GROUP):
                y_diag = jnp.where(col_head == h, full[h * CHUNK:(h + 1) * CHUNK, :], y_diag)
            y_groups.append(y_diag + y_off)
            bg_t = _nt_dot(eye_n, bg).astype(bf16)
            upd = jnp.dot(bg_t, xds[r0:r0 + CHUNK, gc], preferred_element_type=f32)
            state[g] = st * decin_x[r0 + CHUNK - 1:r0 + CHUNK, gc] + upd
        y_chunks.append(jnp.concatenate(y_groups, axis=1))
    y = jnp.concatenate(y_chunks, axis=0) if nch > 1 else y_chunks[0]

    y = y + dskip_ref[...] * xs
    zz = z_ref[...]
    y = y * (zz * jax.nn.sigmoid(zz))
    for g in range(N_GROUPS):
        gc = slice(g * GROUP_COLS, (g + 1) * GROUP_COLS)
        yg = y[:, gc]
        yg = yg * lax.rsqrt(jnp.mean(yg * yg, axis=-1, keepdims=True) + NORM_EPS)
        y_ref[:, D_POOL + g * GROUP_COLS:D_POOL + (g + 1) * GROUP_COLS] = (
            yg * ng_ref[:, gc]).astype(bf16)

    state_out_ref[...] = state[...]
    utail_ref[...] = uext[0:POOL_HALO, :]
    ctail_ref[...] = cext[0:CONV_HALO, :]


def _mixer(proj, dt_raw, params, state0, uhalo0, chalo0, tm, lead):
    m = proj.shape[0]
    (pool_w, pool_scale, conv_w, conv_b, dt_bias, a_log, dskip_x, norm_g, expand) = params
    full = lambda shape: pl.BlockSpec(shape, lambda i: (0,) * len(shape))
    return pl.pallas_call(
        functools.partial(_mixer_kernel, tm=tm, lead=lead),
        grid=(m // tm,),
        in_specs=[
            pl.BlockSpec((tm, D_POOL), lambda i: (i, 0)),
            pl.BlockSpec((tm, D_SSM), lambda i: (i, 1)),
            pl.BlockSpec((tm, D_SSM), lambda i: (i, 2)),
            pl.BlockSpec((tm, 2 * N_GROUPS * N_STATE), lambda i: (i, 6)),
            pl.BlockSpec((tm, N_HEADS), lambda i: (i, 0)),
            full((len(POOL_WINDOWS), POOL_GROUP, POOL_GROUP)),
            full((1, D_POOL)),
            full((4, D_CONV)),
            full((1, D_CONV)),
            full((1, N_HEADS)),
            full((1, N_HEADS)),
            full((1, D_SSM)),
            full((1, D_SSM)),
            full((N_HEADS, D_SSM)),
            full((N_GROUPS, N_STATE, GROUP_COLS)),
            full((POOL_HALO, D_POOL)),
            full((CONV_HALO, D_CONV)),
        ],
        out_specs=[
            pl.BlockSpec((tm, D_MODEL), lambda i: (i, 0)),
            full((N_GROUPS, N_STATE, GROUP_COLS)),
            full((POOL_HALO, D_POOL)),
            full((CONV_HALO, D_CONV)),
        ],
        out_shape=[
            jax.ShapeDtypeStruct((m, D_MODEL), bf16),
            jax.ShapeDtypeStruct((N_GROUPS, N_STATE, GROUP_COLS), f32),
            jax.ShapeDtypeStruct((POOL_HALO, D_POOL), f32),
            jax.ShapeDtypeStruct((CONV_HALO, D_CONV), f32),
        ],
        scratch_shapes=[
            pltpu.VMEM((N_GROUPS, N_STATE, GROUP_COLS), f32),
            pltpu.VMEM((tm + POOL_HALO, D_POOL), f32),
            pltpu.VMEM((tm + CONV_HALO, D_CONV), f32),
        ],
        compiler_params=_cparams(("arbitrary",)),
        name="mixer",
    )(proj, proj, proj, proj, dt_raw, pool_w, pool_scale, conv_w, conv_b, dt_bias, a_log,
      dskip_x, norm_g, expand, state0, uhalo0, chalo0)


def _outproj_kernel(y_ref, w_ref, x_ref, g2_ref, rw_ref, rb_ref,
                    h2_ref, hn_ref, pos_ref, gate_ref, cnt_ref,
                    hbuf, cnt, *, tm, tn):
    i = pl.program_id(0)
    j = pl.program_id(1)
    nj = D_MODEL // tn

    @pl.when((i == 0) & (j == 0))
    def _():
        cnt[...] = jnp.zeros_like(cnt)

    h = x_ref[...] + jnp.dot(y_ref[...], w_ref[...].astype(bf16), preferred_element_type=f32)
    h2_ref[...] = h
    hbuf[j] = h

    @pl.when(j == nj - 1)
    def _():
        ss = jnp.zeros((tm, 1), f32)
        for q in range(nj):
            hq = hbuf[q]
            ss = ss + jnp.sum(hq * hq, axis=-1, keepdims=True)
        inv = lax.rsqrt(ss / float(D_MODEL) + NORM_EPS)
        logits = jnp.zeros((tm, N_EXPERTS), f32) + rb_ref[...]
        for q in range(nj):
            hn = hbuf[q] * inv * g2_ref[:, q * tn:(q + 1) * tn]
            hn_ref[:, q * tn:(q + 1) * tn] = hn
            logits = logits + jnp.dot(hn.astype(bf16),
                                      rw_ref[q * tn:(q + 1) * tn, :].astype(bf16),
                                      preferred_element_type=f32)

        lane = lax.broadcasted_iota(i32, (tm, N_EXPERTS), 1).astype(f32)
        vals = logits
        picked = jnp.zeros((tm, N_EXPERTS), f32)
        tops, idxs = [], []
        for _ in range(TOP_K):
            mx = jnp.max(vals, axis=-1, keepdims=True)
            idx = jnp.min(jnp.where(vals == mx, lane, float(N_EXPERTS)), axis=-1, keepdims=True)
            hit = lane == idx
            tops.append(mx)
            idxs.append(idx)
            picked = picked + hit.astype(f32)
            vals = jnp.where(hit, -jnp.inf, vals)
        exps = [jnp.exp(t - tops[0]) for t in tops]
        denom = exps[0] + exps[1] + exps[2] + exps[3]

        tri = (lax.broadcasted_iota(i32, (tm, tm), 0)
               > lax.broadcasted_iota(i32, (tm, tm), 1)).astype(bf16)
        cum = jnp.dot(tri, picked.astype(bf16), preferred_element_type=f32) + cnt[...]
        lane4 = lax.broadcasted_iota(i32, (tm, TOP_K), 1)
        pos = jnp.zeros((tm, TOP_K), i32)
        gates = jnp.zeros((tm, TOP_K), f32)
        for k in range(TOP_K):
            rank = jnp.sum(jnp.where(lane == idxs[k], cum, 0.0), axis=-1, keepdims=True)
            p = idxs[k].astype(i32) * EXPERT_CAP + rank.astype(i32)
            pos = jnp.where(lane4 == k, p, pos)
            gates = jnp.where(lane4 == k, exps[k] / denom, gates)
        pos_ref[...] = pos
        gate_ref[...] = gates
        cnt[...] = cnt[...] + jnp.sum(picked, axis=0, keepdims=True)
        cnt_ref[...] = cnt[...]


def _outproj(ymix, w_out, x2d, g2, rw, rb, tm=1024, tn=512):
    m = x2d.shape[0]
    return pl.pallas_call(
        functools.partial(_outproj_kernel, tm=tm, tn=tn),
        grid=(m // tm, D_MODEL // tn),
        in_specs=[
            pl.BlockSpec((tm, D_MODEL), lambda i, j: (i, 0)),
            pl.BlockSpec((D_MODEL, tn), lambda i, j: (0, j)),
            pl.BlockSpec((tm, tn), lambda i, j: (i, j)),
            pl.BlockSpec((1, D_MODEL), lambda i, j: (0, 0)),
            pl.BlockSpec((D_MODEL, N_EXPERTS), lambda i, j: (0, 0)),
            pl.BlockSpec((1, N_EXPERTS), lambda i, j: (0, 0)),
        ],
        out_specs=[
            pl.BlockSpec((tm, tn), lambda i, j: (i, j)),
            pl.BlockSpec((tm, D_MODEL), lambda i, j: (i, 0)),
            pl.BlockSpec((tm, TOP_K), lambda i, j: (i, 0)),
            pl.BlockSpec((tm, TOP_K), lambda i, j: (i, 0)),
            pl.BlockSpec((1, N_EXPERTS), lambda i, j: (0, 0)),
        ],
        out_shape=[
            jax.ShapeDtypeStruct((m, D_MODEL), f32),
            jax.ShapeDtypeStruct((m, D_MODEL), f32),
            jax.ShapeDtypeStruct((m, TOP_K), i32),
            jax.ShapeDtypeStruct((m, TOP_K), f32),
            jax.ShapeDtypeStruct((1, N_EXPERTS), f32),
        ],
        scratch_shapes=[
            pltpu.VMEM((D_MODEL // tn, tm, tn), f32),
            pltpu.VMEM((1, N_EXPERTS), f32),
        ],
        compiler_params=_cparams(("arbitrary", "arbitrary")),
        name="outproj_router",
    )(ymix, w_out, x2d, g2, rw, rb)


def _scatter_kernel(pos_ref, cnt_ref, hn_ref, xs_ref, zbuf, sem, zsem, *, tb):
    i = pl.program_id(0)

    @pl.when(i == 0)
    def _():
        zbuf[...] = jnp.zeros_like(zbuf)
        for e in range(N_EXPERTS):
            start = pl.multiple_of(e * EXPERT_CAP + (cnt_ref[e] // 8) * 8, 8)
            pltpu.make_async_copy(zbuf, xs_ref.at[pl.ds(start, ZERO_ROWS)], zsem).start()
        for e in range(N_EXPERTS):
            start = pl.multiple_of(e * EXPERT_CAP + (cnt_ref[e] // 8) * 8, 8)
            pltpu.make_async_copy(zbuf, xs_ref.at[pl.ds(start, ZERO_ROWS)], zsem).wait()

    def body(t, carry):
        for k in range(TOP_K):
            p = pos_ref[(i * tb + t) * TOP_K + k]
            pltpu.make_async_copy(hn_ref.at[pl.ds(t, 1)], xs_ref.at[pl.ds(p, 1)], sem).start()
        return carry

    lax.fori_loop(0, tb, body, 0, unroll=8)
    done = xs_ref.at[pl.ds(0, tb * TOP_K)]
    pltpu.make_async_copy(done, done, sem).wait()


def _scatter(pos_flat, cnt_i, hn2, tb=512):
    m = hn2.shape[0]
    return pl.pallas_call(
        functools.partial(_scatter_kernel, tb=tb),
        grid_spec=pltpu.PrefetchScalarGridSpec(
            num_scalar_prefetch=2,
            grid=(m // tb,),
            in_specs=[pl.BlockSpec((tb, D_MODEL), lambda i, pos, cnt: (i, 0))],
            out_specs=pl.BlockSpec(memory_space=pl.ANY),
            scratch_shapes=[
                pltpu.VMEM((ZERO_ROWS, D_MODEL), f32),
                pltpu.SemaphoreType.DMA(()),
                pltpu.SemaphoreType.DMA(()),
            ],
        ),
        out_shape=jax.ShapeDtypeStruct((N_EXPERTS * EXPERT_CAP, D_MODEL), f32),
        compiler_params=_cparams(("arbitrary",)),
        name="scatter_rows",
    )(pos_flat, cnt_i, hn2)


def _moe_kernel(we_ref, wq_ref, wn_ref, tot_ref,
                xs_ref, wg_hbm, wu_hbm, wd_hbm, bg_ref, bu_ref, bd_ref,
                o_ref, stage, xb, abuf, ws_a, ws_b, ring, sem, wsem):
    w = pl.program_id(0)
    j = pl.program_id(1)
    n = wn_ref[w]
    wnext = jnp.minimum(w + 1, MOE_WORK - 1)
    n_next = jnp.where(w + 1 < MOE_WORK, wn_ref[wnext], 0)

    def tile_copy(item, r):
        e = we_ref[item]
        if r < 2 * MOE_NJ:
            src, blk = (wg_hbm if r % 2 == 0 else wu_hbm), r // 2
        else:
            src, blk = wd_hbm, r - 2 * MOE_NJ
        return pltpu.make_async_copy(src.at[e, :, pl.ds(blk * MOE_TN, MOE_TN)],
                                     ring.at[r % MOE_RING], wsem.at[r % MOE_RING])

    def tile_take(r, dst):
        tile_copy(w, r).wait()
        dst[...] = ring[r % MOE_RING].astype(bf16)
        r2 = r + MOE_RING
        if r2 < MOE_TILES:
            tile_copy(w, r2).start()
        else:
            @pl.when(w + 1 < tot_ref[0])
            def _():
                tile_copy(wnext, r2 - MOE_TILES).start()

    @pl.when((w == 0) & (j == 0))
    def _():
        for r in range(MOE_RING):
            tile_copy(w, r).start()

    def x_copy(item, s):
        base = we_ref[item] * EXPERT_CAP + wq_ref[item] * MOE_ROWS
        return pltpu.make_async_copy(
            xs_ref.at[pl.ds(pl.multiple_of(base + s * MOE_SUB, MOE_SUB), MOE_SUB)],
            stage.at[s % 2], sem.at[s % 2])

    def x_start(item, count, s):
        @pl.when(s < count)
        def _():
            x_copy(item, s).start()

    def x_finish(item, count, s):
        @pl.when(s < count)
        def _():
            x_copy(item, s).wait()
            xb[pl.ds(s * MOE_SUB, MOE_SUB), :] = stage[s % 2].astype(bf16)

    @pl.when((w == 0) & (j == 0))
    def _():
        x_start(w, n, 0)
        x_start(w, n, 1)
        for s in range(MOE_SUBS):
            x_finish(w, n, s)
            if s + 2 < MOE_SUBS:
                x_start(w, n, s + 2)

    def per_subtile(fn):
        def pair(p, carry):
            r0 = p * (2 * MOE_SUB)
            fn(r0)
            fn(r0 + MOE_SUB)
            return carry

        lax.fori_loop(0, n // 2, pair, 0)

        @pl.when(n % 2 == 1)
        def _():
            fn((n - 1) * MOE_SUB)

    for jj in range(2 * MOE_NJ):
        @pl.when((j == jj) & (n > 0))
        def _():
            if jj < MOE_NJ:
                tile_take(2 * jj, ws_a)
                tile_take(2 * jj + 1, ws_b)
            else:
                tile_take(MOE_NJ + jj, ws_a)

    @pl.when((j < MOE_NJ) & (n > 0))
    def _():
        bgv = bg_ref[0]
        buv = bu_ref[0]

        def gate_up(r0):
            rows = pl.ds(pl.multiple_of(r0, MOE_SUB), MOE_SUB)
            x = xb[rows, :]
            g = jnp.dot(x, ws_a[...], preferred_element_type=f32) + bgv
            u = jnp.dot(x, ws_b[...], preferred_element_type=f32) + buv
            g = jnp.minimum(g, SWIGLU_LIMIT)
            u = jnp.clip(u, -SWIGLU_LIMIT, SWIGLU_LIMIT)
            act = (u + 1.0) * (g * jax.nn.sigmoid(SWIGLU_ALPHA * g))
            abuf[j, rows, :] = act.astype(bf16)

        per_subtile(gate_up)

    @pl.when((j >= MOE_NJ) & (n > 0))
    def _():
        bdv = bd_ref[0]

        def down(r0):
            rows = pl.ds(pl.multiple_of(r0, MOE_SUB), MOE_SUB)
            acc = jnp.zeros((MOE_SUB, MOE_TN), f32) + bdv
            for q in range(MOE_NJ):
                acc = acc + jnp.dot(abuf[q, rows, :], ws_a[q * MOE_TN:(q + 1) * MOE_TN, :],
                                    preferred_element_type=f32)
            o_ref[rows, :] = acc

        per_subtile(down)

    assert 2 * (MOE_NJ - 1) >= MOE_SUBS
    for k in range(MOE_NJ):
        @pl.when(j == MOE_NJ + k)
        def _():
            if k == 0:
                x_start(wnext, n_next, 0)
                x_start(wnext, n_next, 1)
            else:
                for s in (2 * k - 2, 2 * k - 1):
                    if s < MOE_SUBS:
                        x_finish(wnext, n_next, s)
                        if s + 2 < MOE_SUBS:
                            x_start(wnext, n_next, s + 2)


def _moe(we, wq, wn, tot, xs, w_gate, w_up, w_down, b_gate, b_up, b_down):
    blocks_per_expert = EXPERT_CAP // MOE_ROWS

    def b_a(w, j, we, wq, wn, tot):
        return (we[w], 0, jnp.minimum(j, MOE_NJ - 1))

    def b_b(w, j, we, wq, wn, tot):
        return (we[w], 0, jnp.maximum(j - MOE_NJ, 0))

    def o_map(w, j, we, wq, wn, tot):
        jj = jnp.where(wn[w] > 0, jnp.maximum(j - MOE_NJ, 0), MOE_NJ - 1)
        return (we[w] * blocks_per_expert + wq[w], jj)

    any_space = pl.BlockSpec(memory_space=pl.ANY)
    return pl.pallas_call(
        _moe_kernel,
        grid_spec=pltpu.PrefetchScalarGridSpec(
            num_scalar_prefetch=4,
            grid=(MOE_WORK, 2 * MOE_NJ),
            in_specs=[
                any_space, any_space, any_space, any_space,
                pl.BlockSpec((1, 1, MOE_TN), b_a),
                pl.BlockSpec((1, 1, MOE_TN), b_a),
                pl.BlockSpec((1, 1, MOE_TN), b_b),
            ],
            out_specs=pl.BlockSpec((MOE_ROWS, MOE_TN), o_map),
            scratch_shapes=[
                pltpu.VMEM((2, MOE_SUB, D_MODEL), f32),
                pltpu.VMEM((MOE_ROWS, D_MODEL), bf16),
                pltpu.VMEM((MOE_NJ, MOE_ROWS, MOE_TN), bf16),
                pltpu.VMEM((D_MODEL, MOE_TN), bf16),
                pltpu.VMEM((D_MODEL, MOE_TN), bf16),
                pltpu.VMEM((MOE_RING, D_MODEL, MOE_TN), f32),
                pltpu.SemaphoreType.DMA((2,)),
                pltpu.SemaphoreType.DMA((MOE_RING,)),
            ],
        ),
        out_shape=jax.ShapeDtypeStruct((N_EXPERTS * EXPERT_CAP, D_MODEL), f32),
        compiler_params=_cparams(("arbitrary", "arbitrary")),
        name="moe_experts",
    )(we, wq, wn, tot, xs, w_gate, w_up, w_down,
      b_gate.reshape(N_EXPERTS, 1, D_EXPERT), b_up.reshape(N_EXPERTS, 1, D_EXPERT),
      b_down.reshape(N_EXPERTS, 1, D_MODEL))


def _work_list(cnt_i):
    nq = (cnt_i + MOE_ROWS - 1) // MOE_ROWS
    ends = jnp.cumsum(nq)
    total = ends[-1]
    w = jnp.arange(MOE_WORK, dtype=i32)
    wc = jnp.minimum(w, total - 1)
    e = jnp.minimum(jnp.sum((ends[None, :] <= wc[:, None]).astype(i32), axis=1), N_EXPERTS - 1)
    q = wc - (ends[e] - nq[e])
    rows = jnp.clip(cnt_i[e] - q * MOE_ROWS, 0, MOE_ROWS)
    n = jnp.where(w < total, (rows + MOE_SUB - 1) // MOE_SUB, 0)
    return e.astype(i32), q.astype(i32), n.astype(i32), total.astype(i32).reshape(1)


def _combine_kernel(pos_ref, o_ref, gate_ref, h2_ref, fg_ref, out_ref, obuf, sem, *, tm):
    i = pl.program_id(0)

    def body(t, carry):
        tok = i * tm + t
        for k in range(TOP_K):
            p = pos_ref[tok * TOP_K + k]
            pltpu.make_async_copy(o_ref.at[pl.ds(p, 1)], obuf.at[pl.ds(k * tm + t, 1)], sem).start()
        return carry

    lax.fori_loop(0, tm, body, 0, unroll=8)
    pltpu.make_async_copy(obuf, obuf, sem).wait()

    gates = gate_ref[...]
    y = h2_ref[...]
    for k in range(TOP_K):
        y = y + gates[:, k:k + 1] * obuf[k * tm:(k + 1) * tm, :]
    out = y * lax.rsqrt(jnp.mean(y * y, axis=-1, keepdims=True) + NORM_EPS)
    out_ref[...] = out * fg_ref[...]


def _combine(pos_flat, o, gates, h2, final_g, tm=256):
    m = h2.shape[0]
    return pl.pallas_call(
        functools.partial(_combine_kernel, tm=tm),
        grid_spec=pltpu.PrefetchScalarGridSpec(
            num_scalar_prefetch=1,
            grid=(m // tm,),
            in_specs=[
                pl.BlockSpec(memory_space=pl.ANY),
                pl.BlockSpec((tm, TOP_K), lambda i, pos: (i, 0)),
                pl.BlockSpec((tm, D_MODEL), lambda i, pos: (i, 0)),
                pl.BlockSpec((1, D_MODEL), lambda i, pos: (0, 0)),
            ],
            out_specs=pl.BlockSpec((tm, D_MODEL), lambda i, pos: (i, 0)),
            scratch_shapes=[
                pltpu.VMEM((TOP_K * tm, D_MODEL), f32),
                pltpu.SemaphoreType.DMA(()),
            ],
        ),
        out_shape=jax.ShapeDtypeStruct((m, D_MODEL), f32),
        compiler_params=_cparams(("arbitrary",)),
        name="combine",
    )(pos_flat, o, gates, h2, final_g)


def kernel(x, meta_tokens, norm1_g, w_in, pool_w, pool_scale, conv_w, conv_b, dt_bias, a_log,
           d_skip, ssm_norm_g, w_out, norm2_g, router_w, router_b, w_gate, b_gate, w_up, b_up,
           w_down, b_down, final_g):
    assert x.shape == (1, SEQ, D_MODEL)
    x2d = x.reshape(SEQ, D_MODEL)
    lead = CHUNK - N_META
    meta_chunk = jnp.concatenate([jnp.zeros((lead, D_MODEL), f32), meta_tokens.astype(f32)], 0)

    g1 = norm1_g[0].reshape(1, D_MODEL)
    w_in0 = w_in[0]
    w_dt = w_in0[:, D_MAIN:]
    head_of_col = jnp.arange(D_SSM, dtype=i32) // HEAD_DIM
    expand = (head_of_col[None, :] == jnp.arange(N_HEADS, dtype=i32)[:, None]).astype(bf16)
    mix_params = (
        pool_w[0], pool_scale[0].reshape(1, D_POOL), conv_w[0], conv_b[0].reshape(1, D_CONV),
        dt_bias[0].reshape(1, N_HEADS), a_log[0].reshape(1, N_HEADS),
        jnp.repeat(d_skip[0], HEAD_DIM).reshape(1, D_SSM), ssm_norm_g[0].reshape(1, D_SSM),
        expand)

    proj_m, dt_m = _inproj(meta_chunk, g1, w_in0, w_dt, tm=CHUNK)
    zeros_state = jnp.zeros((N_GROUPS, N_STATE, GROUP_COLS), f32)
    _, state_m, utail_m, ctail_m = _mixer(
        proj_m, dt_m, mix_params, zeros_state, jnp.zeros((POOL_HALO, D_POOL), f32),
        jnp.zeros((CONV_HALO, D_CONV), f32), tm=CHUNK, lead=lead)

    proj, dt_raw = _inproj(x2d, g1, w_in0, w_dt, tm=1024)
    ymix, _, _, _ = _mixer(proj, dt_raw, mix_params, state_m, utail_m, ctail_m, tm=256, lead=0)

    h2, hn2, pos, gates, cnt = _outproj(
        ymix, w_out[0], x2d, norm2_g[0].reshape(1, D_MODEL), router_w[0],
        router_b[0].reshape(1, N_EXPERTS))

    pos_flat = pos.reshape(SEQ * TOP_K)
    cnt_i = cnt.reshape(N_EXPERTS).astype(i32)
    xs = _scatter(pos_flat, cnt_i, hn2)
    we, wq, wn, tot = _work_list(cnt_i)
    o = _moe(we, wq, wn, tot, xs, w_gate[0], w_up[0], w_down[0], b_gate[0], b_up[0], b_down[0])
    out = _combine(pos_flat, o, gates, h2, final_g.reshape(1, D_MODEL))
    return out.reshape(1, SEQ, D_MODEL)
```

```python
import functools

import jax
import jax.numpy as jnp
from jax import lax
from jax.experimental import pallas as pl
from jax.experimental.pallas import tpu as pltpu

f32 = jnp.float32
bf16 = jnp.bfloat16
i32 = jnp.int32

D_MODEL = 2048
SEQ = 8192
CHUNK = 64
N_META = 16
D_POOL = 1024
POOL_WINDOWS = (2, 4, 8, 16)
POOL_GROUP = 256
D_SSM = 1024
HEAD_DIM = 64
N_HEADS = 16
N_GROUPS = 2
HEADS_PER_GROUP = N_HEADS // N_GROUPS
N_STATE = 128
GROUP_COLS = HEADS_PER_GROUP * HEAD_DIM
D_CONV = D_SSM + 2 * N_GROUPS * N_STATE
D_MAIN = D_POOL + D_SSM + D_CONV
N_EXPERTS = 32
TOP_K = 4
D_EXPERT = 2048
SWIGLU_LIMIT = 7.0
SWIGLU_ALPHA = 1.702
NORM_EPS = 1e-5

POOL_HALO = 16
CONV_HALO = 8

MOE_SUB = 256
MOE_ROWS = 1280
MOE_SUBS = MOE_ROWS // MOE_SUB
ZERO_ROWS = MOE_SUB + 8
EXPERT_CAP = -(-(SEQ + ZERO_ROWS) // MOE_ROWS) * MOE_ROWS
MOE_TN = 512
MOE_NJ = D_EXPERT // MOE_TN
MOE_WORK = (SEQ * TOP_K) // MOE_ROWS + N_EXPERTS
MOE_TILES = 3 * MOE_NJ
MOE_RING = 6
assert MOE_TILES % MOE_RING == 0

VMEM_LIMIT = 56 * 1024 * 1024


def _cparams(sem):
    return pltpu.CompilerParams(dimension_semantics=sem, vmem_limit_bytes=VMEM_LIMIT)


PROJ_TN = 512


def _inproj_kernel(x_ref, g_ref, w_ref, wdt_ref, proj_ref, dt_ref):
    x = x_ref[...]
    y = x * lax.rsqrt(jnp.mean(x * x, axis=-1, keepdims=True) + NORM_EPS)
    hn = (y * g_ref[...]).astype(bf16)
    dt_ref[...] = jnp.dot(hn, wdt_ref[...], preferred_element_type=f32)
    for c in range(D_MAIN // PROJ_TN):
        cols = slice(c * PROJ_TN, (c + 1) * PROJ_TN)
        proj_ref[:, cols] = jnp.dot(hn, w_ref[:, cols], preferred_element_type=f32)


def _inproj(x2d, g, w_main, w_dt, tm):
    m = x2d.shape[0]
    resident = dict(pipeline_mode=pl.Buffered(1))
    return pl.pallas_call(
        _inproj_kernel,
        grid=(m // tm,),
        in_specs=[
            pl.BlockSpec((tm, D_MODEL), lambda i: (i, 0)),
            pl.BlockSpec((1, D_MODEL), lambda i: (0, 0)),
            pl.BlockSpec((D_MODEL, D_MAIN), lambda i: (0, 0), **resident),
            pl.BlockSpec((D_MODEL, N_HEADS), lambda i: (0, 0), **resident),
        ],
        out_specs=[
            pl.BlockSpec((tm, D_MAIN), lambda i: (i, 0)),
            pl.BlockSpec((tm, N_HEADS), lambda i: (i, 0)),
        ],
        out_shape=[
            jax.ShapeDtypeStruct((m, D_MAIN), f32),
            jax.ShapeDtypeStruct((m, N_HEADS), f32),
        ],
        compiler_params=_cparams(("arbitrary",)),
        name="inproj",
    )(x2d, g, w_main, w_dt)


def _split_dot(q, e):
    hi = q.astype(bf16)
    lo = (q - hi.astype(f32)).astype(bf16)
    return (jnp.dot(hi, e, preferred_element_type=f32)
            + jnp.dot(lo, e, preferred_element_type=f32))


def _nt_dot(a, b):
    return lax.dot_general(a, b, (((1,), (1,)), ((), ())), preferred_element_type=f32)


def _mixer_kernel(u_ref, z_ref, xs_ref, bc_ref, dt_ref,
                  poolw_ref, pscale_ref, convw_ref, convb_ref, dtb_ref, alog_ref,
                  dskip_ref, ng_ref, expand_ref, state0_ref, uhalo0_ref, chalo0_ref,
                  y_ref, state_out_ref, utail_ref, ctail_ref,
                  state, uext, cext, *, tm, lead):
    i = pl.program_id(0)
    nch = tm // CHUNK

    @pl.when(i == 0)
    def _():
        state[...] = state0_ref[...]
        uext[0:POOL_HALO, :] = uhalo0_ref[...]
        cext[0:CONV_HALO, :] = chalo0_ref[...]

    uext[POOL_HALO:, :] = u_ref[...]
    cext[CONV_HALO:, 0:D_SSM] = xs_ref[...]
    cext[CONV_HALO:, D_SSM:] = bc_ref[...]

    for g, w in enumerate(POOL_WINDOWS):
        cols = slice(g * POOL_GROUP, (g + 1) * POOL_GROUP)
        ue = uext[:, cols]
        s = ue
        sh = 1
        while sh < w:
            s = s + pltpu.roll(s, sh, axis=0)
            sh *= 2
        d = s[POOL_HALO:, :] / float(w) - ue[POOL_HALO:, :]
        yp = jnp.dot(d.astype(bf16), poolw_ref[g].astype(bf16), preferred_element_type=f32)
        y_ref[:, cols] = (yp * pscale_ref[:, cols]).astype(bf16)

    ce = cext[...]
    cw = convw_ref[...]
    acc = (ce * cw[3:4, :] + pltpu.roll(ce, 1, axis=0) * cw[2:3, :]
           + pltpu.roll(ce, 2, axis=0) * cw[1:2, :] + pltpu.roll(ce, 3, axis=0) * cw[0:1, :])
    acc = acc[CONV_HALO:, :] + convb_ref[...]
    xbc = acc * jax.nn.sigmoid(acc)
    xs = xbc[:, 0:D_SSM]
    bm = xbc[:, D_SSM:D_SSM + N_GROUPS * N_STATE]
    cm = xbc[:, D_SSM + N_GROUPS * N_STATE:]

    uext[0:POOL_HALO, :] = uext[tm:tm + POOL_HALO, :]
    cext[0:CONV_HALO, :] = cext[tm:tm + CONV_HALO, :]

    v = dt_ref[...] + dtb_ref[...]
    dt = jnp.maximum(v, 0.0) + jnp.log1p(jnp.exp(-jnp.abs(v)))
    row = lax.broadcasted_iota(i32, (tm, N_HEADS), 0)
    if lead:
        dt = jnp.where(row >= lead, dt, 0.0)
    a = -jnp.exp(alog_ref[...])
    acs = dt * a
    rin = row % CHUNK
    sh = 1
    while sh < CHUNK:
        acs = acs + jnp.where(rin >= sh, pltpu.roll(acs, sh, axis=0), 0.0)
        sh *= 2
    alast = jnp.concatenate(
        [jnp.broadcast_to(acs[c * CHUNK + CHUNK - 1:c * CHUNK + CHUNK, :], (CHUNK, N_HEADS))
         for c in range(nch)], axis=0)
    dec_st = jnp.exp(alast - acs)
    dec_in = jnp.exp(acs)

    expand = expand_ref[...]
    dt_x = _split_dot(dt, expand)
    dtds_x = _split_dot(dt * dec_st, expand)
    decin_x = _split_dot(dec_in, expand)

    eye_h = (lax.broadcasted_iota(i32, (N_HEADS, N_HEADS), 0)
             == lax.broadcasted_iota(i32, (N_HEADS, N_HEADS), 1)).astype(bf16)
    p1 = acs.astype(bf16)
    r1 = acs - p1.astype(f32)
    p2 = r1.astype(bf16)
    p3 = (r1 - p2.astype(f32)).astype(bf16)
    acs_t = _nt_dot(eye_h, p1) + _nt_dot(eye_h, p2) + _nt_dot(eye_h, p3)

    xdt = (xs * dt_x).astype(bf16)
    xds = (xs * dtds_x).astype(bf16)

    eye_n = (lax.broadcasted_iota(i32, (N_STATE, N_STATE), 0)
             == lax.broadcasted_iota(i32, (N_STATE, N_STATE), 1)).astype(bf16)
    causal = (lax.broadcasted_iota(i32, (CHUNK, CHUNK), 0)
              >= lax.broadcasted_iota(i32, (CHUNK, CHUNK), 1))
    col_head = lax.broadcasted_iota(i32, (CHUNK, GROUP_COLS), 1) // HEAD_DIM

    y_chunks = []
    for c in range(nch):
        r0 = c * CHUNK
        y_groups = []
        for g in range(N_GROUPS):
            gc = slice(g * GROUP_COLS, (g + 1) * GROUP_COLS)
            cg = cm[r0:r0 + CHUNK, g * N_STATE:(g + 1) * N_STATE].astype(bf16)
            bg = bm[r0:r0 + CHUNK, g * N_STATE:(g + 1) * N_STATE].astype(bf16)
            gmat = _nt_dot(cg, bg)
            st = state[g]
            y_off = (jnp.dot(cg, st.astype(bf16), preferred_element_type=f32)
                     * decin_x[r0:r0 + CHUNK, gc])
            scores = []
            for h in range(HEADS_PER_GROUP):
                hh = g * HEADS_PER_GROUP + h
                a_col = acs[r0:r0 + CHUNK, hh:hh + 1]
                a_row = acs_t[hh:hh + 1, r0:r0 + CHUNK]
                lmat = jnp.exp(jnp.where(causal, a_col - a_row, -jnp.inf))
                scores.append((gmat * lmat).astype(bf16))
            sc = jnp.concatenate(scores, axis=0)
            full = jnp.dot(sc, xdt[r0:r0 + CHUNK, gc], preferred_element_type=f32)
            y_diag = jnp.zeros((CHUNK, GROUP_COLS), f32)
            for h in range(HEADS_PER_GROUP):
                y_diag = jnp.where(col_head == h, full[h * CHUNK:(h + 1) * CHUNK, :], y_diag)
            y_groups.append(y_diag + y_off)
            bg_t = _nt_dot(eye_n, bg).astype(bf16)
            upd = jnp.dot(bg_t, xds[r0:r0 + CHUNK, gc], preferred_element_type=f32)
            state[g] = st * decin_x[r0 + CHUNK - 1:r0 + CHUNK, gc] + upd
        y_chunks.append(jnp.concatenate(y_groups, axis=1))
    y = jnp.concatenate(y_chunks, axis=0) if nch > 1 else y_chunks[0]

    y = y + dskip_ref[...] * xs
    zz = z_ref[...]
    y = y * (zz * jax.nn.sigmoid(zz))
    for g in range(N_GROUPS):
        gc = slice(g * GROUP_COLS, (g + 1) * GROUP_COLS)
        yg = y[:, gc]
        yg = yg * lax.rsqrt(jnp.mean(yg * yg, axis=-1, keepdims=True) + NORM_EPS)
        y_ref[:, D_POOL + g * GROUP_COLS:D_POOL + (g + 1) * GROUP_COLS] = (
            yg * ng_ref[:, gc]).astype(bf16)

    state_out_ref[...] = state[...]
    utail_ref[...] = uext[0:POOL_HALO, :]
    ctail_ref[...] = cext[0:CONV_HALO, :]


def _mixer(proj, dt_raw, params, state0, uhalo0, chalo0, tm, lead):
    m = proj.shape[0]
    (pool_w, pool_scale, conv_w, conv_b, dt_bias, a_log, dskip_x, norm_g, expand) = params
    full = lambda shape: pl.BlockSpec(shape, lambda i: (0,) * len(shape))
    return pl.pallas_call(
        functools.partial(_mixer_kernel, tm=tm, lead=lead),
        grid=(m // tm,),
        in_specs=[
            pl.BlockSpec((tm, D_POOL), lambda i: (i, 0)),
            pl.BlockSpec((tm, D_SSM), lambda i: (i, 1)),
            pl.BlockSpec((tm, D_SSM), lambda i: (i, 2)),
            pl.BlockSpec((tm, 2 * N_GROUPS * N_STATE), lambda i: (i, 6)),
            pl.BlockSpec((tm, N_HEADS), lambda i: (i, 0)),
            full((len(POOL_WINDOWS), POOL_GROUP, POOL_GROUP)),
            full((1, D_POOL)),
            full((4, D_CONV)),
            full((1, D_CONV)),
            full((1, N_HEADS)),
            full((1, N_HEADS)),
            full((1, D_SSM)),
            full((1, D_SSM)),
            full((N_HEADS, D_SSM)),
            full((N_GROUPS, N_STATE, GROUP_COLS)),
            full((POOL_HALO, D_POOL)),
            full((CONV_HALO, D_CONV)),
        ],
        out_specs=[
            pl.BlockSpec((tm, D_MODEL), lambda i: (i, 0)),
            full((N_GROUPS, N_STATE, GROUP_COLS)),
            full((POOL_HALO, D_POOL)),
            full((CONV_HALO, D_CONV)),
        ],
        out_shape=[
            jax.ShapeDtypeStruct((m, D_MODEL), bf16),
            jax.ShapeDtypeStruct((N_GROUPS, N_STATE, GROUP_COLS), f32),
            jax.ShapeDtypeStruct((POOL_HALO, D_POOL), f32),
            jax.ShapeDtypeStruct((CONV_HALO, D_CONV), f32),
        ],
        scratch_shapes=[
            pltpu.VMEM((N_GROUPS, N_STATE, GROUP_COLS), f32),
            pltpu.VMEM((tm + POOL_HALO, D_POOL), f32),
            pltpu.VMEM((tm + CONV_HALO, D_CONV), f32),
        ],
        compiler_params=_cparams(("arbitrary",)),
        name="mixer",
    )(proj, proj, proj, proj, dt_raw, pool_w, pool_scale, conv_w, conv_b, dt_bias, a_log,
      dskip_x, norm_g, expand, state0, uhalo0, chalo0)


def _outproj_kernel(y_ref, w_ref, x_ref, g2_ref, rw_ref, rb_ref,
                    h2_ref, hn_ref, pos_ref, gate_ref, cnt_ref,
                    cnt, *, tm, tn):
    i = pl.program_id(0)
    nj = D_MODEL // tn

    @pl.when(i == 0)
    def _():
        cnt[...] = jnp.zeros_like(cnt)

    yv = y_ref[...]
    ss = jnp.zeros((tm, 1), f32)
    for q in range(nj):
        cols = slice(q * tn, (q + 1) * tn)
        h = x_ref[:, cols] + jnp.dot(yv, w_ref[:, cols], preferred_element_type=f32)
        h2_ref[:, cols] = h
        ss = ss + jnp.sum(h * h, axis=-1, keepdims=True)

    def route():
        inv = lax.rsqrt(ss / float(D_MODEL) + NORM_EPS)
        logits = jnp.zeros((tm, N_EXPERTS), f32) + rb_ref[...]
        for q in range(nj):
            cols = slice(q * tn, (q + 1) * tn)
            hn = h2_ref[:, cols] * inv * g2_ref[:, cols]
            hn_ref[:, cols] = hn
            logits = logits + jnp.dot(hn.astype(bf16), rw_ref[cols, :],
                                      preferred_element_type=f32)

        lane = lax.broadcasted_iota(i32, (tm, N_EXPERTS), 1).astype(f32)
        vals = logits
        picked = jnp.zeros((tm, N_EXPERTS), f32)
        tops, idxs = [], []
        for _ in range(TOP_K):
            mx = jnp.max(vals, axis=-1, keepdims=True)
            idx = jnp.min(jnp.where(vals == mx, lane, float(N_EXPERTS)), axis=-1, keepdims=True)
            hit = lane == idx
            tops.append(mx)
            idxs.append(idx)
            picked = picked + hit.astype(f32)
            vals = jnp.where(hit, -jnp.inf, vals)
        exps = [jnp.exp(t - tops[0]) for t in tops]
        denom = exps[0] + exps[1] + exps[2] + exps[3]

        tri = (lax.broadcasted_iota(i32, (tm, tm), 0)
               > lax.broadcasted_iota(i32, (tm, tm), 1)).astype(bf16)
        cum = jnp.dot(tri, picked.astype(bf16), preferred_element_type=f32) + cnt[...]
        lane4 = lax.broadcasted_iota(i32, (tm, TOP_K), 1)
        pos = jnp.zeros((tm, TOP_K), i32)
        gates = jnp.zeros((tm, TOP_K), f32)
        for k in range(TOP_K):
            rank = jnp.sum(jnp.where(lane == idxs[k], cum, 0.0), axis=-1, keepdims=True)
            p = idxs[k].astype(i32) * EXPERT_CAP + rank.astype(i32)
            pos = jnp.where(lane4 == k, p, pos)
            gates = jnp.where(lane4 == k, exps[k] / denom, gates)
        pos_ref[...] = pos
        gate_ref[...] = gates
        cnt[...] = cnt[...] + jnp.sum(picked, axis=0, keepdims=True)
        cnt_ref[...] = cnt[...]

    route()


def _outproj(ymix, w_out, x2d, g2, rw, rb, tm=512, tn=512):
    m = x2d.shape[0]
    resident = dict(pipeline_mode=pl.Buffered(1))
    return pl.pallas_call(
        functools.partial(_outproj_kernel, tm=tm, tn=tn),
        grid=(m // tm,),
        in_specs=[
            pl.BlockSpec((tm, D_MODEL), lambda i: (i, 0)),
            pl.BlockSpec((D_MODEL, D_MODEL), lambda i: (0, 0), **resident),
            pl.BlockSpec((tm, D_MODEL), lambda i: (i, 0)),
            pl.BlockSpec((1, D_MODEL), lambda i: (0, 0)),
            pl.BlockSpec((D_MODEL, N_EXPERTS), lambda i: (0, 0), **resident),
            pl.BlockSpec((1, N_EXPERTS), lambda i: (0, 0)),
        ],
        out_specs=[
            pl.BlockSpec((tm, D_MODEL), lambda i: (i, 0)),
            pl.BlockSpec((tm, D_MODEL), lambda i: (i, 0)),
            pl.BlockSpec((tm, TOP_K), lambda i: (i, 0)),
            pl.BlockSpec((tm, TOP_K), lambda i: (i, 0)),
            pl.BlockSpec((1, N_EXPERTS), lambda i: (0, 0)),
        ],
        out_shape=[
            jax.ShapeDtypeStruct((m, D_MODEL), f32),
            jax.ShapeDtypeStruct((m, D_MODEL), f32),
            jax.ShapeDtypeStruct((m, TOP_K), i32),
            jax.ShapeDtypeStruct((m, TOP_K), f32),
            jax.ShapeDtypeStruct((1, N_EXPERTS), f32),
        ],
        scratch_shapes=[pltpu.VMEM((1, N_EXPERTS), f32)],
        compiler_params=_cparams(("arbitrary",)),
        name="outproj_router",
    )(ymix, w_out, x2d, g2, rw, rb)


def _scatter_kernel(pos_ref, cnt_ref, hn_ref, xs_ref, zbuf, sem, zsem, *, tb):
    i = pl.program_id(0)

    @pl.when(i == 0)
    def _():
        zbuf[...] = jnp.zeros_like(zbuf)
        for e in range(N_EXPERTS):
            start = pl.multiple_of(e * EXPERT_CAP + (cnt_ref[e] // 8) * 8, 8)
            pltpu.make_async_copy(zbuf, xs_ref.at[pl.ds(start, ZERO_ROWS)], zsem).start()
        for e in range(N_EXPERTS):
            start = pl.multiple_of(e * EXPERT_CAP + (cnt_ref[e] // 8) * 8, 8)
            pltpu.make_async_copy(zbuf, xs_ref.at[pl.ds(start, ZERO_ROWS)], zsem).wait()

    def body(t, carry):
        for k in range(TOP_K):
            p = pos_ref[(i * tb + t) * TOP_K + k]
            pltpu.make_async_copy(hn_ref.at[pl.ds(t, 1)], xs_ref.at[pl.ds(p, 1)],
                                  sem).start(priority=k % 2)
        return carry

    lax.fori_loop(0, tb, body, 0, unroll=8)
    done = xs_ref.at[pl.ds(0, tb * TOP_K)]
    pltpu.make_async_copy(done, done, sem).wait()


def _scatter(pos_flat, cnt_i, hn2, tb=512):
    m = hn2.shape[0]
    return pl.pallas_call(
        functools.partial(_scatter_kernel, tb=tb),
        grid_spec=pltpu.PrefetchScalarGridSpec(
            num_scalar_prefetch=2,
            grid=(m // tb,),
            in_specs=[pl.BlockSpec((tb, D_MODEL), lambda i, pos, cnt: (i, 0))],
            out_specs=pl.BlockSpec(memory_space=pl.ANY),
            scratch_shapes=[
                pltpu.VMEM((ZERO_ROWS, D_MODEL), f32),
                pltpu.SemaphoreType.DMA(()),
                pltpu.SemaphoreType.DMA(()),
            ],
        ),
        out_shape=jax.ShapeDtypeStruct((N_EXPERTS * EXPERT_CAP, D_MODEL), f32),
        compiler_params=_cparams(("arbitrary",)),
        name="scatter_rows",
    )(pos_flat, cnt_i, hn2)


def _moe_kernel(we_ref, wq_ref, wn_ref, tot_ref,
                xs_ref, wg_hbm, wu_hbm, wd_hbm, bg_ref, bu_ref, bd_ref,
                o_ref, stage, xb, abuf, ws_a, ws_b, ring, sem, wsem):
    w = pl.program_id(0)
    j = pl.program_id(1)
    n = wn_ref[w]
    wnext = jnp.minimum(w + 1, MOE_WORK - 1)
    n_next = jnp.where(w + 1 < MOE_WORK, wn_ref[wnext], 0)

    def tile_copy(item, r):
        e = we_ref[item]
        if r < 2 * MOE_NJ:
            src, blk = (wg_hbm if r % 2 == 0 else wu_hbm), r // 2
        else:
            src, blk = wd_hbm, r - 2 * MOE_NJ
        return pltpu.make_async_copy(src.at[e, :, pl.ds(blk * MOE_TN, MOE_TN)],
                                     ring.at[r % MOE_RING], wsem.at[r % MOE_RING])

    def tile_take(r, dst):
        tile_copy(w, r).wait()
        dst[...] = ring[r % MOE_RING].astype(bf16)
        r2 = r + MOE_RING
        if r2 < MOE_TILES:
            tile_copy(w, r2).start()
        else:
            @pl.when(w + 1 < tot_ref[0])
            def _():
                tile_copy(wnext, r2 - MOE_TILES).start()

    @pl.when((w == 0) & (j == 0))
    def _():
        for r in range(MOE_RING):
            tile_copy(w, r).start()

    def x_copy(item, s):
        base = we_ref[item] * EXPERT_CAP + wq_ref[item] * MOE_ROWS
        return pltpu.make_async_copy(
            xs_ref.at[pl.ds(pl.multiple_of(base + s * MOE_SUB, MOE_SUB), MOE_SUB)],
            stage.at[s % 2], sem.at[s % 2])

    def x_start(item, count, s):
        @pl.when(s < count)
        def _():
            x_copy(item, s).start()

    def x_finish(item, count, s):
        @pl.when(s < count)
        def _():
            x_copy(item, s).wait()
            xb[pl.ds(s * MOE_SUB, MOE_SUB), :] = stage[s % 2].astype(bf16)

    @pl.when((w == 0) & (j == 0))
    def _():
        x_start(w, n, 0)
        x_start(w, n, 1)
        for s in range(MOE_SUBS):
            x_finish(w, n, s)
            if s + 2 < MOE_SUBS:
                x_start(w, n, s + 2)

    def per_subtile(fn):
        def pair(p, carry):
            r0 = p * (2 * MOE_SUB)
            fn(r0)
            fn(r0 + MOE_SUB)
            return carry

        lax.fori_loop(0, n // 2, pair, 0)

        @pl.when(n % 2 == 1)
        def _():
            fn((n - 1) * MOE_SUB)

    for jj in range(2 * MOE_NJ):
        @pl.when((j == jj) & (n > 0))
        def _():
            if jj < MOE_NJ:
                tile_take(2 * jj, ws_a)
                tile_take(2 * jj + 1, ws_b)
            else:
                tile_take(MOE_NJ + jj, ws_a)

    @pl.when((j < MOE_NJ) & (n > 0))
    def _():
        bgv = bg_ref[0]
        buv = bu_ref[0]

        def gate_up(r0):
            rows = pl.ds(pl.multiple_of(r0, MOE_SUB), MOE_SUB)
            x = xb[rows, :]
            g = jnp.dot(x, ws_a[...], preferred_element_type=f32) + bgv
            u = jnp.dot(x, ws_b[...], preferred_element_type=f32) + buv
            g = jnp.minimum(g, SWIGLU_LIMIT)
            u = jnp.clip(u, -SWIGLU_LIMIT, SWIGLU_LIMIT)
            act = (u + 1.0) * (g * jax.nn.sigmoid(SWIGLU_ALPHA * g))
            abuf[j, rows, :] = act.astype(bf16)

        per_subtile(gate_up)

    @pl.when((j >= MOE_NJ) & (n > 0))
    def _():
        bdv = bd_ref[0]

        def down(r0):
            rows = pl.ds(pl.multiple_of(r0, MOE_SUB), MOE_SUB)
            acc = jnp.zeros((MOE_SUB, MOE_TN), f32) + bdv
            for q in range(MOE_NJ):
                acc = acc + jnp.dot(abuf[q, rows, :], ws_a[q * MOE_TN:(q + 1) * MOE_TN, :],
                                    preferred_element_type=f32)
            o_ref[rows, :] = acc

        per_subtile(down)

    assert 2 * (MOE_NJ - 1) >= MOE_SUBS
    for k in range(MOE_NJ):
        @pl.when(j == MOE_NJ + k)
        def _():
            if k == 0:
                x_start(wnext, n_next, 0)
                x_start(wnext, n_next, 1)
            else:
                for s in (2 * k - 2, 2 * k - 1):
                    if s < MOE_SUBS:
                        x_finish(wnext, n_next, s)
                        if s + 2 < MOE_SUBS:
                            x_start(wnext, n_next, s + 2)


def _moe(we, wq, wn, tot, xs, w_gate, w_up, w_down, b_gate, b_up, b_down):
    blocks_per_expert = EXPERT_CAP // MOE_ROWS

    def b_a(w, j, we, wq, wn, tot):
        return (we[w], 0, jnp.minimum(j, MOE_NJ - 1))

    def b_b(w, j, we, wq, wn, tot):
        return (we[w], 0, jnp.maximum(j - MOE_NJ, 0))

    def o_map(w, j, we, wq, wn, tot):
        jj = jnp.where(wn[w] > 0, jnp.maximum(j - MOE_NJ, 0), MOE_NJ - 1)
        return (we[w] * blocks_per_expert + wq[w], jj)

    any_space = pl.BlockSpec(memory_space=pl.ANY)
    return pl.pallas_call(
        _moe_kernel,
        grid_spec=pltpu.PrefetchScalarGridSpec(
            num_scalar_prefetch=4,
            grid=(MOE_WORK, 2 * MOE_NJ),
            in_specs=[
                any_space, any_space, any_space, any_space,
                pl.BlockSpec((1, 1, MOE_TN), b_a),
                pl.BlockSpec((1, 1, MOE_TN), b_a),
                pl.BlockSpec((1, 1, MOE_TN), b_b),
            ],
            out_specs=pl.BlockSpec((MOE_ROWS, MOE_TN), o_map),
            scratch_shapes=[
                pltpu.VMEM((2, MOE_SUB, D_MODEL), f32),
                pltpu.VMEM((MOE_ROWS, D_MODEL), bf16),
                pltpu.VMEM((MOE_NJ, MOE_ROWS, MOE_TN), bf16),
                pltpu.VMEM((D_MODEL, MOE_TN), bf16),
                pltpu.VMEM((D_MODEL, MOE_TN), bf16),
                pltpu.VMEM((MOE_RING, D_MODEL, MOE_TN), f32),
                pltpu.SemaphoreType.DMA((2,)),
                pltpu.SemaphoreType.DMA((MOE_RING,)),
            ],
        ),
        out_shape=jax.ShapeDtypeStruct((N_EXPERTS * EXPERT_CAP, D_MODEL), f32),
        compiler_params=_cparams(("arbitrary", "arbitrary")),
        name="moe_experts",
    )(we, wq, wn, tot, xs, w_gate, w_up, w_down,
      b_gate.reshape(N_EXPERTS, 1, D_EXPERT), b_up.reshape(N_EXPERTS, 1, D_EXPERT),
      b_down.reshape(N_EXPERTS, 1, D_MODEL))


def _work_list(cnt_i):
    nq = (cnt_i + MOE_ROWS - 1) // MOE_ROWS
    ends = jnp.cumsum(nq)
    total = ends[-1]
    w = jnp.arange(MOE_WORK, dtype=i32)
    wc = jnp.minimum(w, total - 1)
    e = jnp.minimum(jnp.sum((ends[None, :] <= wc[:, None]).astype(i32), axis=1), N_EXPERTS - 1)
    q = wc - (ends[e] - nq[e])
    rows = jnp.clip(cnt_i[e] - q * MOE_ROWS, 0, MOE_ROWS)
    n = jnp.where(w < total, (rows + MOE_SUB - 1) // MOE_SUB, 0)
    return e.astype(i32), q.astype(i32), n.astype(i32), total.astype(i32).reshape(1)


def _combine_kernel(pos_ref, o_ref, gate_ref, h2_ref, fg_ref, out_ref, obuf, sem, *, tm):
    i = pl.program_id(0)

    def body(t, carry):
        tok = i * tm + t
        for k in range(TOP_K):
            p = pos_ref[tok * TOP_K + k]
            pltpu.make_async_copy(o_ref.at[pl.ds(p, 1)], obuf.at[pl.ds(k * tm + t, 1)],
                                  sem).start(priority=k % 2)
        return carry

    lax.fori_loop(0, tm, body, 0, unroll=8)
    pltpu.make_async_copy(obuf, obuf, sem).wait()

    gates = gate_ref[...]
    y = h2_ref[...]
    for k in range(TOP_K):
        y = y + gates[:, k:k + 1] * obuf[k * tm:(k + 1) * tm, :]
    out = y * lax.rsqrt(jnp.mean(y * y, axis=-1, keepdims=True) + NORM_EPS)
    out_ref[...] = out * fg_ref[...]


def _combine(pos_flat, o, gates, h2, final_g, tm=256):
    m = h2.shape[0]
    return pl.pallas_call(
        functools.partial(_combine_kernel, tm=tm),
        grid_spec=pltpu.PrefetchScalarGridSpec(
            num_scalar_prefetch=1,
            grid=(m // tm,),
            in_specs=[
                pl.BlockSpec(memory_space=pl.ANY),
                pl.BlockSpec((tm, TOP_K), lambda i, pos: (i, 0)),
                pl.BlockSpec((tm, D_MODEL), lambda i, pos: (i, 0)),
                pl.BlockSpec((1, D_MODEL), lambda i, pos: (0, 0)),
            ],
            out_specs=pl.BlockSpec((tm, D_MODEL), lambda i, pos: (i, 0)),
            scratch_shapes=[
                pltpu.VMEM((TOP_K * tm, D_MODEL), f32),
                pltpu.SemaphoreType.DMA(()),
            ],
        ),
        out_shape=jax.ShapeDtypeStruct((m, D_MODEL), f32),
        compiler_params=_cparams(("arbitrary",)),
        name="combine",
    )(pos_flat, o, gates, h2, final_g)


def kernel(x, meta_tokens, norm1_g, w_in, pool_w, pool_scale, conv_w, conv_b, dt_bias, a_log,
           d_skip, ssm_norm_g, w_out, norm2_g, router_w, router_b, w_gate, b_gate, w_up, b_up,
           w_down, b_down, final_g):
    assert x.shape == (1, SEQ, D_MODEL)
    x2d = x.reshape(SEQ, D_MODEL)
    lead = CHUNK - N_META
    meta_chunk = jnp.concatenate([jnp.zeros((lead, D_MODEL), f32), meta_tokens.astype(f32)], 0)

    g1 = norm1_g[0].reshape(1, D_MODEL)
    w_main = w_in[0, :, :D_MAIN].astype(bf16)
    w_dt = w_in[0, :, D_MAIN:].astype(bf16)
    head_of_col = jnp.arange(D_SSM, dtype=i32) // HEAD_DIM
    expand = (head_of_col[None, :] == jnp.arange(N_HEADS, dtype=i32)[:, None]).astype(bf16)
    mix_params = (
        pool_w[0], pool_scale[0].reshape(1, D_POOL), conv_w[0], conv_b[0].reshape(1, D_CONV),
        dt_bias[0].reshape(1, N_HEADS), a_log[0].reshape(1, N_HEADS),
        jnp.repeat(d_skip[0], HEAD_DIM).reshape(1, D_SSM), ssm_norm_g[0].reshape(1, D_SSM),
        expand)

    proj_m, dt_m = _inproj(meta_chunk, g1, w_main, w_dt, tm=CHUNK)
    zeros_state = jnp.zeros((N_GROUPS, N_STATE, GROUP_COLS), f32)
    _, state_m, utail_m, ctail_m = _mixer(
        proj_m, dt_m, mix_params, zeros_state, jnp.zeros((POOL_HALO, D_POOL), f32),
        jnp.zeros((CONV_HALO, D_CONV), f32), tm=CHUNK, lead=lead)

    proj, dt_raw = _inproj(x2d, g1, w_main, w_dt, tm=512)
    ymix, _, _, _ = _mixer(proj, dt_raw, mix_params, state_m, utail_m, ctail_m, tm=256, lead=0)

    h2, hn2, pos, gates, cnt = _outproj(
        ymix, w_out[0].astype(bf16), x2d, norm2_g[0].reshape(1, D_MODEL),
        router_w[0].astype(bf16),
        router_b[0].reshape(1, N_EXPERTS))

    pos_flat = pos.reshape(SEQ * TOP_K)
    cnt_i = cnt.reshape(N_EXPERTS).astype(i32)
    xs = _scatter(pos_flat, cnt_i, hn2)
    we, wq, wn, tot = _work_list(cnt_i)
    o = _moe(we, wq, wn, tot, xs, w_gate[0], w_up[0], w_down[0], b_gate[0], b_up[0], b_down[0])
    out = _combine(pos_flat, o, gates, h2, final_g.reshape(1, D_MODEL))
    return out.reshape(1, SEQ, D_MODEL)
```

```python
import functools

import jax
import jax.numpy as jnp
from jax import lax
from jax.experimental import pallas as pl
from jax.experimental.pallas import tpu as pltpu

f32 = jnp.float32
bf16 = jnp.bfloat16
i32 = jnp.int32

D_MODEL = 2048
SEQ = 8192
CHUNK = 64
N_META = 16
D_POOL = 1024
POOL_WINDOWS = (2, 4, 8, 16)
POOL_GROUP = 256
D_SSM = 1024
HEAD_DIM = 64
N_HEADS = 16
N_GROUPS = 2
HEADS_PER_GROUP = N_HEADS // N_GROUPS
N_STATE = 128
GROUP_COLS = HEADS_PER_GROUP * HEAD_DIM
D_CONV = D_SSM + 2 * N_GROUPS * N_STATE
D_MAIN = D_POOL + D_SSM + D_CONV
N_EXPERTS = 32
TOP_K = 4
D_EXPERT = 2048
SWIGLU_LIMIT = 7.0
SWIGLU_ALPHA = 1.702
NORM_EPS = 1e-5

POOL_HALO = 16
CONV_HALO = 8

MOE_SUB = 256
MOE_UNIT = 128
MOE_ROWS = 1280
MOE_SUBS = MOE_ROWS // MOE_SUB
ZERO_ROWS = MOE_SUB + 8
EXPERT_CAP = -(-(SEQ + ZERO_ROWS) // MOE_ROWS) * MOE_ROWS
MOE_TN = 512
MOE_NJ = D_EXPERT // MOE_TN
MOE_WORK = (SEQ * TOP_K) // MOE_ROWS + N_EXPERTS
MOE_TILES = 3 * MOE_NJ
MOE_RING = 6
assert MOE_TILES % MOE_RING == 0

VMEM_LIMIT = 56 * 1024 * 1024


def _cparams(sem):
    return pltpu.CompilerParams(dimension_semantics=sem, vmem_limit_bytes=VMEM_LIMIT)


PROJ_TN = 512


def _inproj_kernel(x_ref, g_ref, w_ref, wdt_ref, proj_ref, dt_ref):
    x = x_ref[...]
    y = x * lax.rsqrt(jnp.mean(x * x, axis=-1, keepdims=True) + NORM_EPS)
    hn = (y * g_ref[...]).astype(bf16)
    dt_ref[...] = jnp.dot(hn, wdt_ref[...], preferred_element_type=f32)
    for c in range(D_MAIN // PROJ_TN):
        cols = slice(c * PROJ_TN, (c + 1) * PROJ_TN)
        proj_ref[:, cols] = jnp.dot(hn, w_ref[:, cols], preferred_element_type=f32)


def _inproj(x2d, g, w_main, w_dt, tm):
    m = x2d.shape[0]
    resident = dict(pipeline_mode=pl.Buffered(1))
    return pl.pallas_call(
        _inproj_kernel,
        grid=(m // tm,),
        in_specs=[
            pl.BlockSpec((tm, D_MODEL), lambda i: (i, 0)),
            pl.BlockSpec((1, D_MODEL), lambda i: (0, 0)),
            pl.BlockSpec((D_MODEL, D_MAIN), lambda i: (0, 0), **resident),
            pl.BlockSpec((D_MODEL, N_HEADS), lambda i: (0, 0), **resident),
        ],
        out_specs=[
            pl.BlockSpec((tm, D_MAIN), lambda i: (i, 0)),
            pl.BlockSpec((tm, N_HEADS), lambda i: (i, 0)),
        ],
        out_shape=[
            jax.ShapeDtypeStruct((m, D_MAIN), f32),
            jax.ShapeDtypeStruct((m, N_HEADS), f32),
        ],
        compiler_params=_cparams(("arbitrary",)),
        name="inproj",
    )(x2d, g, w_main, w_dt)


def _split_dot(q, e):
    hi = q.astype(bf16)
    lo = (q - hi.astype(f32)).astype(bf16)
    return (jnp.dot(hi, e, preferred_element_type=f32)
            + jnp.dot(lo, e, preferred_element_type=f32))


def _nt_dot(a, b):
    return lax.dot_general(a, b, (((1,), (1,)), ((), ())), preferred_element_type=f32)


def _mixer_kernel(u_ref, z_ref, xs_ref, bc_ref, dt_ref,
                  poolw_ref, pscale_ref, convw_ref, convb_ref, dtb_ref, alog_ref,
                  dskip_ref, ng_ref, expand_ref, state0_ref, uhalo0_ref, chalo0_ref,
                  y_ref, state_out_ref, utail_ref, ctail_ref,
                  state, uext, cext, *, tm, lead):
    i = pl.program_id(0)
    nch = tm // CHUNK

    @pl.when(i == 0)
    def _():
        state[...] = state0_ref[...]
        uext[0:POOL_HALO, :] = uhalo0_ref[...]
        cext[0:CONV_HALO, :] = chalo0_ref[...]

    uext[POOL_HALO:, :] = u_ref[...]
    cext[CONV_HALO:, 0:D_SSM] = xs_ref[...]
    cext[CONV_HALO:, D_SSM:] = bc_ref[...]

    for g, w in enumerate(POOL_WINDOWS):
        cols = slice(g * POOL_GROUP, (g + 1) * POOL_GROUP)
        ue = uext[:, cols]
        s = ue
        sh = 1
        while sh < w:
            s = s + pltpu.roll(s, sh, axis=0)
            sh *= 2
        d = s[POOL_HALO:, :] / float(w) - ue[POOL_HALO:, :]
        yp = jnp.dot(d.astype(bf16), poolw_ref[g].astype(bf16), preferred_element_type=f32)
        y_ref[:, cols] = (yp * pscale_ref[:, cols]).astype(bf16)

    ce = cext[...]
    cw = convw_ref[...]
    acc = (ce * cw[3:4, :] + pltpu.roll(ce, 1, axis=0) * cw[2:3, :]
           + pltpu.roll(ce, 2, axis=0) * cw[1:2, :] + pltpu.roll(ce, 3, axis=0) * cw[0:1, :])
    acc = acc[CONV_HALO:, :] + convb_ref[...]
    xbc = acc * jax.nn.sigmoid(acc)
    xs = xbc[:, 0:D_SSM]
    bm = xbc[:, D_SSM:D_SSM + N_GROUPS * N_STATE]
    cm = xbc[:, D_SSM + N_GROUPS * N_STATE:]

    uext[0:POOL_HALO, :] = uext[tm:tm + POOL_HALO, :]
    cext[0:CONV_HALO, :] = cext[tm:tm + CONV_HALO, :]

    v = dt_ref[...] + dtb_ref[...]
    dt = jnp.maximum(v, 0.0) + jnp.log1p(jnp.exp(-jnp.abs(v)))
    row = lax.broadcasted_iota(i32, (tm, N_HEADS), 0)
    if lead:
        dt = jnp.where(row >= lead, dt, 0.0)
    a = -jnp.exp(alog_ref[...])
    acs = dt * a
    rin = row % CHUNK
    sh = 1
    while sh < CHUNK:
        acs = acs + jnp.where(rin >= sh, pltpu.roll(acs, sh, axis=0), 0.0)
        sh *= 2
    alast = jnp.concatenate(
        [jnp.broadcast_to(acs[c * CHUNK + CHUNK - 1:c * CHUNK + CHUNK, :], (CHUNK, N_HEADS))
         for c in range(nch)], axis=0)
    dec_st = jnp.exp(alast - acs)
    dec_in = jnp.exp(acs)

    expand = expand_ref[...]
    dt_x = _split_dot(dt, expand)
    dtds_x = _split_dot(dt * dec_st, expand)
    decin_x = _split_dot(dec_in, expand)

    eye_h = (lax.broadcasted_iota(i32, (N_HEADS, N_HEADS), 0)
             == lax.broadcasted_iota(i32, (N_HEADS, N_HEADS), 1)).astype(bf16)
    p1 = acs.astype(bf16)
    r1 = acs - p1.astype(f32)
    p2 = r1.astype(bf16)
    p3 = (r1 - p2.astype(f32)).astype(bf16)
    acs_t = _nt_dot(eye_h, p1) + _nt_dot(eye_h, p2) + _nt_dot(eye_h, p3)

    xdt = (xs * dt_x).astype(bf16)
    xds = (xs * dtds_x).astype(bf16)

    eye_n = (lax.broadcasted_iota(i32, (N_STATE, N_STATE), 0)
             == lax.broadcasted_iota(i32, (N_STATE, N_STATE), 1)).astype(bf16)
    causal = (lax.broadcasted_iota(i32, (CHUNK, CHUNK), 0)
              >= lax.broadcasted_iota(i32, (CHUNK, CHUNK), 1))
    col_head = lax.broadcasted_iota(i32, (CHUNK, GROUP_COLS), 1) // HEAD_DIM

    y_chunks = []
    for c in range(nch):
        r0 = c * CHUNK
        y_groups = []
        for g in range(N_GROUPS):
            gc = slice(g * GROUP_COLS, (g + 1) * GROUP_COLS)
            cg = cm[r0:r0 + CHUNK, g * N_STATE:(g + 1) * N_STATE].astype(bf16)
            bg = bm[r0:r0 + CHUNK, g * N_STATE:(g + 1) * N_STATE].astype(bf16)
            gmat = _nt_dot(cg, bg)
            st = state[g]
            y_off = (jnp.dot(cg, st.astype(bf16), preferred_element_type=f32)
                     * decin_x[r0:r0 + CHUNK, gc])
            scores = []
            for h in range(HEADS_PER_GROUP):
                hh = g * HEADS_PER_GROUP + h
                a_col = acs[r0:r0 + CHUNK, hh:hh + 1]
                a_row = acs_t[hh:hh + 1, r0:r0 + CHUNK]
                lmat = jnp.exp(jnp.where(causal, a_col - a_row, -jnp.inf))
                scores.append((gmat * lmat).astype(bf16))
            sc = jnp.concatenate(scores, axis=0)
            full = jnp.dot(sc, xdt[r0:r0 + CHUNK, gc], preferred_element_type=f32)
            y_diag = jnp.zeros((CHUNK, GROUP_COLS), f32)
            for h in range(HEADS_PER_GROUP):
                y_diag = jnp.where(col_head == h, full[h * CHUNK:(h + 1) * CHUNK, :], y_diag)
            y_groups.append(y_diag + y_off)
            bg_t = _nt_dot(eye_n, bg).astype(bf16)
            upd = jnp.dot(bg_t, xds[r0:r0 + CHUNK, gc], preferred_element_type=f32)
            state[g] = st * decin_x[r0 + CHUNK - 1:r0 + CHUNK, gc] + upd
        y_chunks.append(jnp.concatenate(y_groups, axis=1))
    y = jnp.concatenate(y_chunks, axis=0) if nch > 1 else y_chunks[0]

    y = y + dskip_ref[...] * xs
    zz = z_ref[...]
    y = y * (zz * jax.nn.sigmoid(zz))
    for g in range(N_GROUPS):
        gc = slice(g * GROUP_COLS, (g + 1) * GROUP_COLS)
        yg = y[:, gc]
        yg = yg * lax.rsqrt(jnp.mean(yg * yg, axis=-1, keepdims=True) + NORM_EPS)
        y_ref[:, D_POOL + g * GROUP_COLS:D_POOL + (g + 1) * GROUP_COLS] = (
            yg * ng_ref[:, gc]).astype(bf16)

    state_out_ref[...] = state[...]
    utail_ref[...] = uext[0:POOL_HALO, :]
    ctail_ref[...] = cext[0:CONV_HALO, :]


def _mixer(proj, dt_raw, params, state0, uhalo0, chalo0, tm, lead):
    m = proj.shape[0]
    (pool_w, pool_scale, conv_w, conv_b, dt_bias, a_log, dskip_x, norm_g, expand) = params
    full = lambda shape: pl.BlockSpec(shape, lambda i: (0,) * len(shape))
    return pl.pallas_call(
        functools.partial(_mixer_kernel, tm=tm, lead=lead),
        grid=(m // tm,),
        in_specs=[
            pl.BlockSpec((tm, D_POOL), lambda i: (i, 0)),
            pl.BlockSpec((tm, D_SSM), lambda i: (i, 1)),
            pl.BlockSpec((tm, D_SSM), lambda i: (i, 2)),
            pl.BlockSpec((tm, 2 * N_GROUPS * N_STATE), lambda i: (i, 6)),
            pl.BlockSpec((tm, N_HEADS), lambda i: (i, 0)),
            full((len(POOL_WINDOWS), POOL_GROUP, POOL_GROUP)),
            full((1, D_POOL)),
            full((4, D_CONV)),
            full((1, D_CONV)),
            full((1, N_HEADS)),
            full((1, N_HEADS)),
            full((1, D_SSM)),
            full((1, D_SSM)),
            full((N_HEADS, D_SSM)),
            full((N_GROUPS, N_STATE, GROUP_COLS)),
            full((POOL_HALO, D_POOL)),
            full((CONV_HALO, D_CONV)),
        ],
        out_specs=[
            pl.BlockSpec((tm, D_MODEL), lambda i: (i, 0)),
            full((N_GROUPS, N_STATE, GROUP_COLS)),
            full((POOL_HALO, D_POOL)),
            full((CONV_HALO, D_CONV)),
        ],
        out_shape=[
            jax.ShapeDtypeStruct((m, D_MODEL), bf16),
            jax.ShapeDtypeStruct((N_GROUPS, N_STATE, GROUP_COLS), f32),
            jax.ShapeDtypeStruct((POOL_HALO, D_POOL), f32),
            jax.ShapeDtypeStruct((CONV_HALO, D_CONV), f32),
        ],
        scratch_shapes=[
            pltpu.VMEM((N_GROUPS, N_STATE, GROUP_COLS), f32),
            pltpu.VMEM((tm + POOL_HALO, D_POOL), f32),
            pltpu.VMEM((tm + CONV_HALO, D_CONV), f32),
        ],
        compiler_params=_cparams(("arbitrary",)),
        name="mixer",
    )(proj, proj, proj, proj, dt_raw, pool_w, pool_scale, conv_w, conv_b, dt_bias, a_log,
      dskip_x, norm_g, expand, state0, uhalo0, chalo0)


def _outproj_kernel(y_ref, w_ref, x_ref, g2_ref, rw_ref, rb_ref,
                    h2_ref, hn_ref, pos_ref, gate_ref, cnt_ref,
                    cnt, *, tm, tn):
    i = pl.program_id(0)
    nj = D_MODEL // tn

    @pl.when(i == 0)
    def _():
        cnt[...] = jnp.zeros_like(cnt)

    yv = y_ref[...]
    ss = jnp.zeros((tm, 1), f32)
    for q in range(nj):
        cols = slice(q * tn, (q + 1) * tn)
        h = x_ref[:, cols] + jnp.dot(yv, w_ref[:, cols], preferred_element_type=f32)
        h2_ref[:, cols] = h
        ss = ss + jnp.sum(h * h, axis=-1, keepdims=True)

    def route():
        inv = lax.rsqrt(ss / float(D_MODEL) + NORM_EPS)
        logits = jnp.zeros((tm, N_EXPERTS), f32) + rb_ref[...]
        for q in range(nj):
            cols = slice(q * tn, (q + 1) * tn)
            hn = h2_ref[:, cols] * inv * g2_ref[:, cols]
            hn_ref[:, cols] = hn
            logits = logits + jnp.dot(hn.astype(bf16), rw_ref[cols, :],
                                      preferred_element_type=f32)

        lane = lax.broadcasted_iota(i32, (tm, N_EXPERTS), 1).astype(f32)
        vals = logits
        picked = jnp.zeros((tm, N_EXPERTS), f32)
        tops, idxs = [], []
        for _ in range(TOP_K):
            mx = jnp.max(vals, axis=-1, keepdims=True)
            idx = jnp.min(jnp.where(vals == mx, lane, float(N_EXPERTS)), axis=-1, keepdims=True)
            hit = lane == idx
            tops.append(mx)
            idxs.append(idx)
            picked = picked + hit.astype(f32)
            vals = jnp.where(hit, -jnp.inf, vals)
        exps = [jnp.exp(t - tops[0]) for t in tops]
        denom = exps[0] + exps[1] + exps[2] + exps[3]

        tri = (lax.broadcasted_iota(i32, (tm, tm), 0)
               > lax.broadcasted_iota(i32, (tm, tm), 1)).astype(bf16)
        cum = jnp.dot(tri, picked.astype(bf16), preferred_element_type=f32) + cnt[...]
        lane4 = lax.broadcasted_iota(i32, (tm, TOP_K), 1)
        pos = jnp.zeros((tm, TOP_K), i32)
        gates = jnp.zeros((tm, TOP_K), f32)
        for k in range(TOP_K):
            rank = jnp.sum(jnp.where(lane == idxs[k], cum, 0.0), axis=-1, keepdims=True)
            p = idxs[k].astype(i32) * EXPERT_CAP + rank.astype(i32)
            pos = jnp.where(lane4 == k, p, pos)
            gates = jnp.where(lane4 == k, exps[k] / denom, gates)
        pos_ref[...] = pos
        gate_ref[...] = gates
        cnt[...] = cnt[...] + jnp.sum(picked, axis=0, keepdims=True)
        cnt_ref[...] = cnt[...]

    route()


def _outproj(ymix, w_out, x2d, g2, rw, rb, tm=512, tn=512):
    m = x2d.shape[0]
    resident = dict(pipeline_mode=pl.Buffered(1))
    return pl.pallas_call(
        functools.partial(_outproj_kernel, tm=tm, tn=tn),
        grid=(m // tm,),
        in_specs=[
            pl.BlockSpec((tm, D_MODEL), lambda i: (i, 0)),
            pl.BlockSpec((D_MODEL, D_MODEL), lambda i: (0, 0), **resident),
            pl.BlockSpec((tm, D_MODEL), lambda i: (i, 0)),
            pl.BlockSpec((1, D_MODEL), lambda i: (0, 0)),
            pl.BlockSpec((D_MODEL, N_EXPERTS), lambda i: (0, 0), **resident),
            pl.BlockSpec((1, N_EXPERTS), lambda i: (0, 0)),
        ],
        out_specs=[
            pl.BlockSpec((tm, D_MODEL), lambda i: (i, 0)),
            pl.BlockSpec((tm, D_MODEL), lambda i: (i, 0)),
            pl.BlockSpec((tm, TOP_K), lambda i: (i, 0)),
            pl.BlockSpec((tm, TOP_K), lambda i: (i, 0)),
            pl.BlockSpec((1, N_EXPERTS), lambda i: (0, 0)),
        ],
        out_shape=[
            jax.ShapeDtypeStruct((m, D_MODEL), f32),
            jax.ShapeDtypeStruct((m, D_MODEL), f32),
            jax.ShapeDtypeStruct((m, TOP_K), i32),
            jax.ShapeDtypeStruct((m, TOP_K), f32),
            jax.ShapeDtypeStruct((1, N_EXPERTS), f32),
        ],
        scratch_shapes=[pltpu.VMEM((1, N_EXPERTS), f32)],
        compiler_params=_cparams(("arbitrary",)),
        name="outproj_router",
    )(ymix, w_out, x2d, g2, rw, rb)


def _scatter_kernel(pos_ref, cnt_ref, hn_ref, xs_ref, zbuf, sem, zsem, *, tb):
    i = pl.program_id(0)

    @pl.when(i == 0)
    def _():
        zbuf[...] = jnp.zeros_like(zbuf)
        for e in range(N_EXPERTS):
            start = pl.multiple_of(e * EXPERT_CAP + (cnt_ref[e] // 8) * 8, 8)
            pltpu.make_async_copy(zbuf, xs_ref.at[pl.ds(start, ZERO_ROWS)], zsem).start()
        for e in range(N_EXPERTS):
            start = pl.multiple_of(e * EXPERT_CAP + (cnt_ref[e] // 8) * 8, 8)
            pltpu.make_async_copy(zbuf, xs_ref.at[pl.ds(start, ZERO_ROWS)], zsem).wait()

    def body(g, carry):
        t0 = pl.multiple_of(g * 8, 8)
        p0 = pl.multiple_of((i * tb + t0) * TOP_K, 8 * TOP_K)
        for u in range(8):
            for k in range(TOP_K):
                p = pos_ref[p0 + (u * TOP_K + k)]
                pltpu.make_async_copy(hn_ref.at[pl.ds(t0 + u, 1)], xs_ref.at[pl.ds(p, 1)],
                                      sem).start(priority=k % 2)
        return carry

    lax.fori_loop(0, tb // 8, body, 0)
    done = xs_ref.at[pl.ds(0, tb * TOP_K)]
    pltpu.make_async_copy(done, done, sem).wait()


def _scatter(pos_flat, cnt_i, hn2, tb=512):
    m = hn2.shape[0]
    return pl.pallas_call(
        functools.partial(_scatter_kernel, tb=tb),
        grid_spec=pltpu.PrefetchScalarGridSpec(
            num_scalar_prefetch=2,
            grid=(m // tb,),
            in_specs=[pl.BlockSpec((tb, D_MODEL), lambda i, pos, cnt: (i, 0))],
            out_specs=pl.BlockSpec(memory_space=pl.ANY),
            scratch_shapes=[
                pltpu.VMEM((ZERO_ROWS, D_MODEL), f32),
                pltpu.SemaphoreType.DMA(()),
                pltpu.SemaphoreType.DMA(()),
            ],
        ),
        out_shape=jax.ShapeDtypeStruct((N_EXPERTS * EXPERT_CAP, D_MODEL), f32),
        compiler_params=_cparams(("arbitrary",)),
        name="scatter_rows",
    )(pos_flat, cnt_i, hn2)


def _moe_kernel(we_ref, wq_ref, wn_ref, tot_ref,
                xs_ref, wg_hbm, wu_hbm, wd_hbm, bg_ref, bu_ref, bd_ref,
                o_ref, stage, xb, abuf, ws_a, ws_b, ring, sem, wsem):
    w = pl.program_id(0)
    j = pl.program_id(1)
    e_cur = we_ref[w]
    units = wn_ref[w]
    n = (units + 1) // 2
    wnext = jnp.minimum(w + 1, MOE_WORK - 1)
    n_next = jnp.where(w + 1 < MOE_WORK, (wn_ref[wnext] + 1) // 2, 0)

    def tile_copy(item, r):
        e = we_ref[item]
        if r < 2 * MOE_NJ:
            src, blk = (wg_hbm if r % 2 == 0 else wu_hbm), r // 2
        else:
            src, blk = wd_hbm, r - 2 * MOE_NJ
        return pltpu.make_async_copy(src.at[e, :, pl.ds(blk * MOE_TN, MOE_TN)],
                                     ring.at[r % MOE_RING], wsem.at[r % MOE_RING])

    def tile_take(r, dst):
        tile_copy(w, r).wait()
        dst[...] = ring[r % MOE_RING].astype(bf16)
        r2 = r + MOE_RING
        if r2 < MOE_TILES:
            tile_copy(w, r2).start()
        else:
            @pl.when(w + 1 < tot_ref[0])
            def _():
                tile_copy(wnext, r2 - MOE_TILES).start()

    @pl.when((w == 0) & (j == 0))
    def _():
        for r in range(MOE_RING):
            tile_copy(w, r).start()

    def x_copy(item, s):
        base = we_ref[item] * EXPERT_CAP + wq_ref[item] * MOE_ROWS
        return pltpu.make_async_copy(
            xs_ref.at[pl.ds(pl.multiple_of(base + s * MOE_SUB, MOE_SUB), MOE_SUB)],
            stage.at[s % 2], sem.at[s % 2])

    def x_start(item, count, s):
        @pl.when(s < count)
        def _():
            x_copy(item, s).start()

    def x_finish(item, count, s):
        @pl.when(s < count)
        def _():
            x_copy(item, s).wait()
            xb[pl.ds(s * MOE_SUB, MOE_SUB), :] = stage[s % 2].astype(bf16)

    @pl.when((w == 0) & (j == 0))
    def _():
        x_start(w, n, 0)
        x_start(w, n, 1)
        for s in range(MOE_SUBS):
            x_finish(w, n, s)
            if s + 2 < MOE_SUBS:
                x_start(w, n, s + 2)

    def per_subtile(fn):
        def pair(p, carry):
            r0 = p * (2 * MOE_SUB)
            fn(r0, MOE_SUB)
            fn(r0 + MOE_SUB, MOE_SUB)
            return carry

        pairs = units // 4
        rem = units % 4
        lax.fori_loop(0, pairs, pair, 0)

        @pl.when(rem >= 2)
        def _():
            fn(pairs * (2 * MOE_SUB), MOE_SUB)

        @pl.when(rem % 2 == 1)
        def _():
            fn(pairs * (2 * MOE_SUB) + (rem // 2) * MOE_SUB, MOE_UNIT)

    for jj in range(2 * MOE_NJ):
        @pl.when((j == jj) & (n > 0))
        def _():
            if jj < MOE_NJ:
                tile_take(2 * jj, ws_a)
                tile_take(2 * jj + 1, ws_b)
            else:
                tile_take(MOE_NJ + jj, ws_a)

    @pl.when((j < MOE_NJ) & (n > 0))
    def _():
        bgv = bg_ref[e_cur, pl.ds(j, 1), :]
        buv = bu_ref[e_cur, pl.ds(j, 1), :]

        def gate_up(r0, nrows):
            rows = pl.ds(pl.multiple_of(r0, MOE_UNIT), nrows)
            x = xb[rows, :]
            g = jnp.dot(x, ws_a[...], preferred_element_type=f32) + bgv
            u = jnp.dot(x, ws_b[...], preferred_element_type=f32) + buv
            g = jnp.minimum(g, SWIGLU_LIMIT)
            u = jnp.clip(u, -SWIGLU_LIMIT, SWIGLU_LIMIT)
            act = (u + 1.0) * (g * jax.nn.sigmoid(SWIGLU_ALPHA * g))
            abuf[j, rows, :] = act.astype(bf16)

        per_subtile(gate_up)

    @pl.when((j >= MOE_NJ) & (n > 0))
    def _():
        bdv = bd_ref[e_cur, pl.ds(j - MOE_NJ, 1), :]

        def down(r0, nrows):
            rows = pl.ds(pl.multiple_of(r0, MOE_UNIT), nrows)
            acc = jnp.zeros((nrows, MOE_TN), f32) + bdv
            for q in range(MOE_NJ):
                acc = acc + jnp.dot(abuf[q, rows, :], ws_a[q * MOE_TN:(q + 1) * MOE_TN, :],
                                    preferred_element_type=f32)
            o_ref[rows, :] = acc

        per_subtile(down)

    assert 2 * (MOE_NJ - 1) >= MOE_SUBS
    for k in range(MOE_NJ):
        @pl.when(j == MOE_NJ + k)
        def _():
            if k == 0:
                x_start(wnext, n_next, 0)
                x_start(wnext, n_next, 1)
            else:
                for s in (2 * k - 2, 2 * k - 1):
                    if s < MOE_SUBS:
                        x_finish(wnext, n_next, s)
                        if s + 2 < MOE_SUBS:
                            x_start(wnext, n_next, s + 2)


def _moe(we, wq, wn, tot, xs, w_gate, w_up, w_down, b_gate, b_up, b_down):
    blocks_per_expert = EXPERT_CAP // MOE_ROWS

    bias_spec = pl.BlockSpec((N_EXPERTS, MOE_NJ, MOE_TN), lambda w, j, we, wq, wn, tot: (0, 0, 0),
                             pipeline_mode=pl.Buffered(1))

    def o_map(w, j, we, wq, wn, tot):
        jj = jnp.where(wn[w] > 0, jnp.maximum(j - MOE_NJ, 0), MOE_NJ - 1)
        return (we[w] * blocks_per_expert + wq[w], jj)

    any_space = pl.BlockSpec(memory_space=pl.ANY)
    return pl.pallas_call(
        _moe_kernel,
        grid_spec=pltpu.PrefetchScalarGridSpec(
            num_scalar_prefetch=4,
            grid=(MOE_WORK, 2 * MOE_NJ),
            in_specs=[
                any_space, any_space, any_space, any_space,
                bias_spec, bias_spec, bias_spec,
            ],
            out_specs=pl.BlockSpec((MOE_ROWS, MOE_TN), o_map),
            scratch_shapes=[
                pltpu.VMEM((2, MOE_SUB, D_MODEL), f32),
                pltpu.VMEM((MOE_ROWS, D_MODEL), bf16),
                pltpu.VMEM((MOE_NJ, MOE_ROWS, MOE_TN), bf16),
                pltpu.VMEM((D_MODEL, MOE_TN), bf16),
                pltpu.VMEM((D_MODEL, MOE_TN), bf16),
                pltpu.VMEM((MOE_RING, D_MODEL, MOE_TN), f32),
                pltpu.SemaphoreType.DMA((2,)),
                pltpu.SemaphoreType.DMA((MOE_RING,)),
            ],
        ),
        out_shape=jax.ShapeDtypeStruct((N_EXPERTS * EXPERT_CAP, D_MODEL), f32),
        compiler_params=_cparams(("arbitrary", "arbitrary")),
        name="moe_experts",
    )(we, wq, wn, tot, xs, w_gate, w_up, w_down,
      b_gate.reshape(N_EXPERTS, MOE_NJ, MOE_TN), b_up.reshape(N_EXPERTS, MOE_NJ, MOE_TN),
      b_down.reshape(N_EXPERTS, MOE_NJ, MOE_TN))


def _work_list(cnt_i):
    nq = (cnt_i + MOE_ROWS - 1) // MOE_ROWS
    ends = jnp.cumsum(nq)
    total = ends[-1]
    w = jnp.arange(MOE_WORK, dtype=i32)
    wc = jnp.minimum(w, total - 1)
    e = jnp.minimum(jnp.sum((ends[None, :] <= wc[:, None]).astype(i32), axis=1), N_EXPERTS - 1)
    q = wc - (ends[e] - nq[e])
    rows = jnp.clip(cnt_i[e] - q * MOE_ROWS, 0, MOE_ROWS)
    n = jnp.where(w < total, (rows + MOE_UNIT - 1) // MOE_UNIT, 0)
    return e.astype(i32), q.astype(i32), n.astype(i32), total.astype(i32).reshape(1)


def _combine_kernel(pos_ref, o_ref, gate_ref, h2_ref, fg_ref, out_ref, obuf, sem, *, tm):
    i = pl.program_id(0)

    def body(g, carry):
        t0 = pl.multiple_of(g * 8, 8)
        p0 = pl.multiple_of((i * tm + t0) * TOP_K, 8 * TOP_K)
        for u in range(8):
            for k in range(TOP_K):
                p = pos_ref[p0 + (u * TOP_K + k)]
                pltpu.make_async_copy(o_ref.at[pl.ds(p, 1)],
                                      obuf.at[pl.ds(k * tm + t0 + u, 1)],
                                      sem).start(priority=k % 2)
        return carry

    lax.fori_loop(0, tm // 8, body, 0)
    pltpu.make_async_copy(obuf, obuf, sem).wait()

    gates = gate_ref[...]
    y = h2_ref[...]
    for k in range(TOP_K):
        y = y + gates[:, k:k + 1] * obuf[k * tm:(k + 1) * tm, :]
    out = y * lax.rsqrt(jnp.mean(y * y, axis=-1, keepdims=True) + NORM_EPS)
    out_ref[...] = out * fg_ref[...]


def _combine(pos_flat, o, gates, h2, final_g, tm=256):
    m = h2.shape[0]
    return pl.pallas_call(
        functools.partial(_combine_kernel, tm=tm),
        grid_spec=pltpu.PrefetchScalarGridSpec(
            num_scalar_prefetch=1,
            grid=(m // tm,),
            in_specs=[
                pl.BlockSpec(memory_space=pl.ANY),
                pl.BlockSpec((tm, TOP_K), lambda i, pos: (i, 0)),
                pl.BlockSpec((tm, D_MODEL), lambda i, pos: (i, 0)),
                pl.BlockSpec((1, D_MODEL), lambda i, pos: (0, 0)),
            ],
            out_specs=pl.BlockSpec((tm, D_MODEL), lambda i, pos: (i, 0)),
            scratch_shapes=[
                pltpu.VMEM((TOP_K * tm, D_MODEL), f32),
                pltpu.SemaphoreType.DMA(()),
            ],
        ),
        out_shape=jax.ShapeDtypeStruct((m, D_MODEL), f32),
        compiler_params=_cparams(("arbitrary",)),
        name="combine",
    )(pos_flat, o, gates, h2, final_g)


def kernel(x, meta_tokens, norm1_g, w_in, pool_w, pool_scale, conv_w, conv_b, dt_bias, a_log,
           d_skip, ssm_norm_g, w_out, norm2_g, router_w, router_b, w_gate, b_gate, w_up, b_up,
           w_down, b_down, final_g):
    assert x.shape == (1, SEQ, D_MODEL)
    x2d = x.reshape(SEQ, D_MODEL)
    lead = CHUNK - N_META
    meta_chunk = jnp.concatenate([jnp.zeros((lead, D_MODEL), f32), meta_tokens.astype(f32)], 0)

    g1 = norm1_g[0].reshape(1, D_MODEL)
    w_main = w_in[0, :, :D_MAIN].astype(bf16)
    w_dt = w_in[0, :, D_MAIN:].astype(bf16)
    head_of_col = jnp.arange(D_SSM, dtype=i32) // HEAD_DIM
    expand = (head_of_col[None, :] == jnp.arange(N_HEADS, dtype=i32)[:, None]).astype(bf16)
    mix_params = (
        pool_w[0], pool_scale[0].reshape(1, D_POOL), conv_w[0], conv_b[0].reshape(1, D_CONV),
        dt_bias[0].reshape(1, N_HEADS), a_log[0].reshape(1, N_HEADS),
        jnp.repeat(d_skip[0], HEAD_DIM).reshape(1, D_SSM), ssm_norm_g[0].reshape(1, D_SSM),
        expand)

    proj_m, dt_m = _inproj(meta_chunk, g1, w_main, w_dt, tm=CHUNK)
    zeros_state = jnp.zeros((N_GROUPS, N_STATE, GROUP_COLS), f32)
    _, state_m, utail_m, ctail_m = _mixer(
        proj_m, dt_m, mix_params, zeros_state, jnp.zeros((POOL_HALO, D_POOL), f32),
        jnp.zeros((CONV_HALO, D_CONV), f32), tm=CHUNK, lead=lead)

    proj, dt_raw = _inproj(x2d, g1, w_main, w_dt, tm=512)
    ymix, _, _, _ = _mixer(proj, dt_raw, mix_params, state_m, utail_m, ctail_m, tm=256, lead=0)

    h2, hn2, pos, gates, cnt = _outproj(
        ymix, w_out[0].astype(bf16), x2d, norm2_g[0].reshape(1, D_MODEL),
        router_w[0].astype(bf16),
        router_b[0].reshape(1, N_EXPERTS))

    pos_flat = pos.reshape(SEQ * TOP_K)
    cnt_i = cnt.reshape(N_EXPERTS).astype(i32)
    xs = _scatter(pos_flat, cnt_i, hn2)
    we, wq, wn, tot = _work_list(cnt_i)
    o = _moe(we, wq, wn, tot, xs, w_gate[0], w_up[0], w_down[0], b_gate[0], b_up[0], b_down[0])
    out = _combine(pos_flat, o, gates, h2, final_g.reshape(1, D_MODEL))
    return out.reshape(1, SEQ, D_MODEL)
```

```python
import functools

import jax
import jax.numpy as jnp
from jax import lax
from jax.experimental import pallas as pl
from jax.experimental.pallas import tpu as pltpu

f32 = jnp.float32
bf16 = jnp.bfloat16
i32 = jnp.int32

D_MODEL = 2048
SEQ = 8192
CHUNK = 64
N_META = 16
D_POOL = 1024
POOL_WINDOWS = (2, 4, 8, 16)
POOL_GROUP = 256
D_SSM = 1024
HEAD_DIM = 64
N_HEADS = 16
N_GROUPS = 2
HEADS_PER_GROUP = N_HEADS // N_GROUPS
N_STATE = 128
GROUP_COLS = HEADS_PER_GROUP * HEAD_DIM
D_CONV = D_SSM + 2 * N_GROUPS * N_STATE
D_MAIN = D_POOL + D_SSM + D_CONV
N_EXPERTS = 32
TOP_K = 4
D_EXPERT = 2048
SWIGLU_LIMIT = 7.0
SWIGLU_ALPHA = 1.702
NORM_EPS = 1e-5

POOL_HALO = 16
CONV_HALO = 8

MOE_SUB = 256
MOE_UNIT = 128
MOE_ROWS = 1280
MOE_SUBS = MOE_ROWS // MOE_SUB
ZERO_ROWS = MOE_SUB + 8
EXPERT_CAP = -(-(SEQ + ZERO_ROWS) // MOE_ROWS) * MOE_ROWS
MOE_TN = 512
MOE_NJ = D_EXPERT // MOE_TN
MOE_WORK = (SEQ * TOP_K) // MOE_ROWS + N_EXPERTS
MOE_TILES = 3 * MOE_NJ
MOE_RING = 6
assert MOE_TILES % MOE_RING == 0

VMEM_LIMIT = 56 * 1024 * 1024


def _cparams(sem):
    return pltpu.CompilerParams(dimension_semantics=sem, vmem_limit_bytes=VMEM_LIMIT)


PROJ_TN = 512


def _inproj_kernel(x_ref, g_ref, w_ref, wdt_ref, proj_ref, dt_ref):
    x = x_ref[...]
    y = x * lax.rsqrt(jnp.mean(x * x, axis=-1, keepdims=True) + NORM_EPS)
    hn = (y * g_ref[...]).astype(bf16)
    dt_ref[...] = jnp.dot(hn, wdt_ref[...], preferred_element_type=f32)
    for c in range(D_MAIN // PROJ_TN):
        cols = slice(c * PROJ_TN, (c + 1) * PROJ_TN)
        proj_ref[:, cols] = jnp.dot(hn, w_ref[:, cols], preferred_element_type=f32)


def _inproj(x2d, g, w_main, w_dt, tm):
    m = x2d.shape[0]
    resident = dict(pipeline_mode=pl.Buffered(1))
    return pl.pallas_call(
        _inproj_kernel,
        grid=(m // tm,),
        in_specs=[
            pl.BlockSpec((tm, D_MODEL), lambda i: (i, 0)),
            pl.BlockSpec((1, D_MODEL), lambda i: (0, 0)),
            pl.BlockSpec((D_MODEL, D_MAIN), lambda i: (0, 0), **resident),
            pl.BlockSpec((D_MODEL, N_HEADS), lambda i: (0, 0), **resident),
        ],
        out_specs=[
            pl.BlockSpec((tm, D_MAIN), lambda i: (i, 0)),
            pl.BlockSpec((tm, N_HEADS), lambda i: (i, 0)),
        ],
        out_shape=[
            jax.ShapeDtypeStruct((m, D_MAIN), f32),
            jax.ShapeDtypeStruct((m, N_HEADS), f32),
        ],
        compiler_params=_cparams(("arbitrary",)),
        name="inproj",
    )(x2d, g, w_main, w_dt)


def _split_dot(q, e):
    hi = q.astype(bf16)
    lo = (q - hi.astype(f32)).astype(bf16)
    return (jnp.dot(hi, e, preferred_element_type=f32)
            + jnp.dot(lo, e, preferred_element_type=f32))


def _nt_dot(a, b):
    return lax.dot_general(a, b, (((1,), (1,)), ((), ())), preferred_element_type=f32)


def _mixer_kernel(u_ref, z_ref, xs_ref, bc_ref, dt_ref,
                  poolw_ref, pscale_ref, convw_ref, convb_ref, dtb_ref, alog_ref,
                  dskip_ref, ng_ref, expand_ref, state0_ref, uhalo0_ref, chalo0_ref,
                  y_ref, state_out_ref, utail_ref, ctail_ref,
                  state, uext, cext, *, tm, lead):
    i = pl.program_id(0)
    nch = tm // CHUNK

    @pl.when(i == 0)
    def _():
        state[...] = state0_ref[...]
        uext[0:POOL_HALO, :] = uhalo0_ref[...]
        cext[0:CONV_HALO, :] = chalo0_ref[...]

    uext[POOL_HALO:, :] = u_ref[...]
    cext[CONV_HALO:, 0:D_SSM] = xs_ref[...]
    cext[CONV_HALO:, D_SSM:] = bc_ref[...]

    for g, w in enumerate(POOL_WINDOWS):
        cols = slice(g * POOL_GROUP, (g + 1) * POOL_GROUP)
        ue = uext[:, cols]
        s = ue
        sh = 1
        while sh < w:
            s = s + pltpu.roll(s, sh, axis=0)
            sh *= 2
        d = s[POOL_HALO:, :] / float(w) - ue[POOL_HALO:, :]
        yp = jnp.dot(d.astype(bf16), poolw_ref[g].astype(bf16), preferred_element_type=f32)
        y_ref[:, cols] = (yp * pscale_ref[:, cols]).astype(bf16)

    ce = cext[...]
    cw = convw_ref[...]
    acc = (ce * cw[3:4, :] + pltpu.roll(ce, 1, axis=0) * cw[2:3, :]
           + pltpu.roll(ce, 2, axis=0) * cw[1:2, :] + pltpu.roll(ce, 3, axis=0) * cw[0:1, :])
    acc = acc[CONV_HALO:, :] + convb_ref[...]
    xbc = acc * jax.nn.sigmoid(acc)
    xs = xbc[:, 0:D_SSM]
    bm = xbc[:, D_SSM:D_SSM + N_GROUPS * N_STATE]
    cm = xbc[:, D_SSM + N_GROUPS * N_STATE:]

    uext[0:POOL_HALO, :] = uext[tm:tm + POOL_HALO, :]
    cext[0:CONV_HALO, :] = cext[tm:tm + CONV_HALO, :]

    v = dt_ref[...] + dtb_ref[...]
    dt = jnp.maximum(v, 0.0) + jnp.log1p(jnp.exp(-jnp.abs(v)))
    row = lax.broadcasted_iota(i32, (tm, N_HEADS), 0)
    if lead:
        dt = jnp.where(row >= lead, dt, 0.0)
    a = -jnp.exp(alog_ref[...])
    acs = dt * a
    rin = row % CHUNK
    sh = 1
    while sh < CHUNK:
        acs = acs + jnp.where(rin >= sh, pltpu.roll(acs, sh, axis=0), 0.0)
        sh *= 2
    alast = jnp.concatenate(
        [jnp.broadcast_to(acs[c * CHUNK + CHUNK - 1:c * CHUNK + CHUNK, :], (CHUNK, N_HEADS))
         for c in range(nch)], axis=0)
    dec_st = jnp.exp(alast - acs)
    dec_in = jnp.exp(acs)

    expand = expand_ref[...]
    dt_x = _split_dot(dt, expand)
    dtds_x = _split_dot(dt * dec_st, expand)
    decin_x = _split_dot(dec_in, expand)

    eye_h = (lax.broadcasted_iota(i32, (N_HEADS, N_HEADS), 0)
             == lax.broadcasted_iota(i32, (N_HEADS, N_HEADS), 1)).astype(bf16)
    p1 = acs.astype(bf16)
    r1 = acs - p1.astype(f32)
    p2 = r1.astype(bf16)
    p3 = (r1 - p2.astype(f32)).astype(bf16)
    acs_t = _nt_dot(eye_h, p1) + _nt_dot(eye_h, p2) + _nt_dot(eye_h, p3)

    xdt = (xs * dt_x).astype(bf16)
    xds = (xs * dtds_x).astype(bf16)

    eye_n = (lax.broadcasted_iota(i32, (N_STATE, N_STATE), 0)
             == lax.broadcasted_iota(i32, (N_STATE, N_STATE), 1)).astype(bf16)
    causal = (lax.broadcasted_iota(i32, (CHUNK, CHUNK), 0)
              >= lax.broadcasted_iota(i32, (CHUNK, CHUNK), 1))
    col_head = lax.broadcasted_iota(i32, (CHUNK, GROUP_COLS), 1) // HEAD_DIM

    y_chunks = []
    for c in range(nch):
        r0 = c * CHUNK
        y_groups = []
        for g in range(N_GROUPS):
            gc = slice(g * GROUP_COLS, (g + 1) * GROUP_COLS)
            cg = cm[r0:r0 + CHUNK, g * N_STATE:(g + 1) * N_STATE].astype(bf16)
            bg = bm[r0:r0 + CHUNK, g * N_STATE:(g + 1) * N_STATE].astype(bf16)
            gmat = _nt_dot(cg, bg)
            st = state[g]
            y_off = (jnp.dot(cg, st.astype(bf16), preferred_element_type=f32)
                     * decin_x[r0:r0 + CHUNK, gc])
            scores = []
            for h in range(HEADS_PER_GROUP):
                hh = g * HEADS_PER_GROUP + h
                a_col = acs[r0:r0 + CHUNK, hh:hh + 1]
                a_row = acs_t[hh:hh + 1, r0:r0 + CHUNK]
                lmat = jnp.exp(jnp.where(causal, a_col - a_row, -jnp.inf))
                scores.append((gmat * lmat).astype(bf16))
            sc = jnp.concatenate(scores, axis=0)
            full = jnp.dot(sc, xdt[r0:r0 + CHUNK, gc], preferred_element_type=f32)
            y_diag = jnp.zeros((CHUNK, GROUP_COLS), f32)
            for h in range(HEADS_PER_GROUP):
                y_diag = jnp.where(col_head == h, full[h * CHUNK:(h + 1) * CHUNK, :], y_diag)
            y_groups.append(y_diag + y_off)
            bg_t = _nt_dot(eye_n, bg).astype(bf16)
            upd = jnp.dot(bg_t, xds[r0:r0 + CHUNK, gc], preferred_element_type=f32)
            state[g] = st * decin_x[r0 + CHUNK - 1:r0 + CHUNK, gc] + upd
        y_chunks.append(jnp.concatenate(y_groups, axis=1))
    y = jnp.concatenate(y_chunks, axis=0) if nch > 1 else y_chunks[0]

    y = y + dskip_ref[...] * xs
    zz = z_ref[...]
    y = y * (zz * jax.nn.sigmoid(zz))
    for g in range(N_GROUPS):
        gc = slice(g * GROUP_COLS, (g + 1) * GROUP_COLS)
        yg = y[:, gc]
        yg = yg * lax.rsqrt(jnp.mean(yg * yg, axis=-1, keepdims=True) + NORM_EPS)
        y_ref[:, D_POOL + g * GROUP_COLS:D_POOL + (g + 1) * GROUP_COLS] = (
            yg * ng_ref[:, gc]).astype(bf16)

    state_out_ref[...] = state[...]
    utail_ref[...] = uext[0:POOL_HALO, :]
    ctail_ref[...] = cext[0:CONV_HALO, :]


def _mixer(proj, dt_raw, params, state0, uhalo0, chalo0, tm, lead):
    m = proj.shape[0]
    (pool_w, pool_scale, conv_w, conv_b, dt_bias, a_log, dskip_x, norm_g, expand) = params
    full = lambda shape: pl.BlockSpec(shape, lambda i: (0,) * len(shape))
    return pl.pallas_call(
        functools.partial(_mixer_kernel, tm=tm, lead=lead),
        grid=(m // tm,),
        in_specs=[
            pl.BlockSpec((tm, D_POOL), lambda i: (i, 0)),
            pl.BlockSpec((tm, D_SSM), lambda i: (i, 1)),
            pl.BlockSpec((tm, D_SSM), lambda i: (i, 2)),
            pl.BlockSpec((tm, 2 * N_GROUPS * N_STATE), lambda i: (i, 6)),
            pl.BlockSpec((tm, N_HEADS), lambda i: (i, 0)),
            full((len(POOL_WINDOWS), POOL_GROUP, POOL_GROUP)),
            full((1, D_POOL)),
            full((4, D_CONV)),
            full((1, D_CONV)),
            full((1, N_HEADS)),
            full((1, N_HEADS)),
            full((1, D_SSM)),
            full((1, D_SSM)),
            full((N_HEADS, D_SSM)),
            full((N_GROUPS, N_STATE, GROUP_COLS)),
            full((POOL_HALO, D_POOL)),
            full((CONV_HALO, D_CONV)),
        ],
        out_specs=[
            pl.BlockSpec((tm, D_MODEL), lambda i: (i, 0)),
            full((N_GROUPS, N_STATE, GROUP_COLS)),
            full((POOL_HALO, D_POOL)),
            full((CONV_HALO, D_CONV)),
        ],
        out_shape=[
            jax.ShapeDtypeStruct((m, D_MODEL), bf16),
            jax.ShapeDtypeStruct((N_GROUPS, N_STATE, GROUP_COLS), f32),
            jax.ShapeDtypeStruct((POOL_HALO, D_POOL), f32),
            jax.ShapeDtypeStruct((CONV_HALO, D_CONV), f32),
        ],
        scratch_shapes=[
            pltpu.VMEM((N_GROUPS, N_STATE, GROUP_COLS), f32),
            pltpu.VMEM((tm + POOL_HALO, D_POOL), f32),
            pltpu.VMEM((tm + CONV_HALO, D_CONV), f32),
        ],
        compiler_params=_cparams(("arbitrary",)),
        name="mixer",
    )(proj, proj, proj, proj, dt_raw, pool_w, pool_scale, conv_w, conv_b, dt_bias, a_log,
      dskip_x, norm_g, expand, state0, uhalo0, chalo0)


def _outproj_kernel(y_ref, w_ref, x_ref, g2_ref, rw_ref, rb_ref,
                    h2_ref, hn_ref, pos_ref, gate_ref, cnt_ref,
                    cnt, *, tm, tn):
    i = pl.program_id(0)
    nj = D_MODEL // tn

    @pl.when(i == 0)
    def _():
        cnt[...] = jnp.zeros_like(cnt)

    yv = y_ref[...]
    ss = jnp.zeros((tm, 1), f32)
    for q in range(nj):
        cols = slice(q * tn, (q + 1) * tn)
        h = x_ref[:, cols] + jnp.dot(yv, w_ref[:, cols], preferred_element_type=f32)
        h2_ref[:, cols] = h
        ss = ss + jnp.sum(h * h, axis=-1, keepdims=True)

    def route():
        inv = lax.rsqrt(ss / float(D_MODEL) + NORM_EPS)
        logits = jnp.zeros((tm, N_EXPERTS), f32) + rb_ref[...]
        for q in range(nj):
            cols = slice(q * tn, (q + 1) * tn)
            hn = h2_ref[:, cols] * inv * g2_ref[:, cols]
            hn_ref[:, cols] = hn
            logits = logits + jnp.dot(hn.astype(bf16), rw_ref[cols, :],
                                      preferred_element_type=f32)

        lane = lax.broadcasted_iota(i32, (tm, N_EXPERTS), 1).astype(f32)
        vals = logits
        picked = jnp.zeros((tm, N_EXPERTS), f32)
        tops, idxs = [], []
        for _ in range(TOP_K):
            mx = jnp.max(vals, axis=-1, keepdims=True)
            idx = jnp.min(jnp.where(vals == mx, lane, float(N_EXPERTS)), axis=-1, keepdims=True)
            hit = lane == idx
            tops.append(mx)
            idxs.append(idx)
            picked = picked + hit.astype(f32)
            vals = jnp.where(hit, -jnp.inf, vals)
        exps = [jnp.exp(t - tops[0]) for t in tops]
        denom = exps[0] + exps[1] + exps[2] + exps[3]

        tri = (lax.broadcasted_iota(i32, (tm, tm), 0)
               > lax.broadcasted_iota(i32, (tm, tm), 1)).astype(bf16)
        cum = jnp.dot(tri, picked.astype(bf16), preferred_element_type=f32) + cnt[...]
        lane4 = lax.broadcasted_iota(i32, (tm, TOP_K), 1)
        pos = jnp.zeros((tm, TOP_K), i32)
        gates = jnp.zeros((tm, TOP_K), f32)
        for k in range(TOP_K):
            rank = jnp.sum(jnp.where(lane == idxs[k], cum, 0.0), axis=-1, keepdims=True)
            p = idxs[k].astype(i32) * EXPERT_CAP + rank.astype(i32)
            pos = jnp.where(lane4 == k, p, pos)
            gates = jnp.where(lane4 == k, exps[k] / denom, gates)
        pos_ref[...] = pos
        gate_ref[...] = gates
        cnt[...] = cnt[...] + jnp.sum(picked, axis=0, keepdims=True)
        cnt_ref[...] = cnt[...]

    route()


def _outproj(ymix, w_out, x2d, g2, rw, rb, tm=512, tn=512):
    m = x2d.shape[0]
    resident = dict(pipeline_mode=pl.Buffered(1))
    return pl.pallas_call(
        functools.partial(_outproj_kernel, tm=tm, tn=tn),
        grid=(m // tm,),
        in_specs=[
            pl.BlockSpec((tm, D_MODEL), lambda i: (i, 0)),
            pl.BlockSpec((D_MODEL, D_MODEL), lambda i: (0, 0), **resident),
            pl.BlockSpec((tm, D_MODEL), lambda i: (i, 0)),
            pl.BlockSpec((1, D_MODEL), lambda i: (0, 0)),
            pl.BlockSpec((D_MODEL, N_EXPERTS), lambda i: (0, 0), **resident),
            pl.BlockSpec((1, N_EXPERTS), lambda i: (0, 0)),
        ],
        out_specs=[
            pl.BlockSpec((tm, D_MODEL), lambda i: (i, 0)),
            pl.BlockSpec((tm, D_MODEL), lambda i: (i, 0)),
            pl.BlockSpec((tm, TOP_K), lambda i: (i, 0)),
            pl.BlockSpec((tm, TOP_K), lambda i: (i, 0)),
            pl.BlockSpec((1, N_EXPERTS), lambda i: (0, 0)),
        ],
        out_shape=[
            jax.ShapeDtypeStruct((m, D_MODEL), f32),
            jax.ShapeDtypeStruct((m, D_MODEL), f32),
            jax.ShapeDtypeStruct((m, TOP_K), i32),
            jax.ShapeDtypeStruct((m, TOP_K), f32),
            jax.ShapeDtypeStruct((1, N_EXPERTS), f32),
        ],
        scratch_shapes=[pltpu.VMEM((1, N_EXPERTS), f32)],
        compiler_params=_cparams(("arbitrary",)),
        name="outproj_router",
    )(ymix, w_out, x2d, g2, rw, rb)


def _scatter_kernel(pos_ref, cnt_ref, hn_ref, xs_ref, zbuf, sem, zsem, *, tb):
    i = pl.program_id(0)

    @pl.when(i == 0)
    def _():
        zbuf[...] = jnp.zeros_like(zbuf)
        for e in range(N_EXPERTS):
            start = pl.multiple_of(e * EXPERT_CAP + (cnt_ref[e] // 8) * 8, 8)
            pltpu.make_async_copy(zbuf, xs_ref.at[pl.ds(start, ZERO_ROWS)], zsem).start()
        for e in range(N_EXPERTS):
            start = pl.multiple_of(e * EXPERT_CAP + (cnt_ref[e] // 8) * 8, 8)
            pltpu.make_async_copy(zbuf, xs_ref.at[pl.ds(start, ZERO_ROWS)], zsem).wait()

    p0 = pl.multiple_of(i * (tb * TOP_K), tb * TOP_K)
    for t in range(tb):
        for k in range(TOP_K):
            p = pos_ref[p0 + (t * TOP_K + k)]
            pltpu.make_async_copy(hn_ref.at[pl.ds(t, 1)], xs_ref.at[pl.ds(p, 1)],
                                  sem).start(priority=k % 2)
    done = xs_ref.at[pl.ds(0, tb * TOP_K)]
    pltpu.make_async_copy(done, done, sem).wait()


def _scatter(pos_flat, cnt_i, hn2, tb=256):
    m = hn2.shape[0]
    return pl.pallas_call(
        functools.partial(_scatter_kernel, tb=tb),
        grid_spec=pltpu.PrefetchScalarGridSpec(
            num_scalar_prefetch=2,
            grid=(m // tb,),
            in_specs=[pl.BlockSpec((tb, D_MODEL), lambda i, pos, cnt: (i, 0))],
            out_specs=pl.BlockSpec(memory_space=pl.ANY),
            scratch_shapes=[
                pltpu.VMEM((ZERO_ROWS, D_MODEL), f32),
                pltpu.SemaphoreType.DMA(()),
                pltpu.SemaphoreType.DMA(()),
            ],
        ),
        out_shape=jax.ShapeDtypeStruct((N_EXPERTS * EXPERT_CAP, D_MODEL), f32),
        compiler_params=_cparams(("arbitrary",)),
        name="scatter_rows",
    )(pos_flat, cnt_i, hn2)


def _moe_kernel(we_ref, wq_ref, wn_ref, tot_ref,
                xs_ref, wg_hbm, wu_hbm, wd_hbm, bg_ref, bu_ref, bd_ref,
                o_ref, stage, xb, abuf, ws_a, ws_b, ring, sem, wsem):
    w = pl.program_id(0)
    j = pl.program_id(1)
    e_cur = we_ref[w]
    units = wn_ref[w]
    n = (units + 1) // 2
    wnext = jnp.minimum(w + 1, MOE_WORK - 1)
    n_next = jnp.where(w + 1 < MOE_WORK, (wn_ref[wnext] + 1) // 2, 0)

    def tile_copy(item, r):
        e = we_ref[item]
        if r < 2 * MOE_NJ:
            src, blk = (wg_hbm if r % 2 == 0 else wu_hbm), r // 2
        else:
            src, blk = wd_hbm, r - 2 * MOE_NJ
        return pltpu.make_async_copy(src.at[e, :, pl.ds(blk * MOE_TN, MOE_TN)],
                                     ring.at[r % MOE_RING], wsem.at[r % MOE_RING])

    def tile_take(r, dst):
        tile_copy(w, r).wait()
        dst[...] = ring[r % MOE_RING].astype(bf16)
        r2 = r + MOE_RING
        if r2 < MOE_TILES:
            tile_copy(w, r2).start()
        else:
            @pl.when(w + 1 < tot_ref[0])
            def _():
                tile_copy(wnext, r2 - MOE_TILES).start()

    @pl.when((w == 0) & (j == 0))
    def _():
        for r in range(MOE_RING):
            tile_copy(w, r).start()

    def x_copy(item, s):
        base = we_ref[item] * EXPERT_CAP + wq_ref[item] * MOE_ROWS
        return pltpu.make_async_copy(
            xs_ref.at[pl.ds(pl.multiple_of(base + s * MOE_SUB, MOE_SUB), MOE_SUB)],
            stage.at[s % 2], sem.at[s % 2])

    def x_start(item, count, s):
        @pl.when(s < count)
        def _():
            x_copy(item, s).start()

    def x_finish(item, count, s):
        @pl.when(s < count)
        def _():
            x_copy(item, s).wait()
            xb[pl.ds(s * MOE_SUB, MOE_SUB), :] = stage[s % 2].astype(bf16)

    @pl.when((w == 0) & (j == 0))
    def _():
        x_start(w, n, 0)
        x_start(w, n, 1)
        for s in range(MOE_SUBS):
            x_finish(w, n, s)
            if s + 2 < MOE_SUBS:
                x_start(w, n, s + 2)

    def per_subtile(fn):
        def pair(p, carry):
            r0 = p * (2 * MOE_SUB)
            fn(r0, MOE_SUB)
            fn(r0 + MOE_SUB, MOE_SUB)
            return carry

        pairs = units // 4
        rem = units % 4
        lax.fori_loop(0, pairs, pair, 0)

        @pl.when(rem >= 2)
        def _():
            fn(pairs * (2 * MOE_SUB), MOE_SUB)

        @pl.when(rem % 2 == 1)
        def _():
            fn(pairs * (2 * MOE_SUB) + (rem // 2) * MOE_SUB, MOE_UNIT)

    for jj in range(2 * MOE_NJ):
        @pl.when((j == jj) & (n > 0))
        def _():
            if jj < MOE_NJ:
                tile_take(2 * jj, ws_a)
                tile_take(2 * jj + 1, ws_b)
            else:
                tile_take(MOE_NJ + jj, ws_a)

    @pl.when((j < MOE_NJ) & (n > 0))
    def _():
        bgv = bg_ref[e_cur, pl.ds(j, 1), :]
        buv = bu_ref[e_cur, pl.ds(j, 1), :]

        def gate_up(r0, nrows):
            rows = pl.ds(pl.multiple_of(r0, MOE_UNIT), nrows)
            x = xb[rows, :]
            g = jnp.dot(x, ws_a[...], preferred_element_type=f32) + bgv
            u = jnp.dot(x, ws_b[...], preferred_element_type=f32) + buv
            g = jnp.minimum(g, SWIGLU_LIMIT)
            u = jnp.clip(u, -SWIGLU_LIMIT, SWIGLU_LIMIT)
            act = (u + 1.0) * (g * jax.nn.sigmoid(SWIGLU_ALPHA * g))
            abuf[j, rows, :] = act.astype(bf16)

        per_subtile(gate_up)

    @pl.when((j >= MOE_NJ) & (n > 0))
    def _():
        bdv = bd_ref[e_cur, pl.ds(j - MOE_NJ, 1), :]

        def down(r0, nrows):
            rows = pl.ds(pl.multiple_of(r0, MOE_UNIT), nrows)
            acc = jnp.zeros((nrows, MOE_TN), f32) + bdv
            for q in range(MOE_NJ):
                acc = acc + jnp.dot(abuf[q, rows, :], ws_a[q * MOE_TN:(q + 1) * MOE_TN, :],
                                    preferred_element_type=f32)
            o_ref[rows, :] = acc

        per_subtile(down)

    assert 2 * (MOE_NJ - 1) >= MOE_SUBS
    for k in range(MOE_NJ):
        @pl.when(j == MOE_NJ + k)
        def _():
            if k == 0:
                x_start(wnext, n_next, 0)
                x_start(wnext, n_next, 1)
            else:
                for s in (2 * k - 2, 2 * k - 1):
                    if s < MOE_SUBS:
                        x_finish(wnext, n_next, s)
                        if s + 2 < MOE_SUBS:
                            x_start(wnext, n_next, s + 2)


def _moe(we, wq, wn, tot, xs, w_gate, w_up, w_down, b_gate, b_up, b_down):
    blocks_per_expert = EXPERT_CAP // MOE_ROWS

    bias_spec = pl.BlockSpec((N_EXPERTS, MOE_NJ, MOE_TN), lambda w, j, we, wq, wn, tot: (0, 0, 0),
                             pipeline_mode=pl.Buffered(1))

    def o_map(w, j, we, wq, wn, tot):
        jj = jnp.where(wn[w] > 0, jnp.maximum(j - MOE_NJ, 0), MOE_NJ - 1)
        return (we[w] * blocks_per_expert + wq[w], jj)

    any_space = pl.BlockSpec(memory_space=pl.ANY)
    return pl.pallas_call(
        _moe_kernel,
        grid_spec=pltpu.PrefetchScalarGridSpec(
            num_scalar_prefetch=4,
            grid=(MOE_WORK, 2 * MOE_NJ),
            in_specs=[
                any_space, any_space, any_space, any_space,
                bias_spec, bias_spec, bias_spec,
            ],
            out_specs=pl.BlockSpec((MOE_ROWS, MOE_TN), o_map),
            scratch_shapes=[
                pltpu.VMEM((2, MOE_SUB, D_MODEL), f32),
                pltpu.VMEM((MOE_ROWS, D_MODEL), bf16),
                pltpu.VMEM((MOE_NJ, MOE_ROWS, MOE_TN), bf16),
                pltpu.VMEM((D_MODEL, MOE_TN), bf16),
                pltpu.VMEM((D_MODEL, MOE_TN), bf16),
                pltpu.VMEM((MOE_RING, D_MODEL, MOE_TN), f32),
                pltpu.SemaphoreType.DMA((2,)),
                pltpu.SemaphoreType.DMA((MOE_RING,)),
            ],
        ),
        out_shape=jax.ShapeDtypeStruct((N_EXPERTS * EXPERT_CAP, D_MODEL), f32),
        compiler_params=_cparams(("arbitrary", "arbitrary")),
        name="moe_experts",
    )(we, wq, wn, tot, xs, w_gate, w_up, w_down,
      b_gate.reshape(N_EXPERTS, MOE_NJ, MOE_TN), b_up.reshape(N_EXPERTS, MOE_NJ, MOE_TN),
      b_down.reshape(N_EXPERTS, MOE_NJ, MOE_TN))


def _work_list(cnt_i):
    nq = (cnt_i + MOE_ROWS - 1) // MOE_ROWS
    ends = jnp.cumsum(nq)
    total = ends[-1]
    w = jnp.arange(MOE_WORK, dtype=i32)
    wc = jnp.minimum(w, total - 1)
    e = jnp.minimum(jnp.sum((ends[None, :] <= wc[:, None]).astype(i32), axis=1), N_EXPERTS - 1)
    q = wc - (ends[e] - nq[e])
    rows = jnp.clip(cnt_i[e] - q * MOE_ROWS, 0, MOE_ROWS)
    n = jnp.where(w < total, (rows + MOE_UNIT - 1) // MOE_UNIT, 0)
    return e.astype(i32), q.astype(i32), n.astype(i32), total.astype(i32).reshape(1)


def _combine_kernel(pos_ref, o_ref, gate_ref, h2_ref, fg_ref, out_ref, obuf, sem, *, tm):
    i = pl.program_id(0)

    p0 = pl.multiple_of(i * (tm * TOP_K), tm * TOP_K)
    for t in range(tm):
        for k in range(TOP_K):
            p = pos_ref[p0 + (t * TOP_K + k)]
            pltpu.make_async_copy(o_ref.at[pl.ds(p, 1)], obuf.at[pl.ds(k * tm + t, 1)],
                                  sem).start(priority=k % 2)
    pltpu.make_async_copy(obuf, obuf, sem).wait()

    gates = gate_ref[...]
    y = h2_ref[...]
    for k in range(TOP_K):
        y = y + gates[:, k:k + 1] * obuf[k * tm:(k + 1) * tm, :]
    out = y * lax.rsqrt(jnp.mean(y * y, axis=-1, keepdims=True) + NORM_EPS)
    out_ref[...] = out * fg_ref[...]


def _combine(pos_flat, o, gates, h2, final_g, tm=256):
    m = h2.shape[0]
    return pl.pallas_call(
        functools.partial(_combine_kernel, tm=tm),
        grid_spec=pltpu.PrefetchScalarGridSpec(
            num_scalar_prefetch=1,
            grid=(m // tm,),
            in_specs=[
                pl.BlockSpec(memory_space=pl.ANY),
                pl.BlockSpec((tm, TOP_K), lambda i, pos: (i, 0)),
                pl.BlockSpec((tm, D_MODEL), lambda i, pos: (i, 0)),
                pl.BlockSpec((1, D_MODEL), lambda i, pos: (0, 0)),
            ],
            out_specs=pl.BlockSpec((tm, D_MODEL), lambda i, pos: (i, 0)),
            scratch_shapes=[
                pltpu.VMEM((TOP_K * tm, D_MODEL), f32),
                pltpu.SemaphoreType.DMA(()),
            ],
        ),
        out_shape=jax.ShapeDtypeStruct((m, D_MODEL), f32),
        compiler_params=_cparams(("arbitrary",)),
        name="combine",
    )(pos_flat, o, gates, h2, final_g)


def kernel(x, meta_tokens, norm1_g, w_in, pool_w, pool_scale, conv_w, conv_b, dt_bias, a_log,
           d_skip, ssm_norm_g, w_out, norm2_g, router_w, router_b, w_gate, b_gate, w_up, b_up,
           w_down, b_down, final_g):
    assert x.shape == (1, SEQ, D_MODEL)
    x2d = x.reshape(SEQ, D_MODEL)
    lead = CHUNK - N_META
    meta_chunk = jnp.concatenate([jnp.zeros((lead, D_MODEL), f32), meta_tokens.astype(f32)], 0)

    g1 = norm1_g[0].reshape(1, D_MODEL)
    w_main = w_in[0, :, :D_MAIN].astype(bf16)
    w_dt = w_in[0, :, D_MAIN:].astype(bf16)
    head_of_col = jnp.arange(D_SSM, dtype=i32) // HEAD_DIM
    expand = (head_of_col[None, :] == jnp.arange(N_HEADS, dtype=i32)[:, None]).astype(bf16)
    mix_params = (
        pool_w[0], pool_scale[0].reshape(1, D_POOL), conv_w[0], conv_b[0].reshape(1, D_CONV),
        dt_bias[0].reshape(1, N_HEADS), a_log[0].reshape(1, N_HEADS),
        jnp.repeat(d_skip[0], HEAD_DIM).reshape(1, D_SSM), ssm_norm_g[0].reshape(1, D_SSM),
        expand)

    proj_m, dt_m = _inproj(meta_chunk, g1, w_main, w_dt, tm=CHUNK)
    zeros_state = jnp.zeros((N_GROUPS, N_STATE, GROUP_COLS), f32)
    _, state_m, utail_m, ctail_m = _mixer(
        proj_m, dt_m, mix_params, zeros_state, jnp.zeros((POOL_HALO, D_POOL), f32),
        jnp.zeros((CONV_HALO, D_CONV), f32), tm=CHUNK, lead=lead)

    proj, dt_raw = _inproj(x2d, g1, w_main, w_dt, tm=512)
    ymix, _, _, _ = _mixer(proj, dt_raw, mix_params, state_m, utail_m, ctail_m, tm=256, lead=0)

    h2, hn2, pos, gates, cnt = _outproj(
        ymix, w_out[0].astype(bf16), x2d, norm2_g[0].reshape(1, D_MODEL),
        router_w[0].astype(bf16),
        router_b[0].reshape(1, N_EXPERTS))

    pos_flat = pos.reshape(SEQ * TOP_K)
    cnt_i = cnt.reshape(N_EXPERTS).astype(i32)
    xs = _scatter(pos_flat, cnt_i, hn2)
    we, wq, wn, tot = _work_list(cnt_i)
    o = _moe(we, wq, wn, tot, xs, w_gate[0], w_up[0], w_down[0], b_gate[0], b_up[0], b_down[0])
    out = _combine(pos_flat, o, gates, h2, final_g.reshape(1, D_MODEL))
    return out.reshape(1, SEQ, D_MODEL)
```

```python
import functools

import jax
import jax.numpy as jnp
from jax import lax
from jax.experimental import pallas as pl
from jax.experimental.pallas import tpu as pltpu

f32 = jnp.float32
bf16 = jnp.bfloat16
i32 = jnp.int32

D_MODEL = 2048
SEQ = 8192
CHUNK = 64
N_META = 16
D_POOL = 1024
POOL_WINDOWS = (2, 4, 8, 16)
POOL_GROUP = 256
D_SSM = 1024
HEAD_DIM = 64
N_HEADS = 16
N_GROUPS = 2
HEADS_PER_GROUP = N_HEADS // N_GROUPS
N_STATE = 128
GROUP_COLS = HEADS_PER_GROUP * HEAD_DIM
D_CONV = D_SSM + 2 * N_GROUPS * N_STATE
D_MAIN = D_POOL + D_SSM + D_CONV
N_EXPERTS = 32
TOP_K = 4
D_EXPERT = 2048
SWIGLU_LIMIT = 7.0
SWIGLU_ALPHA = 1.702
NORM_EPS = 1e-5

GATE_ROWS = 8
POOL_HALO = 16
CONV_HALO = 8

MOE_SUB = 256
MOE_UNIT = 128
MOE_ROWS = 1280
MOE_SUBS = MOE_ROWS // MOE_SUB
ZERO_ROWS = MOE_SUB + 8
EXPERT_CAP = -(-(SEQ + ZERO_ROWS) // MOE_ROWS) * MOE_ROWS
MOE_TN = 512
MOE_NJ = D_EXPERT // MOE_TN
MOE_WORK = (SEQ * TOP_K) // MOE_ROWS + N_EXPERTS
MOE_TILES = 3 * MOE_NJ
MOE_RING = 6
assert MOE_TILES % MOE_RING == 0

VMEM_LIMIT = 56 * 1024 * 1024


def _cparams(sem):
    return pltpu.CompilerParams(dimension_semantics=sem, vmem_limit_bytes=VMEM_LIMIT)


PROJ_TN = 512


def _inproj_kernel(x_ref, g_ref, w_ref, wdt_ref, proj_ref, dt_ref):
    x = x_ref[...]
    y = x * lax.rsqrt(jnp.mean(x * x, axis=-1, keepdims=True) + NORM_EPS)
    hn = (y * g_ref[...]).astype(bf16)
    dt_ref[...] = jnp.dot(hn, wdt_ref[...], preferred_element_type=f32)
    for c in range(D_MAIN // PROJ_TN):
        cols = slice(c * PROJ_TN, (c + 1) * PROJ_TN)
        proj_ref[:, cols] = jnp.dot(hn, w_ref[:, cols], preferred_element_type=f32)


def _inproj(x2d, g, w_main, w_dt, tm):
    m = x2d.shape[0]
    resident = dict(pipeline_mode=pl.Buffered(1))
    return pl.pallas_call(
        _inproj_kernel,
        grid=(m // tm,),
        in_specs=[
            pl.BlockSpec((tm, D_MODEL), lambda i: (i, 0)),
            pl.BlockSpec((1, D_MODEL), lambda i: (0, 0)),
            pl.BlockSpec((D_MODEL, D_MAIN), lambda i: (0, 0), **resident),
            pl.BlockSpec((D_MODEL, N_HEADS), lambda i: (0, 0), **resident),
        ],
        out_specs=[
            pl.BlockSpec((tm, D_MAIN), lambda i: (i, 0)),
            pl.BlockSpec((tm, N_HEADS), lambda i: (i, 0)),
        ],
        out_shape=[
            jax.ShapeDtypeStruct((m, D_MAIN), f32),
            jax.ShapeDtypeStruct((m, N_HEADS), f32),
        ],
        compiler_params=_cparams(("arbitrary",)),
        name="inproj",
    )(x2d, g, w_main, w_dt)


def _split_dot(q, e):
    hi = q.astype(bf16)
    lo = (q - hi.astype(f32)).astype(bf16)
    return (jnp.dot(hi, e, preferred_element_type=f32)
            + jnp.dot(lo, e, preferred_element_type=f32))


def _nt_dot(a, b):
    return lax.dot_general(a, b, (((1,), (1,)), ((), ())), preferred_element_type=f32)


def _mixer_kernel(u_ref, z_ref, xs_ref, bc_ref, dt_ref,
                  poolw_ref, pscale_ref, convw_ref, convb_ref, dtb_ref, alog_ref,
                  dskip_ref, ng_ref, expand_ref, state0_ref, uhalo0_ref, chalo0_ref,
                  y_ref, state_out_ref, utail_ref, ctail_ref,
                  state, uext, cext, *, tm, lead):
    i = pl.program_id(0)
    nch = tm // CHUNK

    @pl.when(i == 0)
    def _():
        state[...] = state0_ref[...]
        uext[0:POOL_HALO, :] = uhalo0_ref[...]
        cext[0:CONV_HALO, :] = chalo0_ref[...]

    uext[POOL_HALO:, :] = u_ref[...]
    cext[CONV_HALO:, 0:D_SSM] = xs_ref[...]
    cext[CONV_HALO:, D_SSM:] = bc_ref[...]

    for g, w in enumerate(POOL_WINDOWS):
        cols = slice(g * POOL_GROUP, (g + 1) * POOL_GROUP)
        ue = uext[:, cols]
        s = ue
        sh = 1
        while sh < w:
            s = s + pltpu.roll(s, sh, axis=0)
            sh *= 2
        d = s[POOL_HALO:, :] / float(w) - ue[POOL_HALO:, :]
        yp = jnp.dot(d.astype(bf16), poolw_ref[g].astype(bf16), preferred_element_type=f32)
        y_ref[:, cols] = (yp * pscale_ref[:, cols]).astype(bf16)

    ce = cext[...]
    cw = convw_ref[...]
    acc = (ce * cw[3:4, :] + pltpu.roll(ce, 1, axis=0) * cw[2:3, :]
           + pltpu.roll(ce, 2, axis=0) * cw[1:2, :] + pltpu.roll(ce, 3, axis=0) * cw[0:1, :])
    acc = acc[CONV_HALO:, :] + convb_ref[...]
    xbc = acc * jax.nn.sigmoid(acc)
    xs = xbc[:, 0:D_SSM]
    bm = xbc[:, D_SSM:D_SSM + N_GROUPS * N_STATE]
    cm = xbc[:, D_SSM + N_GROUPS * N_STATE:]

    uext[0:POOL_HALO, :] = uext[tm:tm + POOL_HALO, :]
    cext[0:CONV_HALO, :] = cext[tm:tm + CONV_HALO, :]

    v = dt_ref[...] + dtb_ref[...]
    dt = jnp.maximum(v, 0.0) + jnp.log1p(jnp.exp(-jnp.abs(v)))
    row = lax.broadcasted_iota(i32, (tm, N_HEADS), 0)
    if lead:
        dt = jnp.where(row >= lead, dt, 0.0)
    a = -jnp.exp(alog_ref[...])
    acs = dt * a
    rin = row % CHUNK
    sh = 1
    while sh < CHUNK:
        acs = acs + jnp.where(rin >= sh, pltpu.roll(acs, sh, axis=0), 0.0)
        sh *= 2
    alast = jnp.concatenate(
        [jnp.broadcast_to(acs[c * CHUNK + CHUNK - 1:c * CHUNK + CHUNK, :], (CHUNK, N_HEADS))
         for c in range(nch)], axis=0)
    dec_st = jnp.exp(alast - acs)
    dec_in = jnp.exp(acs)

    expand = expand_ref[...]
    dt_x = _split_dot(dt, expand)
    dtds_x = _split_dot(dt * dec_st, expand)
    decin_x = _split_dot(dec_in, expand)

    eye_h = (lax.broadcasted_iota(i32, (N_HEADS, N_HEADS), 0)
             == lax.broadcasted_iota(i32, (N_HEADS, N_HEADS), 1)).astype(bf16)
    p1 = acs.astype(bf16)
    r1 = acs - p1.astype(f32)
    p2 = r1.astype(bf16)
    p3 = (r1 - p2.astype(f32)).astype(bf16)
    acs_t = _nt_dot(eye_h, p1) + _nt_dot(eye_h, p2) + _nt_dot(eye_h, p3)

    xdt = (xs * dt_x).astype(bf16)
    xds = (xs * dtds_x).astype(bf16)

    eye_n = (lax.broadcasted_iota(i32, (N_STATE, N_STATE), 0)
             == lax.broadcasted_iota(i32, (N_STATE, N_STATE), 1)).astype(bf16)
    causal = (lax.broadcasted_iota(i32, (CHUNK, CHUNK), 0)
              >= lax.broadcasted_iota(i32, (CHUNK, CHUNK), 1))
    col_head = lax.broadcasted_iota(i32, (CHUNK, GROUP_COLS), 1) // HEAD_DIM

    y_chunks = []
    for c in range(nch):
        r0 = c * CHUNK
        y_groups = []
        for g in range(N_GROUPS):
            gc = slice(g * GROUP_COLS, (g + 1) * GROUP_COLS)
            cg = cm[r0:r0 + CHUNK, g * N_STATE:(g + 1) * N_STATE].astype(bf16)
            bg = bm[r0:r0 + CHUNK, g * N_STATE:(g + 1) * N_STATE].astype(bf16)
            gmat = _nt_dot(cg, bg)
            st = state[g]
            y_off = (jnp.dot(cg, st.astype(bf16), preferred_element_type=f32)
                     * decin_x[r0:r0 + CHUNK, gc])
            scores = []
            for h in range(HEADS_PER_GROUP):
                hh = g * HEADS_PER_GROUP + h
                a_col = acs[r0:r0 + CHUNK, hh:hh + 1]
                a_row = acs_t[hh:hh + 1, r0:r0 + CHUNK]
                lmat = jnp.exp(jnp.where(causal, a_col - a_row, -jnp.inf))
                scores.append((gmat * lmat).astype(bf16))
            sc = jnp.concatenate(scores, axis=0)
            full = jnp.dot(sc, xdt[r0:r0 + CHUNK, gc], preferred_element_type=f32)
            y_diag = jnp.zeros((CHUNK, GROUP_COLS), f32)
            for h in range(HEADS_PER_GROUP):
                y_diag = jnp.where(col_head == h, full[h * CHUNK:(h + 1) * CHUNK, :], y_diag)
            y_groups.append(y_diag + y_off)
            bg_t = _nt_dot(eye_n, bg).astype(bf16)
            upd = jnp.dot(bg_t, xds[r0:r0 + CHUNK, gc], preferred_element_type=f32)
            state[g] = st * decin_x[r0 + CHUNK - 1:r0 + CHUNK, gc] + upd
        y_chunks.append(jnp.concatenate(y_groups, axis=1))
    y = jnp.concatenate(y_chunks, axis=0) if nch > 1 else y_chunks[0]

    y = y + dskip_ref[...] * xs
    zz = z_ref[...]
    y = y * (zz * jax.nn.sigmoid(zz))
    for g in range(N_GROUPS):
        gc = slice(g * GROUP_COLS, (g + 1) * GROUP_COLS)
        yg = y[:, gc]
        yg = yg * lax.rsqrt(jnp.mean(yg * yg, axis=-1, keepdims=True) + NORM_EPS)
        y_ref[:, D_POOL + g * GROUP_COLS:D_POOL + (g + 1) * GROUP_COLS] = (
            yg * ng_ref[:, gc]).astype(bf16)

    state_out_ref[...] = state[...]
    utail_ref[...] = uext[0:POOL_HALO, :]
    ctail_ref[...] = cext[0:CONV_HALO, :]


def _mixer(proj, dt_raw, params, state0, uhalo0, chalo0, tm, lead):
    m = proj.shape[0]
    (pool_w, pool_scale, conv_w, conv_b, dt_bias, a_log, dskip_x, norm_g, expand) = params
    full = lambda shape: pl.BlockSpec(shape, lambda i: (0,) * len(shape))
    return pl.pallas_call(
        functools.partial(_mixer_kernel, tm=tm, lead=lead),
        grid=(m // tm,),
        in_specs=[
            pl.BlockSpec((tm, D_POOL), lambda i: (i, 0)),
            pl.BlockSpec((tm, D_SSM), lambda i: (i, 1)),
            pl.BlockSpec((tm, D_SSM), lambda i: (i, 2)),
            pl.BlockSpec((tm, 2 * N_GROUPS * N_STATE), lambda i: (i, 6)),
            pl.BlockSpec((tm, N_HEADS), lambda i: (i, 0)),
            full((len(POOL_WINDOWS), POOL_GROUP, POOL_GROUP)),
            full((1, D_POOL)),
            full((4, D_CONV)),
            full((1, D_CONV)),
            full((1, N_HEADS)),
            full((1, N_HEADS)),
            full((1, D_SSM)),
            full((1, D_SSM)),
            full((N_HEADS, D_SSM)),
            full((N_GROUPS, N_STATE, GROUP_COLS)),
            full((POOL_HALO, D_POOL)),
            full((CONV_HALO, D_CONV)),
        ],
        out_specs=[
            pl.BlockSpec((tm, D_MODEL), lambda i: (i, 0)),
            full((N_GROUPS, N_STATE, GROUP_COLS)),
            full((POOL_HALO, D_POOL)),
            full((CONV_HALO, D_CONV)),
        ],
        out_shape=[
            jax.ShapeDtypeStruct((m, D_MODEL), bf16),
            jax.ShapeDtypeStruct((N_GROUPS, N_STATE, GROUP_COLS), f32),
            jax.ShapeDtypeStruct((POOL_HALO, D_POOL), f32),
            jax.ShapeDtypeStruct((CONV_HALO, D_CONV), f32),
        ],
        scratch_shapes=[
            pltpu.VMEM((N_GROUPS, N_STATE, GROUP_COLS), f32),
            pltpu.VMEM((tm + POOL_HALO, D_POOL), f32),
            pltpu.VMEM((tm + CONV_HALO, D_CONV), f32),
        ],
        compiler_params=_cparams(("arbitrary",)),
        name="mixer",
    )(proj, proj, proj, proj, dt_raw, pool_w, pool_scale, conv_w, conv_b, dt_bias, a_log,
      dskip_x, norm_g, expand, state0, uhalo0, chalo0)


def _outproj_kernel(y_ref, w_ref, x_ref, g2_ref, rw_ref, rb_ref,
                    h2_ref, hn_ref, pos_ref, gate_ref, cnt_ref,
                    cnt, *, tm, tn):
    i = pl.program_id(0)
    nj = D_MODEL // tn

    @pl.when(i == 0)
    def _():
        cnt[...] = jnp.zeros_like(cnt)

    yv = y_ref[...]
    ss = jnp.zeros((tm, 1), f32)
    for q in range(nj):
        cols = slice(q * tn, (q + 1) * tn)
        h = x_ref[:, cols] + jnp.dot(yv, w_ref[:, cols], preferred_element_type=f32)
        h2_ref[:, cols] = h
        ss = ss + jnp.sum(h * h, axis=-1, keepdims=True)

    inv = lax.rsqrt(ss / float(D_MODEL) + NORM_EPS)
    logits = jnp.zeros((N_EXPERTS, tm), f32) + rb_ref[...]
    for q in range(nj):
        cols = slice(q * tn, (q + 1) * tn)
        hn = h2_ref[:, cols] * inv * g2_ref[:, cols]
        hn_ref[:, cols] = hn
        logits = logits + _nt_dot(rw_ref[:, cols], hn.astype(bf16))

    expert = lax.broadcasted_iota(i32, (N_EXPERTS, tm), 0).astype(f32)
    vals = logits
    picked = jnp.zeros((N_EXPERTS, tm), f32)
    tops, idxs = [], []
    for _ in range(TOP_K):
        mx = jnp.max(vals, axis=0, keepdims=True)
        idx = jnp.min(jnp.where(vals == mx, expert, float(N_EXPERTS)), axis=0, keepdims=True)
        hit = expert == idx
        tops.append(mx)
        idxs.append(idx)
        picked = picked + hit.astype(f32)
        vals = jnp.where(hit, -jnp.inf, vals)
    exps = [jnp.exp(t - tops[0]) for t in tops]
    denom = exps[0] + exps[1] + exps[2] + exps[3]

    before = (lax.broadcasted_iota(i32, (tm, tm), 0)
              < lax.broadcasted_iota(i32, (tm, tm), 1)).astype(bf16)
    cum = jnp.dot(picked.astype(bf16), before, preferred_element_type=f32) + cnt[...]
    slot = lax.broadcasted_iota(i32, (GATE_ROWS, tm), 0)
    pos = jnp.zeros((GATE_ROWS, tm), i32)
    gates = jnp.zeros((GATE_ROWS, tm), f32)
    for k in range(TOP_K):
        rank = jnp.sum(jnp.where(expert == idxs[k], cum, 0.0), axis=0, keepdims=True)
        p = idxs[k].astype(i32) * EXPERT_CAP + rank.astype(i32)
        pos = jnp.where(slot == k, p, pos)
        gates = jnp.where(slot == k, exps[k] / denom, gates)
    pos_ref[...] = pos
    gate_ref[...] = gates
    cnt[...] = cnt[...] + jnp.sum(picked, axis=1, keepdims=True)
    cnt_ref[...] = cnt[...]


def _outproj(ymix, w_out, x2d, g2, rw, rb, tm=512, tn=512):
    m = x2d.shape[0]
    resident = dict(pipeline_mode=pl.Buffered(1))
    return pl.pallas_call(
        functools.partial(_outproj_kernel, tm=tm, tn=tn),
        grid=(m // tm,),
        in_specs=[
            pl.BlockSpec((tm, D_MODEL), lambda i: (i, 0)),
            pl.BlockSpec((D_MODEL, D_MODEL), lambda i: (0, 0), **resident),
            pl.BlockSpec((tm, D_MODEL), lambda i: (i, 0)),
            pl.BlockSpec((1, D_MODEL), lambda i: (0, 0)),
            pl.BlockSpec((N_EXPERTS, D_MODEL), lambda i: (0, 0), **resident),
            pl.BlockSpec((N_EXPERTS, 1), lambda i: (0, 0)),
        ],
        out_specs=[
            pl.BlockSpec((tm, D_MODEL), lambda i: (i, 0)),
            pl.BlockSpec((tm, D_MODEL), lambda i: (i, 0)),
            pl.BlockSpec((GATE_ROWS, tm), lambda i: (0, i)),
            pl.BlockSpec((GATE_ROWS, tm), lambda i: (0, i)),
            pl.BlockSpec((N_EXPERTS, 1), lambda i: (0, 0)),
        ],
        out_shape=[
            jax.ShapeDtypeStruct((m, D_MODEL), f32),
            jax.ShapeDtypeStruct((m, D_MODEL), f32),
            jax.ShapeDtypeStruct((GATE_ROWS, m), i32),
            jax.ShapeDtypeStruct((GATE_ROWS, m), f32),
            jax.ShapeDtypeStruct((N_EXPERTS, 1), f32),
        ],
        scratch_shapes=[pltpu.VMEM((N_EXPERTS, 1), f32)],
        compiler_params=_cparams(("arbitrary",)),
        name="outproj_router",
    )(ymix, w_out, x2d, g2, rw, rb)


def _scatter_kernel(pos_ref, cnt_ref, hn_ref, xs_ref, zbuf, sem, zsem, *, tb):
    i = pl.program_id(0)

    @pl.when(i == 0)
    def _():
        zbuf[...] = jnp.zeros_like(zbuf)
        for e in range(N_EXPERTS):
            start = pl.multiple_of(e * EXPERT_CAP + (cnt_ref[e] // 8) * 8, 8)
            pltpu.make_async_copy(zbuf, xs_ref.at[pl.ds(start, ZERO_ROWS)], zsem).start()
        for e in range(N_EXPERTS):
            start = pl.multiple_of(e * EXPERT_CAP + (cnt_ref[e] // 8) * 8, 8)
            pltpu.make_async_copy(zbuf, xs_ref.at[pl.ds(start, ZERO_ROWS)], zsem).wait()

    t0 = pl.multiple_of(i * tb, tb)
    for t in range(tb):
        for k in range(TOP_K):
            p = pos_ref[t0 + (k * SEQ + t)]
            pltpu.make_async_copy(hn_ref.at[pl.ds(t, 1)], xs_ref.at[pl.ds(p, 1)],
                                  sem).start(priority=k % 2)
    done = xs_ref.at[pl.ds(0, tb * TOP_K)]
    pltpu.make_async_copy(done, done, sem).wait()


def _scatter(pos_flat, cnt_i, hn2, tb=256):
    m = hn2.shape[0]
    return pl.pallas_call(
        functools.partial(_scatter_kernel, tb=tb),
        grid_spec=pltpu.PrefetchScalarGridSpec(
            num_scalar_prefetch=2,
            grid=(m // tb,),
            in_specs=[pl.BlockSpec((tb, D_MODEL), lambda i, pos, cnt: (i, 0))],
            out_specs=pl.BlockSpec(memory_space=pl.ANY),
            scratch_shapes=[
                pltpu.VMEM((ZERO_ROWS, D_MODEL), f32),
                pltpu.SemaphoreType.DMA(()),
                pltpu.SemaphoreType.DMA(()),
            ],
        ),
        out_shape=jax.ShapeDtypeStruct((N_EXPERTS * EXPERT_CAP, D_MODEL), f32),
        compiler_params=_cparams(("arbitrary",)),
        name="scatter_rows",
    )(pos_flat, cnt_i, hn2)


def _moe_kernel(we_ref, wq_ref, wn_ref, tot_ref,
                xs_ref, wg_hbm, wu_hbm, wd_hbm, bg_ref, bu_ref, bd_ref,
                o_ref, stage, xb, abuf, ws_a, ws_b, ring, sem, wsem):
    w = pl.program_id(0)
    j = pl.program_id(1)
    e_cur = we_ref[w]
    units = wn_ref[w]
    n = (units + 1) // 2
    wnext = jnp.minimum(w + 1, MOE_WORK - 1)
    n_next = jnp.where(w + 1 < MOE_WORK, (wn_ref[wnext] + 1) // 2, 0)

    def tile_copy(item, r):
        e = we_ref[item]
        if r < 2 * MOE_NJ:
            src, blk = (wg_hbm if r % 2 == 0 else wu_hbm), r // 2
        else:
            src, blk = wd_hbm, r - 2 * MOE_NJ
        return pltpu.make_async_copy(src.at[e, :, pl.ds(blk * MOE_TN, MOE_TN)],
                                     ring.at[r % MOE_RING], wsem.at[r % MOE_RING])

    def tile_take(r, dst):
        tile_copy(w, r).wait()
        dst[...] = ring[r % MOE_RING].astype(bf16)
        r2 = r + MOE_RING
        if r2 < MOE_TILES:
            tile_copy(w, r2).start()
        else:
            @pl.when(w + 1 < tot_ref[0])
            def _():
                tile_copy(wnext, r2 - MOE_TILES).start()

    @pl.when((w == 0) & (j == 0))
    def _():
        for r in range(MOE_RING):
            tile_copy(w, r).start()

    def x_copy(item, s):
        base = we_ref[item] * EXPERT_CAP + wq_ref[item] * MOE_ROWS
        return pltpu.make_async_copy(
            xs_ref.at[pl.ds(pl.multiple_of(base + s * MOE_SUB, MOE_SUB), MOE_SUB)],
            stage.at[s % 2], sem.at[s % 2])

    def x_start(item, count, s):
        @pl.when(s < count)
        def _():
            x_copy(item, s).start()

    def x_finish(item, count, s):
        @pl.when(s < count)
        def _():
            x_copy(item, s).wait()
            xb[pl.ds(s * MOE_SUB, MOE_SUB), :] = stage[s % 2].astype(bf16)

    @pl.when((w == 0) & (j == 0))
    def _():
        x_start(w, n, 0)
        x_start(w, n, 1)
        for s in range(MOE_SUBS):
            x_finish(w, n, s)
            if s + 2 < MOE_SUBS:
                x_start(w, n, s + 2)

    def per_subtile(fn):
        def pair(p, carry):
            r0 = p * (2 * MOE_SUB)
            fn(r0, MOE_SUB)
            fn(r0 + MOE_SUB, MOE_SUB)
            return carry

        pairs = units // 4
        rem = units % 4
        lax.fori_loop(0, pairs, pair, 0)

        @pl.when(rem >= 2)
        def _():
            fn(pairs * (2 * MOE_SUB), MOE_SUB)

        @pl.when(rem % 2 == 1)
        def _():
            fn(pairs * (2 * MOE_SUB) + (rem // 2) * MOE_SUB, MOE_UNIT)

    for jj in range(2 * MOE_NJ):
        @pl.when((j == jj) & (n > 0))
        def _():
            if jj < MOE_NJ:
                tile_take(2 * jj, ws_a)
                tile_take(2 * jj + 1, ws_b)
            else:
                tile_take(MOE_NJ + jj, ws_a)

    @pl.when((j < MOE_NJ) & (n > 0))
    def _():
        bgv = bg_ref[e_cur, pl.ds(j, 1), :]
        buv = bu_ref[e_cur, pl.ds(j, 1), :]

        def gate_up(r0, nrows):
            rows = pl.ds(pl.multiple_of(r0, MOE_UNIT), nrows)
            x = xb[rows, :]
            g = jnp.dot(x, ws_a[...], preferred_element_type=f32) + bgv
            u = jnp.dot(x, ws_b[...], preferred_element_type=f32) + buv
            g = jnp.minimum(g, SWIGLU_LIMIT)
            u = jnp.clip(u, -SWIGLU_LIMIT, SWIGLU_LIMIT)
            act = (u + 1.0) * (g * jax.nn.sigmoid(SWIGLU_ALPHA * g))
            abuf[j, rows, :] = act.astype(bf16)

        per_subtile(gate_up)

    @pl.when((j >= MOE_NJ) & (n > 0))
    def _():
        bdv = bd_ref[e_cur, pl.ds(j - MOE_NJ, 1), :]

        def down(r0, nrows):
            rows = pl.ds(pl.multiple_of(r0, MOE_UNIT), nrows)
            acc = jnp.zeros((nrows, MOE_TN), f32) + bdv
            for q in range(MOE_NJ):
                acc = acc + jnp.dot(abuf[q, rows, :], ws_a[q * MOE_TN:(q + 1) * MOE_TN, :],
                                    preferred_element_type=f32)
            o_ref[rows, :] = acc

        per_subtile(down)

    assert 2 * (MOE_NJ - 1) >= MOE_SUBS
    for k in range(MOE_NJ):
        @pl.when(j == MOE_NJ + k)
        def _():
            if k == 0:
                x_start(wnext, n_next, 0)
                x_start(wnext, n_next, 1)
            else:
                for s in (2 * k - 2, 2 * k - 1):
                    if s < MOE_SUBS:
                        x_finish(wnext, n_next, s)
                        if s + 2 < MOE_SUBS:
                            x_start(wnext, n_next, s + 2)


def _moe(we, wq, wn, tot, xs, w_gate, w_up, w_down, b_gate, b_up, b_down):
    blocks_per_expert = EXPERT_CAP // MOE_ROWS

    bias_spec = pl.BlockSpec((N_EXPERTS, MOE_NJ, MOE_TN), lambda w, j, we, wq, wn, tot: (0, 0, 0),
                             pipeline_mode=pl.Buffered(1))

    def o_map(w, j, we, wq, wn, tot):
        jj = jnp.where(wn[w] > 0, jnp.maximum(j - MOE_NJ, 0), MOE_NJ - 1)
        return (we[w] * blocks_per_expert + wq[w], jj)

    any_space = pl.BlockSpec(memory_space=pl.ANY)
    return pl.pallas_call(
        _moe_kernel,
        grid_spec=pltpu.PrefetchScalarGridSpec(
            num_scalar_prefetch=4,
            grid=(MOE_WORK, 2 * MOE_NJ),
            in_specs=[
                any_space, any_space, any_space, any_space,
                bias_spec, bias_spec, bias_spec,
            ],
            out_specs=pl.BlockSpec((MOE_ROWS, MOE_TN), o_map),
            scratch_shapes=[
                pltpu.VMEM((2, MOE_SUB, D_MODEL), f32),
                pltpu.VMEM((MOE_ROWS, D_MODEL), bf16),
                pltpu.VMEM((MOE_NJ, MOE_ROWS, MOE_TN), bf16),
                pltpu.VMEM((D_MODEL, MOE_TN), bf16),
                pltpu.VMEM((D_MODEL, MOE_TN), bf16),
                pltpu.VMEM((MOE_RING, D_MODEL, MOE_TN), f32),
                pltpu.SemaphoreType.DMA((2,)),
                pltpu.SemaphoreType.DMA((MOE_RING,)),
            ],
        ),
        out_shape=jax.ShapeDtypeStruct((N_EXPERTS * EXPERT_CAP, D_MODEL), f32),
        compiler_params=_cparams(("arbitrary", "arbitrary")),
        name="moe_experts",
    )(we, wq, wn, tot, xs, w_gate, w_up, w_down,
      b_gate.reshape(N_EXPERTS, MOE_NJ, MOE_TN), b_up.reshape(N_EXPERTS, MOE_NJ, MOE_TN),
      b_down.reshape(N_EXPERTS, MOE_NJ, MOE_TN))


def _work_list(cnt_i):
    nq = (cnt_i + MOE_ROWS - 1) // MOE_ROWS
    ends = jnp.cumsum(nq)
    total = ends[-1]
    w = jnp.arange(MOE_WORK, dtype=i32)
    wc = jnp.minimum(w, total - 1)
    e = jnp.minimum(jnp.sum((ends[None, :] <= wc[:, None]).astype(i32), axis=1), N_EXPERTS - 1)
    q = wc - (ends[e] - nq[e])
    rows = jnp.clip(cnt_i[e] - q * MOE_ROWS, 0, MOE_ROWS)
    n = jnp.where(w < total, (rows + MOE_UNIT - 1) // MOE_UNIT, 0)
    return e.astype(i32), q.astype(i32), n.astype(i32), total.astype(i32).reshape(1)


def _combine_kernel(pos_ref, o_ref, gate_ref, h2_ref, fg_ref, out_ref, obuf, sem, *, tm):
    i = pl.program_id(0)

    t0 = pl.multiple_of(i * tm, tm)
    for t in range(tm):
        for k in range(TOP_K):
            p = pos_ref[t0 + (k * SEQ + t)]
            pltpu.make_async_copy(o_ref.at[pl.ds(p, 1)], obuf.at[pl.ds(k * tm + t, 1)],
                                  sem).start(priority=k % 2)
    pltpu.make_async_copy(obuf, obuf, sem).wait()

    gt = gate_ref[...]
    eye = (lax.broadcasted_iota(i32, (tm, tm), 0)
           == lax.broadcasted_iota(i32, (tm, tm), 1)).astype(bf16)
    g1 = gt.astype(bf16)
    r1 = gt - g1.astype(f32)
    g2 = r1.astype(bf16)
    g3 = (r1 - g2.astype(f32)).astype(bf16)
    gates = _nt_dot(eye, g1) + _nt_dot(eye, g2) + _nt_dot(eye, g3)
    y = h2_ref[...]
    for k in range(TOP_K):
        y = y + gates[:, k:k + 1] * obuf[k * tm:(k + 1) * tm, :]
    out = y * lax.rsqrt(jnp.mean(y * y, axis=-1, keepdims=True) + NORM_EPS)
    out_ref[...] = out * fg_ref[...]


def _combine(pos_flat, o, gates, h2, final_g, tm=256):
    m = h2.shape[0]
    return pl.pallas_call(
        functools.partial(_combine_kernel, tm=tm),
        grid_spec=pltpu.PrefetchScalarGridSpec(
            num_scalar_prefetch=1,
            grid=(m // tm,),
            in_specs=[
                pl.BlockSpec(memory_space=pl.ANY),
                pl.BlockSpec((GATE_ROWS, tm), lambda i, pos: (0, i)),
                pl.BlockSpec((tm, D_MODEL), lambda i, pos: (i, 0)),
                pl.BlockSpec((1, D_MODEL), lambda i, pos: (0, 0)),
            ],
            out_specs=pl.BlockSpec((tm, D_MODEL), lambda i, pos: (i, 0)),
            scratch_shapes=[
                pltpu.VMEM((TOP_K * tm, D_MODEL), f32),
                pltpu.SemaphoreType.DMA(()),
            ],
        ),
        out_shape=jax.ShapeDtypeStruct((m, D_MODEL), f32),
        compiler_params=_cparams(("arbitrary",)),
        name="combine",
    )(pos_flat, o, gates, h2, final_g)


def kernel(x, meta_tokens, norm1_g, w_in, pool_w, pool_scale, conv_w, conv_b, dt_bias, a_log,
           d_skip, ssm_norm_g, w_out, norm2_g, router_w, router_b, w_gate, b_gate, w_up, b_up,
           w_down, b_down, final_g):
    assert x.shape == (1, SEQ, D_MODEL)
    x2d = x.reshape(SEQ, D_MODEL)
    lead = CHUNK - N_META
    meta_chunk = jnp.concatenate([jnp.zeros((lead, D_MODEL), f32), meta_tokens.astype(f32)], 0)

    g1 = norm1_g[0].reshape(1, D_MODEL)
    w_main = w_in[0, :, :D_MAIN].astype(bf16)
    w_dt = w_in[0, :, D_MAIN:].astype(bf16)
    head_of_col = jnp.arange(D_SSM, dtype=i32) // HEAD_DIM
    expand = (head_of_col[None, :] == jnp.arange(N_HEADS, dtype=i32)[:, None]).astype(bf16)
    mix_params = (
        pool_w[0], pool_scale[0].reshape(1, D_POOL), conv_w[0], conv_b[0].reshape(1, D_CONV),
        dt_bias[0].reshape(1, N_HEADS), a_log[0].reshape(1, N_HEADS),
        jnp.repeat(d_skip[0], HEAD_DIM).reshape(1, D_SSM), ssm_norm_g[0].reshape(1, D_SSM),
        expand)

    proj_m, dt_m = _inproj(meta_chunk, g1, w_main, w_dt, tm=CHUNK)
    zeros_state = jnp.zeros((N_GROUPS, N_STATE, GROUP_COLS), f32)
    _, state_m, utail_m, ctail_m = _mixer(
        proj_m, dt_m, mix_params, zeros_state, jnp.zeros((POOL_HALO, D_POOL), f32),
        jnp.zeros((CONV_HALO, D_CONV), f32), tm=CHUNK, lead=lead)

    proj, dt_raw = _inproj(x2d, g1, w_main, w_dt, tm=512)
    ymix, _, _, _ = _mixer(proj, dt_raw, mix_params, state_m, utail_m, ctail_m, tm=256, lead=0)

    h2, hn2, pos, gates, cnt = _outproj(
        ymix, w_out[0].astype(bf16), x2d, norm2_g[0].reshape(1, D_MODEL),
        router_w[0].T.astype(bf16),
        router_b[0].reshape(N_EXPERTS, 1))

    pos_flat = pos[:TOP_K].reshape(TOP_K * SEQ)
    cnt_i = cnt.reshape(N_EXPERTS).astype(i32)
    xs = _scatter(pos_flat, cnt_i, hn2)
    we, wq, wn, tot = _work_list(cnt_i)
    o = _moe(we, wq, wn, tot, xs, w_gate[0], w_up[0], w_down[0], b_gate[0], b_up[0], b_down[0])
    out = _combine(pos_flat, o, gates, h2, final_g.reshape(1, D_MODEL))
    return out.reshape(1, SEQ, D_MODEL)
```

```python
import functools

import jax
import jax.numpy as jnp
from jax import lax
from jax.experimental import pallas as pl
from jax.experimental.pallas import tpu as pltpu

f32 = jnp.float32
bf16 = jnp.bfloat16
i32 = jnp.int32

D_MODEL = 2048
SEQ = 8192
CHUNK = 64
N_META = 16
D_POOL = 1024
POOL_WINDOWS = (2, 4, 8, 16)
POOL_GROUP = 256
D_SSM = 1024
HEAD_DIM = 64
N_HEADS = 16
N_GROUPS = 2
HEADS_PER_GROUP = N_HEADS // N_GROUPS
N_STATE = 128
GROUP_COLS = HEADS_PER_GROUP * HEAD_DIM
D_CONV = D_SSM + 2 * N_GROUPS * N_STATE
D_MAIN = D_POOL + D_SSM + D_CONV
N_EXPERTS = 32
TOP_K = 4
D_EXPERT = 2048
SWIGLU_LIMIT = 7.0
SWIGLU_ALPHA = 1.702
NORM_EPS = 1e-5

GATE_ROWS = 8
POOL_HALO = 16
CONV_HALO = 8

MOE_SUB = 256
MOE_UNIT = 128
MOE_ROWS = 1280
MOE_SUBS = MOE_ROWS // MOE_SUB
ZERO_ROWS = MOE_SUB + 8
EXPERT_CAP = -(-(SEQ + ZERO_ROWS) // MOE_ROWS) * MOE_ROWS
MOE_TN = 512
MOE_NJ = D_EXPERT // MOE_TN
MOE_WORK = (SEQ * TOP_K) // MOE_ROWS + N_EXPERTS
MOE_TILES = 3 * MOE_NJ
MOE_RING = 6
assert MOE_TILES % MOE_RING == 0

VMEM_LIMIT = 56 * 1024 * 1024


def _cparams(sem):
    return pltpu.CompilerParams(dimension_semantics=sem, vmem_limit_bytes=VMEM_LIMIT)


PROJ_TN = 512


def _split_dot(q, e):
    hi = q.astype(bf16)
    lo = (q - hi.astype(f32)).astype(bf16)
    return (jnp.dot(hi, e, preferred_element_type=f32)
            + jnp.dot(lo, e, preferred_element_type=f32))


def _nt_dot(a, b):
    return lax.dot_general(a, b, (((1,), (1,)), ((), ())), preferred_element_type=f32)


def _mixer_kernel(x_ref, g1_ref, wmain_ref, wdt_ref,
                  poolw_ref, pscale_ref, convw_ref, convb_ref, dtb_ref, alog_ref,
                  dskip_ref, ng_ref, expand_ref, state0_ref, uhalo0_ref, chalo0_ref,
                  y_ref, state_out_ref, utail_ref, ctail_ref,
                  state, uext, cext, zbuf, projbuf, dtbuf, *, tm, lead):
    i = pl.program_id(0)
    nch = tm // CHUNK
    live = i > 0
    slot_w = i % 2
    slot_r = 1 - slot_w

    @pl.when(i == 0)
    def _():
        state[...] = state0_ref[...]
        uext[0:POOL_HALO, :] = uhalo0_ref[...]
        cext[0:CONV_HALO, :] = chalo0_ref[...]
        projbuf[1] = jnp.zeros((tm, D_MAIN), f32)
        dtbuf[1] = jnp.zeros((tm, N_HEADS), f32)

    uext[POOL_HALO:, :] = projbuf[slot_r, :, 0:D_POOL]
    zbuf[...] = projbuf[slot_r, :, D_POOL:D_POOL + D_SSM]
    cext[CONV_HALO:, :] = projbuf[slot_r, :, D_POOL + D_SSM:D_MAIN]
    dt_in = dtbuf[slot_r]

    xin = x_ref[...]
    xn = xin * lax.rsqrt(jnp.mean(xin * xin, axis=-1, keepdims=True) + NORM_EPS)
    hn = (xn * g1_ref[...]).astype(bf16)
    dtbuf[slot_w] = jnp.dot(hn, wdt_ref[...], preferred_element_type=f32)
    pending = list(range(D_MAIN // PROJ_TN))

    def project(count):
        for _ in range(min(count, len(pending))):
            c = pending.pop(0)
            pc = slice(c * PROJ_TN, (c + 1) * PROJ_TN)
            projbuf[slot_w, :, pc] = jnp.dot(hn, wmain_ref[:, pc], preferred_element_type=f32)

    project(1)

    for g, w in enumerate(POOL_WINDOWS):
        cols = slice(g * POOL_GROUP, (g + 1) * POOL_GROUP)
        ue = uext[:, cols]
        s = ue
        sh = 1
        while sh < w:
            s = s + pltpu.roll(s, sh, axis=0)
            sh *= 2
        d = s[POOL_HALO:, :] / float(w) - ue[POOL_HALO:, :]
        yp = jnp.dot(d.astype(bf16), poolw_ref[g].astype(bf16), preferred_element_type=f32)
        y_ref[:, cols] = (yp * pscale_ref[:, cols]).astype(bf16)

    project(1)
    ce = cext[...]
    cw = convw_ref[...]
    acc = (ce * cw[3:4, :] + pltpu.roll(ce, 1, axis=0) * cw[2:3, :]
           + pltpu.roll(ce, 2, axis=0) * cw[1:2, :] + pltpu.roll(ce, 3, axis=0) * cw[0:1, :])
    acc = acc[CONV_HALO:, :] + convb_ref[...]
    xbc = acc * jax.nn.sigmoid(acc)
    xs = xbc[:, 0:D_SSM]
    bm = xbc[:, D_SSM:D_SSM + N_GROUPS * N_STATE]
    cm = xbc[:, D_SSM + N_GROUPS * N_STATE:]

    uext[0:POOL_HALO, :] = jnp.where(live, uext[tm:tm + POOL_HALO, :], uext[0:POOL_HALO, :])
    cext[0:CONV_HALO, :] = jnp.where(live, cext[tm:tm + CONV_HALO, :], cext[0:CONV_HALO, :])

    project(1)
    v = dt_in + dtb_ref[...]
    dt = jnp.maximum(v, 0.0) + jnp.log1p(jnp.exp(-jnp.abs(v)))
    row = lax.broadcasted_iota(i32, (tm, N_HEADS), 0)
    if lead:
        dt = jnp.where(row >= lead, dt, 0.0)
    a = -jnp.exp(alog_ref[...])
    acs = dt * a
    rin = row % CHUNK
    sh = 1
    while sh < CHUNK:
        acs = acs + jnp.where(rin >= sh, pltpu.roll(acs, sh, axis=0), 0.0)
        sh *= 2
    alast = jnp.concatenate(
        [jnp.broadcast_to(acs[c * CHUNK + CHUNK - 1:c * CHUNK + CHUNK, :], (CHUNK, N_HEADS))
         for c in range(nch)], axis=0)
    dec_st = jnp.exp(alast - acs)
    dec_in = jnp.exp(acs)

    expand = expand_ref[...]
    dt_x = _split_dot(dt, expand)
    dtds_x = _split_dot(dt * dec_st, expand)
    decin_x = _split_dot(dec_in, expand)

    eye_h = (lax.broadcasted_iota(i32, (N_HEADS, N_HEADS), 0)
             == lax.broadcasted_iota(i32, (N_HEADS, N_HEADS), 1)).astype(bf16)
    p1 = acs.astype(bf16)
    r1 = acs - p1.astype(f32)
    p2 = r1.astype(bf16)
    p3 = (r1 - p2.astype(f32)).astype(bf16)
    acs_t = _nt_dot(eye_h, p1) + _nt_dot(eye_h, p2) + _nt_dot(eye_h, p3)

    xdt = (xs * dt_x).astype(bf16)
    xds = (xs * dtds_x).astype(bf16)

    eye_n = (lax.broadcasted_iota(i32, (N_STATE, N_STATE), 0)
             == lax.broadcasted_iota(i32, (N_STATE, N_STATE), 1)).astype(bf16)
    causal = (lax.broadcasted_iota(i32, (CHUNK, CHUNK), 0)
              >= lax.broadcasted_iota(i32, (CHUNK, CHUNK), 1))
    col_head = lax.broadcasted_iota(i32, (CHUNK, GROUP_COLS), 1) // HEAD_DIM

    y_chunks = []
    for c in range(nch):
        project(1)
        r0 = c * CHUNK
        y_groups = []
        for g in range(N_GROUPS):
            gc = slice(g * GROUP_COLS, (g + 1) * GROUP_COLS)
            cg = cm[r0:r0 + CHUNK, g * N_STATE:(g + 1) * N_STATE].astype(bf16)
            bg = bm[r0:r0 + CHUNK, g * N_STATE:(g + 1) * N_STATE].astype(bf16)
            gmat = _nt_dot(cg, bg)
            st = state[g]
            y_off = (jnp.dot(cg, st.astype(bf16), preferred_element_type=f32)
                     * decin_x[r0:r0 + CHUNK, gc])
            scores = []
            for h in range(HEADS_PER_GROUP):
                hh = g * HEADS_PER_GROUP + h
                a_col = acs[r0:r0 + CHUNK, hh:hh + 1]
                a_row = acs_t[hh:hh + 1, r0:r0 + CHUNK]
                lmat = jnp.exp(jnp.where(causal, a_col - a_row, -jnp.inf))
                scores.append((gmat * lmat).astype(bf16))
            sc = jnp.concatenate(scores, axis=0)
            full = jnp.dot(sc, xdt[r0:r0 + CHUNK, gc], preferred_element_type=f32)
            y_diag = jnp.zeros((CHUNK, GROUP_COLS), f32)
            for h in range(HEADS_PER_GROUP):
                y_diag = jnp.where(col_head == h, full[h * CHUNK:(h + 1) * CHUNK, :], y_diag)
            y_groups.append(y_diag + y_off)
            bg_t = _nt_dot(eye_n, bg).astype(bf16)
            upd = jnp.dot(bg_t, xds[r0:r0 + CHUNK, gc], preferred_element_type=f32)
            state[g] = jnp.where(live, st * decin_x[r0 + CHUNK - 1:r0 + CHUNK, gc] + upd, st)
        y_chunks.append(jnp.concatenate(y_groups, axis=1))
    y = jnp.concatenate(y_chunks, axis=0) if nch > 1 else y_chunks[0]

    project(len(pending))
    y = y + dskip_ref[...] * xs
    zz = zbuf[...]
    y = y * (zz * jax.nn.sigmoid(zz))
    for g in range(N_GROUPS):
        gc = slice(g * GROUP_COLS, (g + 1) * GROUP_COLS)
        yg = y[:, gc]
        yg = yg * lax.rsqrt(jnp.mean(yg * yg, axis=-1, keepdims=True) + NORM_EPS)
        y_ref[:, D_POOL + g * GROUP_COLS:D_POOL + (g + 1) * GROUP_COLS] = (
            yg * ng_ref[:, gc]).astype(bf16)

    state_out_ref[...] = state[...]
    utail_ref[...] = uext[0:POOL_HALO, :]
    ctail_ref[...] = cext[0:CONV_HALO, :]


def _mixer(x2d, g1, w_main, w_dt, params, state0, uhalo0, chalo0, tm, lead):
    m = x2d.shape[0]
    nt = m // tm
    (pool_w, pool_scale, conv_w, conv_b, dt_bias, a_log, dskip_x, norm_g, expand) = params
    full = lambda shape: pl.BlockSpec(shape, lambda i: (0,) * len(shape))
    resident = lambda shape: pl.BlockSpec(shape, lambda i: (0,) * len(shape),
                                          pipeline_mode=pl.Buffered(1))
    return pl.pallas_call(
        functools.partial(_mixer_kernel, tm=tm, lead=lead),
        grid=(nt + 1,),
        in_specs=[
            pl.BlockSpec((tm, D_MODEL), lambda i: (jnp.minimum(i, nt - 1), 0)),
            full((1, D_MODEL)),
            resident((D_MODEL, D_MAIN)),
            resident((D_MODEL, N_HEADS)),
            full((len(POOL_WINDOWS), POOL_GROUP, POOL_GROUP)),
            full((1, D_POOL)),
            full((4, D_CONV)),
            full((1, D_CONV)),
            full((1, N_HEADS)),
            full((1, N_HEADS)),
            full((1, D_SSM)),
            full((1, D_SSM)),
            full((N_HEADS, D_SSM)),
            full((N_GROUPS, N_STATE, GROUP_COLS)),
            full((POOL_HALO, D_POOL)),
            full((CONV_HALO, D_CONV)),
        ],
        out_specs=[
            pl.BlockSpec((tm, D_MODEL), lambda i: (jnp.maximum(i - 1, 0), 0)),
            full((N_GROUPS, N_STATE, GROUP_COLS)),
            full((POOL_HALO, D_POOL)),
            full((CONV_HALO, D_CONV)),
        ],
        out_shape=[
            jax.ShapeDtypeStruct((m, D_MODEL), bf16),
            jax.ShapeDtypeStruct((N_GROUPS, N_STATE, GROUP_COLS), f32),
            jax.ShapeDtypeStruct((POOL_HALO, D_POOL), f32),
            jax.ShapeDtypeStruct((CONV_HALO, D_CONV), f32),
        ],
        scratch_shapes=[
            pltpu.VMEM((N_GROUPS, N_STATE, GROUP_COLS), f32),
            pltpu.VMEM((tm + POOL_HALO, D_POOL), f32),
            pltpu.VMEM((tm + CONV_HALO, D_CONV), f32),
            pltpu.VMEM((tm, D_SSM), f32),
            pltpu.VMEM((2, tm, D_MAIN), f32),
            pltpu.VMEM((2, tm, N_HEADS), f32),
        ],
        compiler_params=_cparams(("arbitrary",)),
        name="mixer",
    )(x2d, g1, w_main, w_dt, pool_w, pool_scale, conv_w, conv_b, dt_bias, a_log,
      dskip_x, norm_g, expand, state0, uhalo0, chalo0)


def _outproj_kernel(y_ref, w_ref, x_ref, g2_ref, rw_ref, rb_ref,
                    h2_ref, hn_ref, pos_ref, gate_ref, cnt_ref,
                    cnt, *, tm, tn):
    i = pl.program_id(0)
    nj = D_MODEL // tn

    @pl.when(i == 0)
    def _():
        cnt[...] = jnp.zeros_like(cnt)

    yv = y_ref[...]
    ss = jnp.zeros((tm, 1), f32)
    for q in range(nj):
        cols = slice(q * tn, (q + 1) * tn)
        h = x_ref[:, cols] + jnp.dot(yv, w_ref[:, cols], preferred_element_type=f32)
        h2_ref[:, cols] = h
        ss = ss + jnp.sum(h * h, axis=-1, keepdims=True)

    inv = lax.rsqrt(ss / float(D_MODEL) + NORM_EPS)
    logits = jnp.zeros((N_EXPERTS, tm), f32) + rb_ref[...]
    for q in range(nj):
        cols = slice(q * tn, (q + 1) * tn)
        hn = h2_ref[:, cols] * inv * g2_ref[:, cols]
        hn_ref[:, cols] = hn
        logits = logits + _nt_dot(rw_ref[:, cols], hn.astype(bf16))

    expert = lax.broadcasted_iota(i32, (N_EXPERTS, tm), 0).astype(f32)
    vals = logits
    picked = jnp.zeros((N_EXPERTS, tm), f32)
    tops, idxs = [], []
    for _ in range(TOP_K):
        mx = jnp.max(vals, axis=0, keepdims=True)
        idx = jnp.min(jnp.where(vals == mx, expert, float(N_EXPERTS)), axis=0, keepdims=True)
        hit = expert == idx
        tops.append(mx)
        idxs.append(idx)
        picked = picked + hit.astype(f32)
        vals = jnp.where(hit, -jnp.inf, vals)
    exps = [jnp.exp(t - tops[0]) for t in tops]
    denom = exps[0] + exps[1] + exps[2] + exps[3]

    before = (lax.broadcasted_iota(i32, (tm, tm), 0)
              < lax.broadcasted_iota(i32, (tm, tm), 1)).astype(bf16)
    cum = jnp.dot(picked.astype(bf16), before, preferred_element_type=f32) + cnt[...]
    slot = lax.broadcasted_iota(i32, (GATE_ROWS, tm), 0)
    pos = jnp.zeros((GATE_ROWS, tm), i32)
    gates = jnp.zeros((GATE_ROWS, tm), f32)
    for k in range(TOP_K):
        rank = jnp.sum(jnp.where(expert == idxs[k], cum, 0.0), axis=0, keepdims=True)
        p = idxs[k].astype(i32) * EXPERT_CAP + rank.astype(i32)
        pos = jnp.where(slot == k, p, pos)
        gates = jnp.where(slot == k, exps[k] / denom, gates)
    pos_ref[...] = pos
    gate_ref[...] = gates
    cnt[...] = cnt[...] + jnp.sum(picked, axis=1, keepdims=True)
    cnt_ref[...] = cnt[...]


def _outproj(ymix, w_out, x2d, g2, rw, rb, tm=512, tn=512):
    m = x2d.shape[0]
    resident = dict(pipeline_mode=pl.Buffered(1))
    return pl.pallas_call(
        functools.partial(_outproj_kernel, tm=tm, tn=tn),
        grid=(m // tm,),
        in_specs=[
            pl.BlockSpec((tm, D_MODEL), lambda i: (i, 0)),
            pl.BlockSpec((D_MODEL, D_MODEL), lambda i: (0, 0), **resident),
            pl.BlockSpec((tm, D_MODEL), lambda i: (i, 0)),
            pl.BlockSpec((1, D_MODEL), lambda i: (0, 0)),
            pl.BlockSpec((N_EXPERTS, D_MODEL), lambda i: (0, 0), **resident),
            pl.BlockSpec((N_EXPERTS, 1), lambda i: (0, 0)),
        ],
        out_specs=[
            pl.BlockSpec((tm, D_MODEL), lambda i: (i, 0)),
            pl.BlockSpec((tm, D_MODEL), lambda i: (i, 0)),
            pl.BlockSpec((GATE_ROWS, tm), lambda i: (0, i)),
            pl.BlockSpec((GATE_ROWS, tm), lambda i: (0, i)),
            pl.BlockSpec((N_EXPERTS, 1), lambda i: (0, 0)),
        ],
        out_shape=[
            jax.ShapeDtypeStruct((m, D_MODEL), f32),
            jax.ShapeDtypeStruct((m, D_MODEL), f32),
            jax.ShapeDtypeStruct((GATE_ROWS, m), i32),
            jax.ShapeDtypeStruct((GATE_ROWS, m), f32),
            jax.ShapeDtypeStruct((N_EXPERTS, 1), f32),
        ],
        scratch_shapes=[pltpu.VMEM((N_EXPERTS, 1), f32)],
        compiler_params=_cparams(("arbitrary",)),
        name="outproj_router",
    )(ymix, w_out, x2d, g2, rw, rb)


def _scatter_kernel(pos_ref, cnt_ref, hn_ref, xs_ref, zbuf, sem, zsem, *, tb):
    i = pl.program_id(0)

    @pl.when(i == 0)
    def _():
        zbuf[...] = jnp.zeros_like(zbuf)
        for e in range(N_EXPERTS):
            start = pl.multiple_of(e * EXPERT_CAP + (cnt_ref[e] // 8) * 8, 8)
            pltpu.make_async_copy(zbuf, xs_ref.at[pl.ds(start, ZERO_ROWS)], zsem).start()
        for e in range(N_EXPERTS):
            start = pl.multiple_of(e * EXPERT_CAP + (cnt_ref[e] // 8) * 8, 8)
            pltpu.make_async_copy(zbuf, xs_ref.at[pl.ds(start, ZERO_ROWS)], zsem).wait()

    t0 = pl.multiple_of(i * tb, tb)
    for t in range(tb):
        for k in range(TOP_K):
            p = pos_ref[t0 + (k * SEQ + t)]
            pltpu.make_async_copy(hn_ref.at[pl.ds(t, 1)], xs_ref.at[pl.ds(p, 1)],
                                  sem).start(priority=k % 2)
    done = xs_ref.at[pl.ds(0, tb * TOP_K)]
    pltpu.make_async_copy(done, done, sem).wait()


def _scatter(pos_flat, cnt_i, hn2, tb=256):
    m = hn2.shape[0]
    return pl.pallas_call(
        functools.partial(_scatter_kernel, tb=tb),
        grid_spec=pltpu.PrefetchScalarGridSpec(
            num_scalar_prefetch=2,
            grid=(m // tb,),
            in_specs=[pl.BlockSpec((tb, D_MODEL), lambda i, pos, cnt: (i, 0))],
            out_specs=pl.BlockSpec(memory_space=pl.ANY),
            scratch_shapes=[
                pltpu.VMEM((ZERO_ROWS, D_MODEL), f32),
                pltpu.SemaphoreType.DMA(()),
                pltpu.SemaphoreType.DMA(()),
            ],
        ),
        out_shape=jax.ShapeDtypeStruct((N_EXPERTS * EXPERT_CAP, D_MODEL), f32),
        compiler_params=_cparams(("arbitrary",)),
        name="scatter_rows",
    )(pos_flat, cnt_i, hn2)


def _moe_kernel(we_ref, wq_ref, wn_ref, tot_ref,
                xs_ref, wg_hbm, wu_hbm, wd_hbm, bg_ref, bu_ref, bd_ref,
                o_ref, stage, xb, abuf, ws_a, ws_b, ring, sem, wsem):
    w = pl.program_id(0)
    j = pl.program_id(1)
    e_cur = we_ref[w]
    units = wn_ref[w]
    n = (units + 1) // 2
    wnext = jnp.minimum(w + 1, MOE_WORK - 1)
    n_next = jnp.where(w + 1 < MOE_WORK, (wn_ref[wnext] + 1) // 2, 0)

    def tile_copy(item, r):
        e = we_ref[item]
        if r < 2 * MOE_NJ:
            src, blk = (wg_hbm if r % 2 == 0 else wu_hbm), r // 2
        else:
            src, blk = wd_hbm, r - 2 * MOE_NJ
        return pltpu.make_async_copy(src.at[e, :, pl.ds(blk * MOE_TN, MOE_TN)],
                                     ring.at[r % MOE_RING], wsem.at[r % MOE_RING])

    def tile_take(r, dst):
        tile_copy(w, r).wait()
        dst[...] = ring[r % MOE_RING].astype(bf16)
        r2 = r + MOE_RING
        if r2 < MOE_TILES:
            tile_copy(w, r2).start()
        else:
            @pl.when(w + 1 < tot_ref[0])
            def _():
                tile_copy(wnext, r2 - MOE_TILES).start()

    @pl.when((w == 0) & (j == 0))
    def _():
        for r in range(MOE_RING):
            tile_copy(w, r).start()

    def x_copy(item, s):
        base = we_ref[item] * EXPERT_CAP + wq_ref[item] * MOE_ROWS
        return pltpu.make_async_copy(
            xs_ref.at[pl.ds(pl.multiple_of(base + s * MOE_SUB, MOE_SUB), MOE_SUB)],
            stage.at[s % 2], sem.at[s % 2])

    def x_start(item, count, s):
        @pl.when(s < count)
        def _():
            x_copy(item, s).start()

    def x_finish(item, count, s):
        @pl.when(s < count)
        def _():
            x_copy(item, s).wait()
            xb[pl.ds(s * MOE_SUB, MOE_SUB), :] = stage[s % 2].astype(bf16)

    @pl.when((w == 0) & (j == 0))
    def _():
        x_start(w, n, 0)
        x_start(w, n, 1)
        for s in range(MOE_SUBS):
            x_finish(w, n, s)
            if s + 2 < MOE_SUBS:
                x_start(w, n, s + 2)

    def per_subtile(fn):
        def pair(p, carry):
            r0 = p * (2 * MOE_SUB)
            fn(r0, MOE_SUB)
            fn(r0 + MOE_SUB, MOE_SUB)
            return carry

        pairs = units // 4
        rem = units % 4
        lax.fori_loop(0, pairs, pair, 0)

        @pl.when(rem >= 2)
        def _():
            fn(pairs * (2 * MOE_SUB), MOE_SUB)

        @pl.when(rem % 2 == 1)
        def _():
            fn(pairs * (2 * MOE_SUB) + (rem // 2) * MOE_SUB, MOE_UNIT)

    for jj in range(2 * MOE_NJ):
        @pl.when((j == jj) & (n > 0))
        def _():
            if jj < MOE_NJ:
                tile_take(2 * jj, ws_a)
                tile_take(2 * jj + 1, ws_b)
            else:
                tile_take(MOE_NJ + jj, ws_a)

    @pl.when((j < MOE_NJ) & (n > 0))
    def _():
        bgv = bg_ref[e_cur, pl.ds(j, 1), :]
        buv = bu_ref[e_cur, pl.ds(j, 1), :]

        def gate_up(r0, nrows):
            rows = pl.ds(pl.multiple_of(r0, MOE_UNIT), nrows)
            x = xb[rows, :]
            g = jnp.dot(x, ws_a[...], preferred_element_type=f32) + bgv
            u = jnp.dot(x, ws_b[...], preferred_element_type=f32) + buv
            g = jnp.minimum(g, SWIGLU_LIMIT)
            u = jnp.clip(u, -SWIGLU_LIMIT, SWIGLU_LIMIT)
            act = (u + 1.0) * (g * jax.nn.sigmoid(SWIGLU_ALPHA * g))
            abuf[j, rows, :] = act.astype(bf16)

        per_subtile(gate_up)

    @pl.when((j >= MOE_NJ) & (n > 0))
    def _():
        bdv = bd_ref[e_cur, pl.ds(j - MOE_NJ, 1), :]

        def down(r0, nrows):
            rows = pl.ds(pl.multiple_of(r0, MOE_UNIT), nrows)
            acc = jnp.zeros((nrows, MOE_TN), f32) + bdv
            for q in range(MOE_NJ):
                acc = acc + jnp.dot(abuf[q, rows, :], ws_a[q * MOE_TN:(q + 1) * MOE_TN, :],
                                    preferred_element_type=f32)
            o_ref[rows, :] = acc

        per_subtile(down)

    assert 2 * (MOE_NJ - 1) >= MOE_SUBS
    for k in range(MOE_NJ):
        @pl.when(j == MOE_NJ + k)
        def _():
            if k == 0:
                x_start(wnext, n_next, 0)
                x_start(wnext, n_next, 1)
            else:
                for s in (2 * k - 2, 2 * k - 1):
                    if s < MOE_SUBS:
                        x_finish(wnext, n_next, s)
                        if s + 2 < MOE_SUBS:
                            x_start(wnext, n_next, s + 2)


def _moe(we, wq, wn, tot, xs, w_gate, w_up, w_down, b_gate, b_up, b_down):
    blocks_per_expert = EXPERT_CAP // MOE_ROWS

    bias_spec = pl.BlockSpec((N_EXPERTS, MOE_NJ, MOE_TN), lambda w, j, we, wq, wn, tot: (0, 0, 0),
                             pipeline_mode=pl.Buffered(1))

    def o_map(w, j, we, wq, wn, tot):
        jj = jnp.where(wn[w] > 0, jnp.maximum(j - MOE_NJ, 0), MOE_NJ - 1)
        return (we[w] * blocks_per_expert + wq[w], jj)

    any_space = pl.BlockSpec(memory_space=pl.ANY)
    return pl.pallas_call(
        _moe_kernel,
        grid_spec=pltpu.PrefetchScalarGridSpec(
            num_scalar_prefetch=4,
            grid=(MOE_WORK, 2 * MOE_NJ),
            in_specs=[
                any_space, any_space, any_space, any_space,
                bias_spec, bias_spec, bias_spec,
            ],
            out_specs=pl.BlockSpec((MOE_ROWS, MOE_TN), o_map),
            scratch_shapes=[
                pltpu.VMEM((2, MOE_SUB, D_MODEL), f32),
                pltpu.VMEM((MOE_ROWS, D_MODEL), bf16),
                pltpu.VMEM((MOE_NJ, MOE_ROWS, MOE_TN), bf16),
                pltpu.VMEM((D_MODEL, MOE_TN), bf16),
                pltpu.VMEM((D_MODEL, MOE_TN), bf16),
                pltpu.VMEM((MOE_RING, D_MODEL, MOE_TN), f32),
                pltpu.SemaphoreType.DMA((2,)),
                pltpu.SemaphoreType.DMA((MOE_RING,)),
            ],
        ),
        out_shape=jax.ShapeDtypeStruct((N_EXPERTS * EXPERT_CAP, D_MODEL), f32),
        compiler_params=_cparams(("arbitrary", "arbitrary")),
        name="moe_experts",
    )(we, wq, wn, tot, xs, w_gate, w_up, w_down,
      b_gate.reshape(N_EXPERTS, MOE_NJ, MOE_TN), b_up.reshape(N_EXPERTS, MOE_NJ, MOE_TN),
      b_down.reshape(N_EXPERTS, MOE_NJ, MOE_TN))


def _work_list(cnt_i):
    nq = (cnt_i + MOE_ROWS - 1) // MOE_ROWS
    ends = jnp.cumsum(nq)
    total = ends[-1]
    w = jnp.arange(MOE_WORK, dtype=i32)
    wc = jnp.minimum(w, total - 1)
    e = jnp.minimum(jnp.sum((ends[None, :] <= wc[:, None]).astype(i32), axis=1), N_EXPERTS - 1)
    q = wc - (ends[e] - nq[e])
    rows = jnp.clip(cnt_i[e] - q * MOE_ROWS, 0, MOE_ROWS)
    n = jnp.where(w < total, (rows + MOE_UNIT - 1) // MOE_UNIT, 0)
    return e.astype(i32), q.astype(i32), n.astype(i32), total.astype(i32).reshape(1)


def _combine_kernel(pos_ref, o_ref, gate_ref, h2_ref, fg_ref, out_ref, obuf, sem, *, tm):
    i = pl.program_id(0)

    t0 = pl.multiple_of(i * tm, tm)
    for t in range(tm):
        for k in range(TOP_K):
            p = pos_ref[t0 + (k * SEQ + t)]
            pltpu.make_async_copy(o_ref.at[pl.ds(p, 1)], obuf.at[pl.ds(k * tm + t, 1)],
                                  sem).start(priority=k % 2)
    pltpu.make_async_copy(obuf, obuf, sem).wait()

    gt = gate_ref[...]
    eye = (lax.broadcasted_iota(i32, (tm, tm), 0)
           == lax.broadcasted_iota(i32, (tm, tm), 1)).astype(bf16)
    g1 = gt.astype(bf16)
    r1 = gt - g1.astype(f32)
    g2 = r1.astype(bf16)
    g3 = (r1 - g2.astype(f32)).astype(bf16)
    gates = _nt_dot(eye, g1) + _nt_dot(eye, g2) + _nt_dot(eye, g3)
    y = h2_ref[...]
    for k in range(TOP_K):
        y = y + gates[:, k:k + 1] * obuf[k * tm:(k + 1) * tm, :]
    out = y * lax.rsqrt(jnp.mean(y * y, axis=-1, keepdims=True) + NORM_EPS)
    out_ref[...] = out * fg_ref[...]


def _combine(pos_flat, o, gates, h2, final_g, tm=256):
    m = h2.shape[0]
    return pl.pallas_call(
        functools.partial(_combine_kernel, tm=tm),
        grid_spec=pltpu.PrefetchScalarGridSpec(
            num_scalar_prefetch=1,
            grid=(m // tm,),
            in_specs=[
                pl.BlockSpec(memory_space=pl.ANY),
                pl.BlockSpec((GATE_ROWS, tm), lambda i, pos: (0, i)),
                pl.BlockSpec((tm, D_MODEL), lambda i, pos: (i, 0)),
                pl.BlockSpec((1, D_MODEL), lambda i, pos: (0, 0)),
            ],
            out_specs=pl.BlockSpec((tm, D_MODEL), lambda i, pos: (i, 0)),
            scratch_shapes=[
                pltpu.VMEM((TOP_K * tm, D_MODEL), f32),
                pltpu.SemaphoreType.DMA(()),
            ],
        ),
        out_shape=jax.ShapeDtypeStruct((m, D_MODEL), f32),
        compiler_params=_cparams(("arbitrary",)),
        name="combine",
    )(pos_flat, o, gates, h2, final_g)


def kernel(x, meta_tokens, norm1_g, w_in, pool_w, pool_scale, conv_w, conv_b, dt_bias, a_log,
           d_skip, ssm_norm_g, w_out, norm2_g, router_w, router_b, w_gate, b_gate, w_up, b_up,
           w_down, b_down, final_g):
    assert x.shape == (1, SEQ, D_MODEL)
    x2d = x.reshape(SEQ, D_MODEL)
    lead = CHUNK - N_META
    meta_chunk = jnp.concatenate([jnp.zeros((lead, D_MODEL), f32), meta_tokens.astype(f32)], 0)

    g1 = norm1_g[0].reshape(1, D_MODEL)
    w_main = w_in[0, :, :D_MAIN].astype(bf16)
    w_dt = w_in[0, :, D_MAIN:].astype(bf16)
    head_of_col = jnp.arange(D_SSM, dtype=i32) // HEAD_DIM
    expand = (head_of_col[None, :] == jnp.arange(N_HEADS, dtype=i32)[:, None]).astype(bf16)
    mix_params = (
        pool_w[0], pool_scale[0].reshape(1, D_POOL), conv_w[0], conv_b[0].reshape(1, D_CONV),
        dt_bias[0].reshape(1, N_HEADS), a_log[0].reshape(1, N_HEADS),
        jnp.repeat(d_skip[0], HEAD_DIM).reshape(1, D_SSM), ssm_norm_g[0].reshape(1, D_SSM),
        expand)

    zeros_state = jnp.zeros((N_GROUPS, N_STATE, GROUP_COLS), f32)
    _, state_m, utail_m, ctail_m = _mixer(
        meta_chunk, g1, w_main, w_dt, mix_params, zeros_state,
        jnp.zeros((POOL_HALO, D_POOL), f32), jnp.zeros((CONV_HALO, D_CONV), f32),
        tm=CHUNK, lead=lead)

    ymix, _, _, _ = _mixer(x2d, g1, w_main, w_dt, mix_params, state_m, utail_m, ctail_m,
                           tm=256, lead=0)

    h2, hn2, pos, gates, cnt = _outproj(
        ymix, w_out[0].astype(bf16), x2d, norm2_g[0].reshape(1, D_MODEL),
        router_w[0].T.astype(bf16),
        router_b[0].reshape(N_EXPERTS, 1))

    pos_flat = pos[:TOP_K].reshape(TOP_K * SEQ)
    cnt_i = cnt.reshape(N_EXPERTS).astype(i32)
    xs = _scatter(pos_flat, cnt_i, hn2)
    we, wq, wn, tot = _work_list(cnt_i)
    o = _moe(we, wq, wn, tot, xs, w_gate[0], w_up[0], w_down[0], b_gate[0], b_up[0], b_down[0])
    out = _combine(pos_flat, o, gates, h2, final_g.reshape(1, D_MODEL))
    return out.reshape(1, SEQ, D_MODEL)
```

```python
import functools

import jax
import jax.numpy as jnp
from jax import lax
from jax.experimental import pallas as pl
from jax.experimental.pallas import tpu as pltpu

f32 = jnp.float32
bf16 = jnp.bfloat16
i32 = jnp.int32

D_MODEL = 2048
SEQ = 8192
CHUNK = 64
N_META = 16
D_POOL = 1024
POOL_WINDOWS = (2, 4, 8, 16)
POOL_GROUP = 256
D_SSM = 1024
HEAD_DIM = 64
N_HEADS = 16
N_GROUPS = 2
HEADS_PER_GROUP = N_HEADS // N_GROUPS
N_STATE = 128
GROUP_COLS = HEADS_PER_GROUP * HEAD_DIM
D_CONV = D_SSM + 2 * N_GROUPS * N_STATE
D_MAIN = D_POOL + D_SSM + D_CONV
N_EXPERTS = 32
TOP_K = 4
D_EXPERT = 2048
SWIGLU_LIMIT = 7.0
SWIGLU_ALPHA = 1.702
NORM_EPS = 1e-5

GATE_ROWS = 8
POOL_HALO = 16
CONV_HALO = 8

MOE_SUB = 256
MOE_UNIT = 128
MOE_ROWS = 1280
MOE_SUBS = MOE_ROWS // MOE_SUB
ZERO_ROWS = MOE_SUB + 8
EXPERT_CAP = -(-(SEQ + ZERO_ROWS) // MOE_ROWS) * MOE_ROWS
MOE_TN = 512
MOE_NJ = D_EXPERT // MOE_TN
MOE_WORK = (SEQ * TOP_K) // MOE_ROWS + N_EXPERTS
MOE_TILES = 3 * MOE_NJ
MOE_RING = 6
assert MOE_TILES % MOE_RING == 0

VMEM_LIMIT = 56 * 1024 * 1024


def _cparams(sem):
    return pltpu.CompilerParams(dimension_semantics=sem, vmem_limit_bytes=VMEM_LIMIT)


PROJ_TN = 512


def _split_dot(q, e):
    hi = q.astype(bf16)
    lo = (q - hi.astype(f32)).astype(bf16)
    return (jnp.dot(hi, e, preferred_element_type=f32)
            + jnp.dot(lo, e, preferred_element_type=f32))


def _nt_dot(a, b):
    return lax.dot_general(a, b, (((1,), (1,)), ((), ())), preferred_element_type=f32)


def _inproj_kernel(x_ref, g_ref, w_ref, wdt_ref, proj_ref, dt_ref):
    x = x_ref[...]
    y = x * lax.rsqrt(jnp.mean(x * x, axis=-1, keepdims=True) + NORM_EPS)
    hn = (y * g_ref[...]).astype(bf16)
    dt_ref[...] = jnp.dot(hn, wdt_ref[...], preferred_element_type=f32)
    for c in range(D_MAIN // PROJ_TN):
        cols = slice(c * PROJ_TN, (c + 1) * PROJ_TN)
        proj_ref[:, cols] = jnp.dot(hn, w_ref[:, cols], preferred_element_type=f32)


def _inproj(x2d, g, w_main, w_dt, tm):
    m = x2d.shape[0]
    resident = dict(pipeline_mode=pl.Buffered(1))
    return pl.pallas_call(
        _inproj_kernel,
        grid=(m // tm,),
        in_specs=[
            pl.BlockSpec((tm, D_MODEL), lambda i: (i, 0)),
            pl.BlockSpec((1, D_MODEL), lambda i: (0, 0)),
            pl.BlockSpec((D_MODEL, D_MAIN), lambda i: (0, 0), **resident),
            pl.BlockSpec((D_MODEL, N_HEADS), lambda i: (0, 0), **resident),
        ],
        out_specs=[
            pl.BlockSpec((tm, D_MAIN), lambda i: (i, 0)),
            pl.BlockSpec((tm, N_HEADS), lambda i: (i, 0)),
        ],
        out_shape=[
            jax.ShapeDtypeStruct((m, D_MAIN), f32),
            jax.ShapeDtypeStruct((m, N_HEADS), f32),
        ],
        compiler_params=_cparams(("arbitrary",)),
        name="inproj",
    )(x2d, g, w_main, w_dt)


def _mixer_kernel(u_ref, z_ref, xs_ref, bc_ref, dt_ref,
                  poolw_ref, pscale_ref, convw_ref, convb_ref, dtb_ref, alog_ref,
                  dskip_ref, ng_ref, expand_ref, state0_ref, uhalo0_ref, chalo0_ref,
                  y_ref, state_out_ref, utail_ref, ctail_ref,
                  state, uext, cext, *, tm, lead):
    i = pl.program_id(0)
    nch = tm // CHUNK

    @pl.when(i == 0)
    def _():
        state[...] = state0_ref[...]
        uext[0:POOL_HALO, :] = uhalo0_ref[...]
        cext[0:CONV_HALO, :] = chalo0_ref[...]

    uext[POOL_HALO:, :] = u_ref[...]
    cext[CONV_HALO:, 0:D_SSM] = xs_ref[...]
    cext[CONV_HALO:, D_SSM:] = bc_ref[...]

    for g, w in enumerate(POOL_WINDOWS):
        cols = slice(g * POOL_GROUP, (g + 1) * POOL_GROUP)
        ue = uext[:, cols]
        s = ue
        sh = 1
        while sh < w:
            s = s + pltpu.roll(s, sh, axis=0)
            sh *= 2
        d = s[POOL_HALO:, :] / float(w) - ue[POOL_HALO:, :]
        yp = jnp.dot(d.astype(bf16), poolw_ref[g].astype(bf16), preferred_element_type=f32)
        y_ref[:, cols] = (yp * pscale_ref[:, cols]).astype(bf16)

    ce = cext[...]
    cw = convw_ref[...]
    acc = (ce * cw[3:4, :] + pltpu.roll(ce, 1, axis=0) * cw[2:3, :]
           + pltpu.roll(ce, 2, axis=0) * cw[1:2, :] + pltpu.roll(ce, 3, axis=0) * cw[0:1, :])
    acc = acc[CONV_HALO:, :] + convb_ref[...]
    xbc = acc * jax.nn.sigmoid(acc)
    xs = xbc[:, 0:D_SSM]
    bm = xbc[:, D_SSM:D_SSM + N_GROUPS * N_STATE]
    cm = xbc[:, D_SSM + N_GROUPS * N_STATE:]

    uext[0:POOL_HALO, :] = uext[tm:tm + POOL_HALO, :]
    cext[0:CONV_HALO, :] = cext[tm:tm + CONV_HALO, :]

    v = dt_ref[...] + dtb_ref[...]
    dt = jnp.maximum(v, 0.0) + jnp.log1p(jnp.exp(-jnp.abs(v)))
    row = lax.broadcasted_iota(i32, (tm, N_HEADS), 0)
    if lead:
        dt = jnp.where(row >= lead, dt, 0.0)
    a = -jnp.exp(alog_ref[...])
    acs = dt * a
    rin = row % CHUNK
    sh = 1
    while sh < CHUNK:
        acs = acs + jnp.where(rin >= sh, pltpu.roll(acs, sh, axis=0), 0.0)
        sh *= 2
    alast = jnp.concatenate(
        [jnp.broadcast_to(acs[c * CHUNK + CHUNK - 1:c * CHUNK + CHUNK, :], (CHUNK, N_HEADS))
         for c in range(nch)], axis=0)
    dec_st = jnp.exp(alast - acs)
    dec_in = jnp.exp(acs)

    expand = expand_ref[...]
    dt_x = _split_dot(dt, expand)
    dtds_x = _split_dot(dt * dec_st, expand)
    decin_x = _split_dot(dec_in, expand)

    eye_h = (lax.broadcasted_iota(i32, (N_HEADS, N_HEADS), 0)
             == lax.broadcasted_iota(i32, (N_HEADS, N_HEADS), 1)).astype(bf16)
    p1 = acs.astype(bf16)
    r1 = acs - p1.astype(f32)
    p2 = r1.astype(bf16)
    p3 = (r1 - p2.astype(f32)).astype(bf16)
    acs_t = _nt_dot(eye_h, p1) + _nt_dot(eye_h, p2) + _nt_dot(eye_h, p3)

    xdt = (xs * dt_x).astype(bf16)
    xds = (xs * dtds_x).astype(bf16)

    eye_n = (lax.broadcasted_iota(i32, (N_STATE, N_STATE), 0)
             == lax.broadcasted_iota(i32, (N_STATE, N_STATE), 1)).astype(bf16)
    causal = (lax.broadcasted_iota(i32, (CHUNK, CHUNK), 0)
              >= lax.broadcasted_iota(i32, (CHUNK, CHUNK), 1))
    col_head = lax.broadcasted_iota(i32, (CHUNK, GROUP_COLS), 1) // HEAD_DIM

    y_chunks = []
    for c in range(nch):
        r0 = c * CHUNK
        y_groups = []
        for g in range(N_GROUPS):
            gc = slice(g * GROUP_COLS, (g + 1) * GROUP_COLS)
            cg = cm[r0:r0 + CHUNK, g * N_STATE:(g + 1) * N_STATE].astype(bf16)
            bg = bm[r0:r0 + CHUNK, g * N_STATE:(g + 1) * N_STATE].astype(bf16)
            gmat = _nt_dot(cg, bg)
            st = state[g]
            y_off = (jnp.dot(cg, st.astype(bf16), preferred_element_type=f32)
                     * decin_x[r0:r0 + CHUNK, gc])
            scores = []
            for h in range(HEADS_PER_GROUP):
                hh = g * HEADS_PER_GROUP + h
                a_col = acs[r0:r0 + CHUNK, hh:hh + 1]
                a_row = acs_t[hh:hh + 1, r0:r0 + CHUNK]
                lmat = jnp.exp(jnp.where(causal, a_col - a_row, -jnp.inf))
                scores.append((gmat * lmat).astype(bf16))
            sc = jnp.concatenate(scores, axis=0)
            full = jnp.dot(sc, xdt[r0:r0 + CHUNK, gc], preferred_element_type=f32)
            y_diag = jnp.zeros((CHUNK, GROUP_COLS), f32)
            for h in range(HEADS_PER_GROUP):
                y_diag = jnp.where(col_head == h, full[h * CHUNK:(h + 1) * CHUNK, :], y_diag)
            y_groups.append(y_diag + y_off)
            bg_t = _nt_dot(eye_n, bg).astype(bf16)
            upd = jnp.dot(bg_t, xds[r0:r0 + CHUNK, gc], preferred_element_type=f32)
            state[g] = st * decin_x[r0 + CHUNK - 1:r0 + CHUNK, gc] + upd
        y_chunks.append(jnp.concatenate(y_groups, axis=1))
    y = jnp.concatenate(y_chunks, axis=0) if nch > 1 else y_chunks[0]

    y = y + dskip_ref[...] * xs
    zz = z_ref[...]
    y = y * (zz * jax.nn.sigmoid(zz))
    for g in range(N_GROUPS):
        gc = slice(g * GROUP_COLS, (g + 1) * GROUP_COLS)
        yg = y[:, gc]
        yg = yg * lax.rsqrt(jnp.mean(yg * yg, axis=-1, keepdims=True) + NORM_EPS)
        y_ref[:, D_POOL + g * GROUP_COLS:D_POOL + (g + 1) * GROUP_COLS] = (
            yg * ng_ref[:, gc]).astype(bf16)

    state_out_ref[...] = state[...]
    utail_ref[...] = uext[0:POOL_HALO, :]
    ctail_ref[...] = cext[0:CONV_HALO, :]


def _mixer(proj, dt_raw, params, state0, uhalo0, chalo0, tm, lead):
    m = proj.shape[0]
    (pool_w, pool_scale, conv_w, conv_b, dt_bias, a_log, dskip_x, norm_g, expand) = params
    full = lambda shape: pl.BlockSpec(shape, lambda i: (0,) * len(shape))
    return pl.pallas_call(
        functools.partial(_mixer_kernel, tm=tm, lead=lead),
        grid=(m // tm,),
        in_specs=[
            pl.BlockSpec((tm, D_POOL), lambda i: (i, 0)),
            pl.BlockSpec((tm, D_SSM), lambda i: (i, 1)),
            pl.BlockSpec((tm, D_SSM), lambda i: (i, 2)),
            pl.BlockSpec((tm, 2 * N_GROUPS * N_STATE), lambda i: (i, 6)),
            pl.BlockSpec((tm, N_HEADS), lambda i: (i, 0)),
            full((len(POOL_WINDOWS), POOL_GROUP, POOL_GROUP)),
            full((1, D_POOL)),
            full((4, D_CONV)),
            full((1, D_CONV)),
            full((1, N_HEADS)),
            full((1, N_HEADS)),
            full((1, D_SSM)),
            full((1, D_SSM)),
            full((N_HEADS, D_SSM)),
            full((N_GROUPS, N_STATE, GROUP_COLS)),
            full((POOL_HALO, D_POOL)),
            full((CONV_HALO, D_CONV)),
        ],
        out_specs=[
            pl.BlockSpec((tm, D_MODEL), lambda i: (i, 0)),
            full((N_GROUPS, N_STATE, GROUP_COLS)),
            full((POOL_HALO, D_POOL)),
            full((CONV_HALO, D_CONV)),
        ],
        out_shape=[
            jax.ShapeDtypeStruct((m, D_MODEL), bf16),
            jax.ShapeDtypeStruct((N_GROUPS, N_STATE, GROUP_COLS), f32),
            jax.ShapeDtypeStruct((POOL_HALO, D_POOL), f32),
            jax.ShapeDtypeStruct((CONV_HALO, D_CONV), f32),
        ],
        scratch_shapes=[
            pltpu.VMEM((N_GROUPS, N_STATE, GROUP_COLS), f32),
            pltpu.VMEM((tm + POOL_HALO, D_POOL), f32),
            pltpu.VMEM((tm + CONV_HALO, D_CONV), f32),
        ],
        compiler_params=_cparams(("arbitrary",)),
        name="mixer",
    )(proj, proj, proj, proj, dt_raw, pool_w, pool_scale, conv_w, conv_b, dt_bias, a_log,
      dskip_x, norm_g, expand, state0, uhalo0, chalo0)


def _outproj_kernel(y_ref, w_ref, x_ref, g2_ref, rw_ref, rb_ref,
                    h2_ref, hn_ref, pos_ref, gate_ref, cnt_ref,
                    cnt, *, tm, tn):
    i = pl.program_id(0)
    nj = D_MODEL // tn

    @pl.when(i == 0)
    def _():
        cnt[...] = jnp.zeros_like(cnt)

    yv = y_ref[...]
    ss = jnp.zeros((tm, 1), f32)
    for q in range(nj):
        cols = slice(q * tn, (q + 1) * tn)
        h = x_ref[:, cols] + jnp.dot(yv, w_ref[:, cols], preferred_element_type=f32)
        h2_ref[:, cols] = h
        ss = ss + jnp.sum(h * h, axis=-1, keepdims=True)

    inv = lax.rsqrt(ss / float(D_MODEL) + NORM_EPS)
    logits = jnp.zeros((N_EXPERTS, tm), f32) + rb_ref[...]
    for q in range(nj):
        cols = slice(q * tn, (q + 1) * tn)
        hn = h2_ref[:, cols] * inv * g2_ref[:, cols]
        hn_ref[:, cols] = hn
        logits = logits + _nt_dot(rw_ref[:, cols], hn.astype(bf16))

    expert = lax.broadcasted_iota(i32, (N_EXPERTS, tm), 0).astype(f32)
    vals = logits
    picked = jnp.zeros((N_EXPERTS, tm), f32)
    tops, idxs = [], []
    for _ in range(TOP_K):
        mx = jnp.max(vals, axis=0, keepdims=True)
        idx = jnp.min(jnp.where(vals == mx, expert, float(N_EXPERTS)), axis=0, keepdims=True)
        hit = expert == idx
        tops.append(mx)
        idxs.append(idx)
        picked = picked + hit.astype(f32)
        vals = jnp.where(hit, -jnp.inf, vals)
    exps = [jnp.exp(t - tops[0]) for t in tops]
    denom = exps[0] + exps[1] + exps[2] + exps[3]

    before = (lax.broadcasted_iota(i32, (tm, tm), 0)
              < lax.broadcasted_iota(i32, (tm, tm), 1)).astype(bf16)
    cum = jnp.dot(picked.astype(bf16), before, preferred_element_type=f32) + cnt[...]
    slot = lax.broadcasted_iota(i32, (GATE_ROWS, tm), 0)
    pos = jnp.zeros((GATE_ROWS, tm), i32)
    gates = jnp.zeros((GATE_ROWS, tm), f32)
    for k in range(TOP_K):
        rank = jnp.sum(jnp.where(expert == idxs[k], cum, 0.0), axis=0, keepdims=True)
        p = idxs[k].astype(i32) * EXPERT_CAP + rank.astype(i32)
        pos = jnp.where(slot == k, p, pos)
        gates = jnp.where(slot == k, exps[k] / denom, gates)
    pos_ref[...] = pos
    gate_ref[...] = gates
    cnt[...] = cnt[...] + jnp.sum(picked, axis=1, keepdims=True)
    cnt_ref[...] = cnt[...]


def _outproj(ymix, w_out, x2d, g2, rw, rb, tm=512, tn=512):
    m = x2d.shape[0]
    resident = dict(pipeline_mode=pl.Buffered(1))
    return pl.pallas_call(
        functools.partial(_outproj_kernel, tm=tm, tn=tn),
        grid=(m // tm,),
        in_specs=[
            pl.BlockSpec((tm, D_MODEL), lambda i: (i, 0)),
            pl.BlockSpec((D_MODEL, D_MODEL), lambda i: (0, 0), **resident),
            pl.BlockSpec((tm, D_MODEL), lambda i: (i, 0)),
            pl.BlockSpec((1, D_MODEL), lambda i: (0, 0)),
            pl.BlockSpec((N_EXPERTS, D_MODEL), lambda i: (0, 0), **resident),
            pl.BlockSpec((N_EXPERTS, 1), lambda i: (0, 0)),
        ],
        out_specs=[
            pl.BlockSpec((tm, D_MODEL), lambda i: (i, 0)),
            pl.BlockSpec((tm, D_MODEL), lambda i: (i, 0)),
            pl.BlockSpec((GATE_ROWS, tm), lambda i: (0, i)),
            pl.BlockSpec((GATE_ROWS, tm), lambda i: (0, i)),
            pl.BlockSpec((N_EXPERTS, 1), lambda i: (0, 0)),
        ],
        out_shape=[
            jax.ShapeDtypeStruct((m, D_MODEL), f32),
            jax.ShapeDtypeStruct((m, D_MODEL), f32),
            jax.ShapeDtypeStruct((GATE_ROWS, m), i32),
            jax.ShapeDtypeStruct((GATE_ROWS, m), f32),
            jax.ShapeDtypeStruct((N_EXPERTS, 1), f32),
        ],
        scratch_shapes=[pltpu.VMEM((N_EXPERTS, 1), f32)],
        compiler_params=_cparams(("arbitrary",)),
        name="outproj_router",
    )(ymix, w_out, x2d, g2, rw, rb)


def _scatter_kernel(pos_ref, cnt_ref, hn_ref, xs_ref, zbuf, sem, zsem, *, tb):
    i = pl.program_id(0)

    @pl.when(i == 0)
    def _():
        zbuf[...] = jnp.zeros_like(zbuf)
        for e in range(N_EXPERTS):
            start = pl.multiple_of(e * EXPERT_CAP + (cnt_ref[e] // 8) * 8, 8)
            pltpu.make_async_copy(zbuf, xs_ref.at[pl.ds(start, ZERO_ROWS)], zsem).start()
        for e in range(N_EXPERTS):
            start = pl.multiple_of(e * EXPERT_CAP + (cnt_ref[e] // 8) * 8, 8)
            pltpu.make_async_copy(zbuf, xs_ref.at[pl.ds(start, ZERO_ROWS)], zsem).wait()

    t0 = pl.multiple_of(i * tb, tb)
    for t in range(tb):
        for k in range(TOP_K):
            p = pos_ref[t0 + (k * SEQ + t)]
            pltpu.make_async_copy(hn_ref.at[pl.ds(t, 1)], xs_ref.at[pl.ds(p, 1)],
                                  sem).start(priority=k % 2)
    done = xs_ref.at[pl.ds(0, tb * TOP_K)]
    pltpu.make_async_copy(done, done, sem).wait()


def _scatter(pos_flat, cnt_i, hn2, tb=256):
    m = hn2.shape[0]
    return pl.pallas_call(
        functools.partial(_scatter_kernel, tb=tb),
        grid_spec=pltpu.PrefetchScalarGridSpec(
            num_scalar_prefetch=2,
            grid=(m // tb,),
            in_specs=[pl.BlockSpec((tb, D_MODEL), lambda i, pos, cnt: (i, 0))],
            out_specs=pl.BlockSpec(memory_space=pl.ANY),
            scratch_shapes=[
                pltpu.VMEM((ZERO_ROWS, D_MODEL), f32),
                pltpu.SemaphoreType.DMA(()),
                pltpu.SemaphoreType.DMA(()),
            ],
        ),
        out_shape=jax.ShapeDtypeStruct((N_EXPERTS * EXPERT_CAP, D_MODEL), f32),
        compiler_params=_cparams(("arbitrary",)),
        name="scatter_rows",
    )(pos_flat, cnt_i, hn2)


def _pack_bf16_pair(lo, hi):
    lo_bits = lax.bitcast_convert_type(lo.astype(bf16).astype(f32), jnp.uint32)
    hi_bits = lax.bitcast_convert_type(hi.astype(bf16).astype(f32), jnp.uint32)
    return (lo_bits >> 16) | (hi_bits & jnp.uint32(0xFFFF0000))


def _unpack_bf16_pair(word):
    lo = lax.bitcast_convert_type(word << 16, f32)
    hi = lax.bitcast_convert_type(word & jnp.uint32(0xFFFF0000), f32)
    return lo, hi


def _moe_kernel(we_ref, wq_ref, wn_ref, tot_ref,
                xs_ref, wg_hbm, wu_hbm, wd_hbm, bg_ref, bu_ref, bd_ref,
                o_ref, stage, xb, abuf, ws_a, ws_b, ring, sem, wsem):
    w = pl.program_id(0)
    j = pl.program_id(1)
    e_cur = we_ref[w]
    units = wn_ref[w]
    n = (units + 1) // 2
    wnext = jnp.minimum(w + 1, MOE_WORK - 1)
    n_next = jnp.where(w + 1 < MOE_WORK, (wn_ref[wnext] + 1) // 2, 0)

    def tile_copy(item, r):
        e = we_ref[item]
        if r < 2 * MOE_NJ:
            src, blk = (wg_hbm if r % 2 == 0 else wu_hbm), r // 2
        else:
            src, blk = wd_hbm, r - 2 * MOE_NJ
        return pltpu.make_async_copy(src.at[e, :, pl.ds(blk * MOE_TN, MOE_TN)],
                                     ring.at[r % MOE_RING], wsem.at[r % MOE_RING])

    def tile_take(r, dst):
        tile_copy(w, r).wait()
        dst[...] = ring[r % MOE_RING].astype(bf16)
        r2 = r + MOE_RING
        if r2 < MOE_TILES:
            tile_copy(w, r2).start()
        else:
            @pl.when(w + 1 < tot_ref[0])
            def _():
                tile_copy(wnext, r2 - MOE_TILES).start()

    @pl.when((w == 0) & (j == 0))
    def _():
        for r in range(MOE_RING):
            tile_copy(w, r).start()

    def x_copy(item, s):
        base = we_ref[item] * EXPERT_CAP + wq_ref[item] * MOE_ROWS
        return pltpu.make_async_copy(
            xs_ref.at[pl.ds(pl.multiple_of(base + s * MOE_SUB, MOE_SUB), MOE_SUB)],
            stage.at[s % 2], sem.at[s % 2])

    def x_start(item, count, s):
        @pl.when(s < count)
        def _():
            x_copy(item, s).start()

    def x_finish(item, count, s):
        @pl.when(s < count)
        def _():
            x_copy(item, s).wait()
            xb[pl.ds(s * MOE_SUB, MOE_SUB), :] = stage[s % 2].astype(bf16)

    @pl.when((w == 0) & (j == 0))
    def _():
        x_start(w, n, 0)
        x_start(w, n, 1)
        for s in range(MOE_SUBS):
            x_finish(w, n, s)
            if s + 2 < MOE_SUBS:
                x_start(w, n, s + 2)

    def per_subtile(fn):
        def pair(p, carry):
            r0 = p * (2 * MOE_SUB)
            fn(r0, MOE_SUB)
            fn(r0 + MOE_SUB, MOE_SUB)
            return carry

        pairs = units // 4
        rem = units % 4
        lax.fori_loop(0, pairs, pair, 0)

        @pl.when(rem >= 2)
        def _():
            fn(pairs * (2 * MOE_SUB), MOE_SUB)

        @pl.when(rem % 2 == 1)
        def _():
            fn(pairs * (2 * MOE_SUB) + (rem // 2) * MOE_SUB, MOE_UNIT)

    for jj in range(2 * MOE_NJ):
        @pl.when((j == jj) & (n > 0))
        def _():
            if jj < MOE_NJ:
                tile_take(2 * jj, ws_a)
                tile_take(2 * jj + 1, ws_b)
            else:
                tile_take(MOE_NJ + jj, ws_a)

    @pl.when((j < MOE_NJ) & (n > 0))
    def _():
        bgv = bg_ref[e_cur, pl.ds(j, 1), :]
        buv = bu_ref[e_cur, pl.ds(j, 1), :]

        def gate_up(r0, nrows):
            rows = pl.ds(pl.multiple_of(r0, MOE_UNIT), nrows)
            x = xb[rows, :]
            g = jnp.dot(x, ws_a[...], preferred_element_type=f32) + bgv
            u = jnp.dot(x, ws_b[...], preferred_element_type=f32) + buv
            g = jnp.minimum(g, SWIGLU_LIMIT)
            u = jnp.clip(u, -SWIGLU_LIMIT, SWIGLU_LIMIT)
            act = (u + 1.0) * (g * jax.nn.sigmoid(SWIGLU_ALPHA * g))
            abuf[j, rows, :] = act.astype(bf16)

        per_subtile(gate_up)

    @pl.when((j >= MOE_NJ) & (n > 0))
    def _():
        bdv = bd_ref[e_cur, pl.ds(j - MOE_NJ, 1), :]

        def down(r0, nrows):
            rows = pl.ds(pl.multiple_of(r0, MOE_UNIT), nrows)
            acc = jnp.zeros((nrows, MOE_TN), f32) + bdv
            for q in range(MOE_NJ):
                acc = acc + jnp.dot(abuf[q, rows, :], ws_a[q * MOE_TN:(q + 1) * MOE_TN, :],
                                    preferred_element_type=f32)
            o_ref[rows, :] = _pack_bf16_pair(acc[:, :MOE_TN // 2], acc[:, MOE_TN // 2:])

        per_subtile(down)

    assert 2 * (MOE_NJ - 1) >= MOE_SUBS
    for k in range(MOE_NJ):
        @pl.when(j == MOE_NJ + k)
        def _():
            if k == 0:
                x_start(wnext, n_next, 0)
                x_start(wnext, n_next, 1)
            else:
                for s in (2 * k - 2, 2 * k - 1):
                    if s < MOE_SUBS:
                        x_finish(wnext, n_next, s)
                        if s + 2 < MOE_SUBS:
                            x_start(wnext, n_next, s + 2)


def _moe(we, wq, wn, tot, xs, w_gate, w_up, w_down, b_gate, b_up, b_down):
    blocks_per_expert = EXPERT_CAP // MOE_ROWS

    bias_spec = pl.BlockSpec((N_EXPERTS, MOE_NJ, MOE_TN), lambda w, j, we, wq, wn, tot: (0, 0, 0),
                             pipeline_mode=pl.Buffered(1))

    def o_map(w, j, we, wq, wn, tot):
        jj = jnp.where(wn[w] > 0, jnp.maximum(j - MOE_NJ, 0), MOE_NJ - 1)
        return (we[w] * blocks_per_expert + wq[w], jj)

    any_space = pl.BlockSpec(memory_space=pl.ANY)
    return pl.pallas_call(
        _moe_kernel,
        grid_spec=pltpu.PrefetchScalarGridSpec(
            num_scalar_prefetch=4,
            grid=(MOE_WORK, 2 * MOE_NJ),
            in_specs=[
                any_space, any_space, any_space, any_space,
                bias_spec, bias_spec, bias_spec,
            ],
            out_specs=pl.BlockSpec((MOE_ROWS, MOE_TN // 2), o_map),
            scratch_shapes=[
                pltpu.VMEM((2, MOE_SUB, D_MODEL), f32),
                pltpu.VMEM((MOE_ROWS, D_MODEL), bf16),
                pltpu.VMEM((MOE_NJ, MOE_ROWS, MOE_TN), bf16),
                pltpu.VMEM((D_MODEL, MOE_TN), bf16),
                pltpu.VMEM((D_MODEL, MOE_TN), bf16),
                pltpu.VMEM((MOE_RING, D_MODEL, MOE_TN), f32),
                pltpu.SemaphoreType.DMA((2,)),
                pltpu.SemaphoreType.DMA((MOE_RING,)),
            ],
        ),
        out_shape=jax.ShapeDtypeStruct((N_EXPERTS * EXPERT_CAP, D_MODEL // 2), jnp.uint32),
        compiler_params=_cparams(("arbitrary", "arbitrary")),
        name="moe_experts",
    )(we, wq, wn, tot, xs, w_gate, w_up, w_down,
      b_gate.reshape(N_EXPERTS, MOE_NJ, MOE_TN), b_up.reshape(N_EXPERTS, MOE_NJ, MOE_TN),
      b_down.reshape(N_EXPERTS, MOE_NJ, MOE_TN))


def _work_list(cnt_i):
    nq = (cnt_i + MOE_ROWS - 1) // MOE_ROWS
    ends = jnp.cumsum(nq)
    total = ends[-1]
    w = jnp.arange(MOE_WORK, dtype=i32)
    wc = jnp.minimum(w, total - 1)
    e = jnp.minimum(jnp.sum((ends[None, :] <= wc[:, None]).astype(i32), axis=1), N_EXPERTS - 1)
    q = wc - (ends[e] - nq[e])
    rows = jnp.clip(cnt_i[e] - q * MOE_ROWS, 0, MOE_ROWS)
    n = jnp.where(w < total, (rows + MOE_UNIT - 1) // MOE_UNIT, 0)
    return e.astype(i32), q.astype(i32), n.astype(i32), total.astype(i32).reshape(1)


def _combine_kernel(pos_ref, o_ref, gate_ref, h2_ref, fg_ref, out_ref, obuf, sem, *, tm):
    i = pl.program_id(0)

    t0 = pl.multiple_of(i * tm, tm)
    for t in range(tm):
        for k in range(TOP_K):
            p = pos_ref[t0 + (k * SEQ + t)]
            pltpu.make_async_copy(o_ref.at[pl.ds(p, 1)], obuf.at[pl.ds(k * tm + t, 1)],
                                  sem).start(priority=k % 2)
    pltpu.make_async_copy(obuf, obuf, sem).wait()

    gt = gate_ref[...]
    eye = (lax.broadcasted_iota(i32, (tm, tm), 0)
           == lax.broadcasted_iota(i32, (tm, tm), 1)).astype(bf16)
    g1 = gt.astype(bf16)
    r1 = gt - g1.astype(f32)
    g2 = r1.astype(bf16)
    g3 = (r1 - g2.astype(f32)).astype(bf16)
    gates = _nt_dot(eye, g1) + _nt_dot(eye, g2) + _nt_dot(eye, g3)
    half = MOE_TN // 2
    y_parts = [h2_ref[:, c * half:(c + 1) * half] for c in range(D_MODEL // half)]
    for k in range(TOP_K):
        gk = gates[:, k:k + 1]
        for b in range(D_MODEL // MOE_TN):
            lo, hi = _unpack_bf16_pair(obuf[k * tm:(k + 1) * tm, b * half:(b + 1) * half])
            y_parts[2 * b] = y_parts[2 * b] + gk * lo
            y_parts[2 * b + 1] = y_parts[2 * b + 1] + gk * hi
    ss = jnp.zeros((tm, 1), f32)
    for yp in y_parts:
        ss = ss + jnp.sum(yp * yp, axis=-1, keepdims=True)
    inv = lax.rsqrt(ss / float(D_MODEL) + NORM_EPS)
    for c, yp in enumerate(y_parts):
        cols = slice(c * half, (c + 1) * half)
        out_ref[:, cols] = yp * inv * fg_ref[:, cols]


def _combine(pos_flat, o, gates, h2, final_g, tm=256):
    m = h2.shape[0]
    return pl.pallas_call(
        functools.partial(_combine_kernel, tm=tm),
        grid_spec=pltpu.PrefetchScalarGridSpec(
            num_scalar_prefetch=1,
            grid=(m // tm,),
            in_specs=[
                pl.BlockSpec(memory_space=pl.ANY),
                pl.BlockSpec((GATE_ROWS, tm), lambda i, pos: (0, i)),
                pl.BlockSpec((tm, D_MODEL), lambda i, pos: (i, 0)),
                pl.BlockSpec((1, D_MODEL), lambda i, pos: (0, 0)),
            ],
            out_specs=pl.BlockSpec((tm, D_MODEL), lambda i, pos: (i, 0)),
            scratch_shapes=[
                pltpu.VMEM((TOP_K * tm, D_MODEL // 2), jnp.uint32),
                pltpu.SemaphoreType.DMA(()),
            ],
        ),
        out_shape=jax.ShapeDtypeStruct((m, D_MODEL), f32),
        compiler_params=_cparams(("arbitrary",)),
        name="combine",
    )(pos_flat, o, gates, h2, final_g)


def kernel(x, meta_tokens, norm1_g, w_in, pool_w, pool_scale, conv_w, conv_b, dt_bias, a_log,
           d_skip, ssm_norm_g, w_out, norm2_g, router_w, router_b, w_gate, b_gate, w_up, b_up,
           w_down, b_down, final_g):
    assert x.shape == (1, SEQ, D_MODEL)
    x2d = x.reshape(SEQ, D_MODEL)
    lead = CHUNK - N_META
    meta_chunk = jnp.concatenate([jnp.zeros((lead, D_MODEL), f32), meta_tokens.astype(f32)], 0)

    g1 = norm1_g[0].reshape(1, D_MODEL)
    w_main = w_in[0, :, :D_MAIN].astype(bf16)
    w_dt = w_in[0, :, D_MAIN:].astype(bf16)
    head_of_col = jnp.arange(D_SSM, dtype=i32) // HEAD_DIM
    expand = (head_of_col[None, :] == jnp.arange(N_HEADS, dtype=i32)[:, None]).astype(bf16)
    mix_params = (
        pool_w[0], pool_scale[0].reshape(1, D_POOL), conv_w[0], conv_b[0].reshape(1, D_CONV),
        dt_bias[0].reshape(1, N_HEADS), a_log[0].reshape(1, N_HEADS),
        jnp.repeat(d_skip[0], HEAD_DIM).reshape(1, D_SSM), ssm_norm_g[0].reshape(1, D_SSM),
        expand)

    proj_m, dt_m = _inproj(meta_chunk, g1, w_main, w_dt, tm=CHUNK)
    zeros_state = jnp.zeros((N_GROUPS, N_STATE, GROUP_COLS), f32)
    _, state_m, utail_m, ctail_m = _mixer(
        proj_m, dt_m, mix_params, zeros_state, jnp.zeros((POOL_HALO, D_POOL), f32),
        jnp.zeros((CONV_HALO, D_CONV), f32), tm=CHUNK, lead=lead)

    proj, dt_raw = _inproj(x2d, g1, w_main, w_dt, tm=512)
    ymix, _, _, _ = _mixer(proj, dt_raw, mix_params, state_m, utail_m, ctail_m, tm=256, lead=0)

    h2, hn2, pos, gates, cnt = _outproj(
        ymix, w_out[0].astype(bf16), x2d, norm2_g[0].reshape(1, D_MODEL),
        router_w[0].T.astype(bf16),
        router_b[0].reshape(N_EXPERTS, 1))

    pos_flat = pos[:TOP_K].reshape(TOP_K * SEQ)
    cnt_i = cnt.reshape(N_EXPERTS).astype(i32)
    xs = _scatter(pos_flat, cnt_i, hn2)
    we, wq, wn, tot = _work_list(cnt_i)
    o = _moe(we, wq, wn, tot, xs, w_gate[0], w_up[0], w_down[0], b_gate[0], b_up[0], b_down[0])
    out = _combine(pos_flat, o, gates, h2, final_g.reshape(1, D_MODEL))
    return out.reshape(1, SEQ, D_MODEL)
```

```python
import functools

import jax
import jax.numpy as jnp
from jax import lax
from jax.experimental import pallas as pl
from jax.experimental.pallas import tpu as pltpu

f32 = jnp.float32
bf16 = jnp.bfloat16
i32 = jnp.int32

D_MODEL = 2048
SEQ = 8192
CHUNK = 64
N_META = 16
D_POOL = 1024
POOL_WINDOWS = (2, 4, 8, 16)
POOL_GROUP = 256
D_SSM = 1024
HEAD_DIM = 64
N_HEADS = 16
N_GROUPS = 2
HEADS_PER_GROUP = N_HEADS // N_GROUPS
N_STATE = 128
GROUP_COLS = HEADS_PER_GROUP * HEAD_DIM
D_CONV = D_SSM + 2 * N_GROUPS * N_STATE
D_MAIN = D_POOL + D_SSM + D_CONV
N_EXPERTS = 32
TOP_K = 4
D_EXPERT = 2048
SWIGLU_LIMIT = 7.0
SWIGLU_ALPHA = 1.702
NORM_EPS = 1e-5

GATE_ROWS = 8
POOL_HALO = 16
CONV_HALO = 8

MOE_SUB = 256
MOE_UNIT = 128
MOE_ROWS = 1280
MOE_SUBS = MOE_ROWS // MOE_SUB
ZERO_ROWS = MOE_SUB + 8
EXPERT_CAP = -(-(SEQ + ZERO_ROWS) // MOE_ROWS) * MOE_ROWS
MOE_TN = 512
MOE_NJ = D_EXPERT // MOE_TN
MOE_WORK = (SEQ * TOP_K) // MOE_ROWS + N_EXPERTS
MOE_TILES = 3 * MOE_NJ
MOE_RING = 6
assert MOE_TILES % MOE_RING == 0

VMEM_LIMIT = 56 * 1024 * 1024


def _cparams(sem):
    return pltpu.CompilerParams(dimension_semantics=sem, vmem_limit_bytes=VMEM_LIMIT)


PROJ_TN = 512


def _split_dot(q, e):
    hi = q.astype(bf16)
    lo = (q - hi.astype(f32)).astype(bf16)
    return (jnp.dot(hi, e, preferred_element_type=f32)
            + jnp.dot(lo, e, preferred_element_type=f32))


def _nt_dot(a, b):
    return lax.dot_general(a, b, (((1,), (1,)), ((), ())), preferred_element_type=f32)


def _inproj_kernel(x_ref, g_ref, w_ref, wdt_ref, proj_ref, dt_ref):
    x = x_ref[...]
    y = x * lax.rsqrt(jnp.mean(x * x, axis=-1, keepdims=True) + NORM_EPS)
    hn = (y * g_ref[...]).astype(bf16)
    dt_ref[...] = jnp.dot(hn, wdt_ref[...], preferred_element_type=f32)
    for c in range(D_MAIN // PROJ_TN):
        cols = slice(c * PROJ_TN, (c + 1) * PROJ_TN)
        proj_ref[:, cols] = jnp.dot(hn, w_ref[:, cols], preferred_element_type=f32)


def _inproj(x2d, g, w_main, w_dt, tm):
    m = x2d.shape[0]
    resident = dict(pipeline_mode=pl.Buffered(1))
    return pl.pallas_call(
        _inproj_kernel,
        grid=(m // tm,),
        in_specs=[
            pl.BlockSpec((tm, D_MODEL), lambda i: (i, 0)),
            pl.BlockSpec((1, D_MODEL), lambda i: (0, 0)),
            pl.BlockSpec((D_MODEL, D_MAIN), lambda i: (0, 0), **resident),
            pl.BlockSpec((D_MODEL, N_HEADS), lambda i: (0, 0), **resident),
        ],
        out_specs=[
            pl.BlockSpec((tm, D_MAIN), lambda i: (i, 0)),
            pl.BlockSpec((tm, N_HEADS), lambda i: (i, 0)),
        ],
        out_shape=[
            jax.ShapeDtypeStruct((m, D_MAIN), f32),
            jax.ShapeDtypeStruct((m, N_HEADS), f32),
        ],
        compiler_params=_cparams(("arbitrary",)),
        name="inproj",
    )(x2d, g, w_main, w_dt)


def _mixer_kernel(u_ref, z_ref, xs_ref, bc_ref, dt_ref,
                  poolw_ref, pscale_ref, convw_ref, convb_ref, dtb_ref, alog_ref,
                  dskip_ref, ng_ref, expand_ref, state0_ref, uhalo0_ref, chalo0_ref,
                  y_ref, state_out_ref, utail_ref, ctail_ref,
                  state, uext, cext, *, tm, lead):
    i = pl.program_id(0)
    nch = tm // CHUNK

    @pl.when(i == 0)
    def _():
        state[...] = state0_ref[...]
        uext[0:POOL_HALO, :] = uhalo0_ref[...]
        cext[0:CONV_HALO, :] = chalo0_ref[...]

    uext[POOL_HALO:, :] = u_ref[...]
    cext[CONV_HALO:, 0:D_SSM] = xs_ref[...]
    cext[CONV_HALO:, D_SSM:] = bc_ref[...]

    for g, w in enumerate(POOL_WINDOWS):
        cols = slice(g * POOL_GROUP, (g + 1) * POOL_GROUP)
        ue = uext[:, cols]
        s = ue
        sh = 1
        while sh < w:
            s = s + pltpu.roll(s, sh, axis=0)
            sh *= 2
        d = s[POOL_HALO:, :] / float(w) - ue[POOL_HALO:, :]
        yp = jnp.dot(d.astype(bf16), poolw_ref[g].astype(bf16), preferred_element_type=f32)
        y_ref[:, cols] = (yp * pscale_ref[:, cols]).astype(bf16)

    ce = cext[...]
    cw = convw_ref[...]
    acc = (ce * cw[3:4, :] + pltpu.roll(ce, 1, axis=0) * cw[2:3, :]
           + pltpu.roll(ce, 2, axis=0) * cw[1:2, :] + pltpu.roll(ce, 3, axis=0) * cw[0:1, :])
    acc = acc[CONV_HALO:, :] + convb_ref[...]
    xbc = acc * jax.nn.sigmoid(acc)
    xs = xbc[:, 0:D_SSM]
    bm = xbc[:, D_SSM:D_SSM + N_GROUPS * N_STATE]
    cm = xbc[:, D_SSM + N_GROUPS * N_STATE:]

    uext[0:POOL_HALO, :] = uext[tm:tm + POOL_HALO, :]
    cext[0:CONV_HALO, :] = cext[tm:tm + CONV_HALO, :]

    v = dt_ref[...] + dtb_ref[...]
    dt = jnp.maximum(v, 0.0) + jnp.log1p(jnp.exp(-jnp.abs(v)))
    row = lax.broadcasted_iota(i32, (tm, N_HEADS), 0)
    if lead:
        dt = jnp.where(row >= lead, dt, 0.0)
    a = -jnp.exp(alog_ref[...])
    acs = dt * a
    rin = row % CHUNK
    sh = 1
    while sh < CHUNK:
        acs = acs + jnp.where(rin >= sh, pltpu.roll(acs, sh, axis=0), 0.0)
        sh *= 2
    alast = jnp.concatenate(
        [jnp.broadcast_to(acs[c * CHUNK + CHUNK - 1:c * CHUNK + CHUNK, :], (CHUNK, N_HEADS))
         for c in range(nch)], axis=0)
    dec_st = jnp.exp(alast - acs)
    dec_in = jnp.exp(acs)

    expand = expand_ref[...]
    dt_x = _split_dot(dt, expand)
    dtds_x = _split_dot(dt * dec_st, expand)
    decin_x = _split_dot(dec_in, expand)

    eye_h = (lax.broadcasted_iota(i32, (N_HEADS, N_HEADS), 0)
             == lax.broadcasted_iota(i32, (N_HEADS, N_HEADS), 1)).astype(bf16)
    p1 = acs.astype(bf16)
    r1 = acs - p1.astype(f32)
    p2 = r1.astype(bf16)
    p3 = (r1 - p2.astype(f32)).astype(bf16)
    acs_t = _nt_dot(eye_h, p1) + _nt_dot(eye_h, p2) + _nt_dot(eye_h, p3)

    xdt = (xs * dt_x).astype(bf16)
    xds = (xs * dtds_x).astype(bf16)

    eye_n = (lax.broadcasted_iota(i32, (N_STATE, N_STATE), 0)
             == lax.broadcasted_iota(i32, (N_STATE, N_STATE), 1)).astype(bf16)
    causal = (lax.broadcasted_iota(i32, (CHUNK, CHUNK), 0)
              >= lax.broadcasted_iota(i32, (CHUNK, CHUNK), 1))
    col_head = lax.broadcasted_iota(i32, (CHUNK, GROUP_COLS), 1) // HEAD_DIM

    y_chunks = []
    for c in range(nch):
        r0 = c * CHUNK
        y_groups = []
        for g in range(N_GROUPS):
            gc = slice(g * GROUP_COLS, (g + 1) * GROUP_COLS)
            cg = cm[r0:r0 + CHUNK, g * N_STATE:(g + 1) * N_STATE].astype(bf16)
            bg = bm[r0:r0 + CHUNK, g * N_STATE:(g + 1) * N_STATE].astype(bf16)
            gmat = _nt_dot(cg, bg)
            st = state[g]
            y_off = (jnp.dot(cg, st.astype(bf16), preferred_element_type=f32)
                     * decin_x[r0:r0 + CHUNK, gc])
            scores = []
            for h in range(HEADS_PER_GROUP):
                hh = g * HEADS_PER_GROUP + h
                a_col = acs[r0:r0 + CHUNK, hh:hh + 1]
                a_row = acs_t[hh:hh + 1, r0:r0 + CHUNK]
                lmat = jnp.exp(jnp.where(causal, a_col - a_row, -jnp.inf))
                scores.append((gmat * lmat).astype(bf16))
            sc = jnp.concatenate(scores, axis=0)
            full = jnp.dot(sc, xdt[r0:r0 + CHUNK, gc], preferred_element_type=f32)
            y_diag = jnp.zeros((CHUNK, GROUP_COLS), f32)
            for h in range(HEADS_PER_GROUP):
                y_diag = jnp.where(col_head == h, full[h * CHUNK:(h + 1) * CHUNK, :], y_diag)
            y_groups.append(y_diag + y_off)
            bg_t = _nt_dot(eye_n, bg).astype(bf16)
            upd = jnp.dot(bg_t, xds[r0:r0 + CHUNK, gc], preferred_element_type=f32)
            state[g] = st * decin_x[r0 + CHUNK - 1:r0 + CHUNK, gc] + upd
        y_chunks.append(jnp.concatenate(y_groups, axis=1))
    y = jnp.concatenate(y_chunks, axis=0) if nch > 1 else y_chunks[0]

    y = y + dskip_ref[...] * xs
    zz = z_ref[...]
    y = y * (zz * jax.nn.sigmoid(zz))
    for g in range(N_GROUPS):
        gc = slice(g * GROUP_COLS, (g + 1) * GROUP_COLS)
        yg = y[:, gc]
        yg = yg * lax.rsqrt(jnp.mean(yg * yg, axis=-1, keepdims=True) + NORM_EPS)
        y_ref[:, D_POOL + g * GROUP_COLS:D_POOL + (g + 1) * GROUP_COLS] = (
            yg * ng_ref[:, gc]).astype(bf16)

    state_out_ref[...] = state[...]
    utail_ref[...] = uext[0:POOL_HALO, :]
    ctail_ref[...] = cext[0:CONV_HALO, :]


def _mixer(proj, dt_raw, params, state0, uhalo0, chalo0, tm, lead):
    m = proj.shape[0]
    (pool_w, pool_scale, conv_w, conv_b, dt_bias, a_log, dskip_x, norm_g, expand) = params
    full = lambda shape: pl.BlockSpec(shape, lambda i: (0,) * len(shape))
    return pl.pallas_call(
        functools.partial(_mixer_kernel, tm=tm, lead=lead),
        grid=(m // tm,),
        in_specs=[
            pl.BlockSpec((tm, D_POOL), lambda i: (i, 0)),
            pl.BlockSpec((tm, D_SSM), lambda i: (i, 1)),
            pl.BlockSpec((tm, D_SSM), lambda i: (i, 2)),
            pl.BlockSpec((tm, 2 * N_GROUPS * N_STATE), lambda i: (i, 6)),
            pl.BlockSpec((tm, N_HEADS), lambda i: (i, 0)),
            full((len(POOL_WINDOWS), POOL_GROUP, POOL_GROUP)),
            full((1, D_POOL)),
            full((4, D_CONV)),
            full((1, D_CONV)),
            full((1, N_HEADS)),
            full((1, N_HEADS)),
            full((1, D_SSM)),
            full((1, D_SSM)),
            full((N_HEADS, D_SSM)),
            full((N_GROUPS, N_STATE, GROUP_COLS)),
            full((POOL_HALO, D_POOL)),
            full((CONV_HALO, D_CONV)),
        ],
        out_specs=[
            pl.BlockSpec((tm, D_MODEL), lambda i: (i, 0)),
            full((N_GROUPS, N_STATE, GROUP_COLS)),
            full((POOL_HALO, D_POOL)),
            full((CONV_HALO, D_CONV)),
        ],
        out_shape=[
            jax.ShapeDtypeStruct((m, D_MODEL), bf16),
            jax.ShapeDtypeStruct((N_GROUPS, N_STATE, GROUP_COLS), f32),
            jax.ShapeDtypeStruct((POOL_HALO, D_POOL), f32),
            jax.ShapeDtypeStruct((CONV_HALO, D_CONV), f32),
        ],
        scratch_shapes=[
            pltpu.VMEM((N_GROUPS, N_STATE, GROUP_COLS), f32),
            pltpu.VMEM((tm + POOL_HALO, D_POOL), f32),
            pltpu.VMEM((tm + CONV_HALO, D_CONV), f32),
        ],
        compiler_params=_cparams(("arbitrary",)),
        name="mixer",
    )(proj, proj, proj, proj, dt_raw, pool_w, pool_scale, conv_w, conv_b, dt_bias, a_log,
      dskip_x, norm_g, expand, state0, uhalo0, chalo0)


def _outproj_kernel(y_ref, w_ref, x_ref, g2_ref, rw_ref, rb_ref,
                    h2_ref, hn_ref, pos_ref, gate_ref, cnt_ref,
                    cnt, *, tm, tn):
    i = pl.program_id(0)
    nj = D_MODEL // tn

    @pl.when(i == 0)
    def _():
        cnt[...] = jnp.zeros_like(cnt)

    yv = y_ref[...]
    ss = jnp.zeros((tm, 1), f32)
    for q in range(nj):
        cols = slice(q * tn, (q + 1) * tn)
        h = x_ref[:, cols] + jnp.dot(yv, w_ref[:, cols], preferred_element_type=f32)
        h2_ref[:, cols] = h
        ss = ss + jnp.sum(h * h, axis=-1, keepdims=True)

    inv = lax.rsqrt(ss / float(D_MODEL) + NORM_EPS)
    logits = jnp.zeros((N_EXPERTS, tm), f32) + rb_ref[...]
    for q in range(nj):
        cols = slice(q * tn, (q + 1) * tn)
        hn = h2_ref[:, cols] * inv * g2_ref[:, cols]
        hn_ref[:, cols] = hn
        logits = logits + _nt_dot(rw_ref[:, cols], hn.astype(bf16))

    expert = lax.broadcasted_iota(i32, (N_EXPERTS, tm), 0).astype(f32)
    vals = logits
    picked = jnp.zeros((N_EXPERTS, tm), f32)
    tops, idxs = [], []
    for _ in range(TOP_K):
        mx = jnp.max(vals, axis=0, keepdims=True)
        idx = jnp.min(jnp.where(vals == mx, expert, float(N_EXPERTS)), axis=0, keepdims=True)
        hit = expert == idx
        tops.append(mx)
        idxs.append(idx)
        picked = picked + hit.astype(f32)
        vals = jnp.where(hit, -jnp.inf, vals)
    exps = [jnp.exp(t - tops[0]) for t in tops]
    denom = exps[0] + exps[1] + exps[2] + exps[3]

    before = (lax.broadcasted_iota(i32, (tm, tm), 0)
              < lax.broadcasted_iota(i32, (tm, tm), 1)).astype(bf16)
    cum = jnp.dot(picked.astype(bf16), before, preferred_element_type=f32) + cnt[...]
    slot = lax.broadcasted_iota(i32, (GATE_ROWS, tm), 0)
    pos = jnp.zeros((GATE_ROWS, tm), i32)
    gates = jnp.zeros((GATE_ROWS, tm), f32)
    for k in range(TOP_K):
        rank = jnp.sum(jnp.where(expert == idxs[k], cum, 0.0), axis=0, keepdims=True)
        p = idxs[k].astype(i32) * EXPERT_CAP + rank.astype(i32)
        pos = jnp.where(slot == k, p, pos)
        gates = jnp.where(slot == k, exps[k] / denom, gates)
    pos_ref[...] = pos
    gate_ref[...] = gates
    cnt[...] = cnt[...] + jnp.sum(picked, axis=1, keepdims=True)
    cnt_ref[...] = cnt[...]


def _outproj(ymix, w_out, x2d, g2, rw, rb, tm=512, tn=512):
    m = x2d.shape[0]
    resident = dict(pipeline_mode=pl.Buffered(1))
    return pl.pallas_call(
        functools.partial(_outproj_kernel, tm=tm, tn=tn),
        grid=(m // tm,),
        in_specs=[
            pl.BlockSpec((tm, D_MODEL), lambda i: (i, 0)),
            pl.BlockSpec((D_MODEL, D_MODEL), lambda i: (0, 0), **resident),
            pl.BlockSpec((tm, D_MODEL), lambda i: (i, 0)),
            pl.BlockSpec((1, D_MODEL), lambda i: (0, 0)),
            pl.BlockSpec((N_EXPERTS, D_MODEL), lambda i: (0, 0), **resident),
            pl.BlockSpec((N_EXPERTS, 1), lambda i: (0, 0)),
        ],
        out_specs=[
            pl.BlockSpec((tm, D_MODEL), lambda i: (i, 0)),
            pl.BlockSpec((tm, D_MODEL), lambda i: (i, 0)),
            pl.BlockSpec((GATE_ROWS, tm), lambda i: (0, i)),
            pl.BlockSpec((GATE_ROWS, tm), lambda i: (0, i)),
            pl.BlockSpec((N_EXPERTS, 1), lambda i: (0, 0)),
        ],
        out_shape=[
            jax.ShapeDtypeStruct((m, D_MODEL), f32),
            jax.ShapeDtypeStruct((m, D_MODEL), f32),
            jax.ShapeDtypeStruct((GATE_ROWS, m), i32),
            jax.ShapeDtypeStruct((GATE_ROWS, m), f32),
            jax.ShapeDtypeStruct((N_EXPERTS, 1), f32),
        ],
        scratch_shapes=[pltpu.VMEM((N_EXPERTS, 1), f32)],
        compiler_params=_cparams(("arbitrary",)),
        name="outproj_router",
    )(ymix, w_out, x2d, g2, rw, rb)


def _scatter_kernel(pos_ref, cnt_ref, hn_ref, xs_ref, zbuf, sem, zsem, *, tb):
    i = pl.program_id(0)

    @pl.when(i == 0)
    def _():
        zbuf[...] = jnp.zeros_like(zbuf)
        for e in range(N_EXPERTS):
            start = pl.multiple_of(e * EXPERT_CAP + (cnt_ref[e] // 8) * 8, 8)
            pltpu.make_async_copy(zbuf, xs_ref.at[pl.ds(start, ZERO_ROWS)], zsem).start()
        for e in range(N_EXPERTS):
            start = pl.multiple_of(e * EXPERT_CAP + (cnt_ref[e] // 8) * 8, 8)
            pltpu.make_async_copy(zbuf, xs_ref.at[pl.ds(start, ZERO_ROWS)], zsem).wait()

    t0 = pl.multiple_of(i * tb, tb)
    for t in range(tb):
        for k in range(TOP_K):
            p = pos_ref[t0 + (k * SEQ + t)]
            pltpu.make_async_copy(hn_ref.at[pl.ds(t, 1)], xs_ref.at[pl.ds(p, 1)],
                                  sem).start(priority=k % 2)
    done = xs_ref.at[pl.ds(0, tb * TOP_K)]
    pltpu.make_async_copy(done, done, sem).wait()


def _scatter(pos_flat, cnt_i, hn2, tb=256):
    m = hn2.shape[0]
    return pl.pallas_call(
        functools.partial(_scatter_kernel, tb=tb),
        grid_spec=pltpu.PrefetchScalarGridSpec(
            num_scalar_prefetch=2,
            grid=(m // tb,),
            in_specs=[pl.BlockSpec((tb, D_MODEL), lambda i, pos, cnt: (i, 0))],
            out_specs=pl.BlockSpec(memory_space=pl.ANY),
            scratch_shapes=[
                pltpu.VMEM((ZERO_ROWS, D_MODEL), f32),
                pltpu.SemaphoreType.DMA(()),
                pltpu.SemaphoreType.DMA(()),
            ],
        ),
        out_shape=jax.ShapeDtypeStruct((N_EXPERTS * EXPERT_CAP, D_MODEL), f32),
        compiler_params=_cparams(("arbitrary",)),
        name="scatter_rows",
    )(pos_flat, cnt_i, hn2)


def _pack_bf16_pair(lo, hi):
    lo_bits = lax.bitcast_convert_type(lo.astype(bf16).astype(f32), jnp.uint32)
    hi_bits = lax.bitcast_convert_type(hi.astype(bf16).astype(f32), jnp.uint32)
    return (lo_bits >> 16) | (hi_bits & jnp.uint32(0xFFFF0000))


def _unpack_bf16_pair(word):
    lo = lax.bitcast_convert_type(word << 16, f32)
    hi = lax.bitcast_convert_type(word & jnp.uint32(0xFFFF0000), f32)
    return lo, hi


def _moe_kernel(we_ref, wq_ref, wn_ref, tot_ref,
                xs_ref, wg_hbm, wu_hbm, wd_hbm, bg_ref, bu_ref, bd_ref,
                o_ref, stage, xb, abuf, ws_a, ws_b, ring, sem, wsem):
    w = pl.program_id(0)
    j = pl.program_id(1)
    e_cur = we_ref[w]
    units = wn_ref[w]
    n = (units + 1) // 2
    wnext = jnp.minimum(w + 1, MOE_WORK - 1)
    n_next = jnp.where(w + 1 < MOE_WORK, (wn_ref[wnext] + 1) // 2, 0)

    def tile_copy(item, r):
        e = we_ref[item]
        if r < 2 * MOE_NJ:
            src, blk = (wg_hbm if r % 2 == 0 else wu_hbm), r // 2
        else:
            src, blk = wd_hbm, r - 2 * MOE_NJ
        return pltpu.make_async_copy(src.at[e, :, pl.ds(blk * MOE_TN, MOE_TN)],
                                     ring.at[r % MOE_RING], wsem.at[r % MOE_RING])

    def tile_refill(r):
        r2 = r + MOE_RING
        if r2 < MOE_TILES:
            tile_copy(w, r2).start()
        else:
            @pl.when(w + 1 < tot_ref[0])
            def _():
                tile_copy(wnext, r2 - MOE_TILES).start()

    @pl.when((w == 0) & (j == 0))
    def _():
        for r in range(MOE_RING):
            tile_copy(w, r).start()

    def x_copy(item, s):
        base = we_ref[item] * EXPERT_CAP + wq_ref[item] * MOE_ROWS
        return pltpu.make_async_copy(
            xs_ref.at[pl.ds(pl.multiple_of(base + s * MOE_SUB, MOE_SUB), MOE_SUB)],
            stage.at[s % 2], sem.at[s % 2])

    def x_start(item, count, s):
        @pl.when(s < count)
        def _():
            x_copy(item, s).start()

    def x_finish(item, count, s):
        @pl.when(s < count)
        def _():
            x_copy(item, s).wait()
            xb[pl.ds(s * MOE_SUB, MOE_SUB), :] = stage[s % 2].astype(bf16)

    @pl.when((w == 0) & (j == 0))
    def _():
        x_start(w, n, 0)
        x_start(w, n, 1)
        for s in range(MOE_SUBS):
            x_finish(w, n, s)
            if s + 2 < MOE_SUBS:
                x_start(w, n, s + 2)

    def per_subtile(fn):
        def pair(p, carry):
            r0 = p * (2 * MOE_SUB)
            fn(r0, MOE_SUB, False)
            fn(r0 + MOE_SUB, MOE_SUB, False)
            return carry

        pairs = units // 4
        rem = units % 4
        no_pairs = pairs == 0

        @pl.when(pairs > 0)
        def _():
            fn(0, MOE_SUB, True)
            fn(MOE_SUB, MOE_SUB, False)
            lax.fori_loop(1, pairs, pair, 0)

        @pl.when((rem >= 2) & no_pairs)
        def _():
            fn(0, MOE_SUB, True)

        @pl.when((rem >= 2) & (pairs > 0))
        def _():
            fn(pairs * (2 * MOE_SUB), MOE_SUB, False)

        @pl.when((rem == 1) & no_pairs)
        def _():
            fn(0, MOE_UNIT, True)

        @pl.when((rem % 2 == 1) & jnp.logical_not((rem == 1) & no_pairs))
        def _():
            fn(pairs * (2 * MOE_SUB) + (rem // 2) * MOE_SUB, MOE_UNIT, False)

    for jj in range(2 * MOE_NJ):
        @pl.when((j == jj) & (n > 0))
        def _():
            if jj < MOE_NJ:
                tile_copy(w, 2 * jj).wait()
                tile_copy(w, 2 * jj + 1).wait()
            else:
                tile_copy(w, MOE_NJ + jj).wait()

    @pl.when((j < MOE_NJ) & (n > 0))
    def _():
        bgv = bg_ref[e_cur, pl.ds(j, 1), :]
        buv = bu_ref[e_cur, pl.ds(j, 1), :]

        def gate_up(r0, nrows, first):
            rows = pl.ds(pl.multiple_of(r0, MOE_UNIT), nrows)
            x = xb[rows, :]
            if first:
                wa = ring[(2 * j) % MOE_RING].astype(bf16)
                wb = ring[(2 * j + 1) % MOE_RING].astype(bf16)
                ws_a[...] = wa
                ws_b[...] = wb
            else:
                wa = ws_a[...]
                wb = ws_b[...]
            g = jnp.dot(x, wa, preferred_element_type=f32) + bgv
            u = jnp.dot(x, wb, preferred_element_type=f32) + buv
            g = jnp.minimum(g, SWIGLU_LIMIT)
            u = jnp.clip(u, -SWIGLU_LIMIT, SWIGLU_LIMIT)
            act = (u + 1.0) * (g * jax.nn.sigmoid(SWIGLU_ALPHA * g))
            abuf[j, rows, :] = act.astype(bf16)

        per_subtile(gate_up)

    @pl.when((j >= MOE_NJ) & (n > 0))
    def _():
        bdv = bd_ref[e_cur, pl.ds(j - MOE_NJ, 1), :]

        def down(r0, nrows, first):
            rows = pl.ds(pl.multiple_of(r0, MOE_UNIT), nrows)
            acc = jnp.zeros((nrows, MOE_TN), f32) + bdv
            for q in range(MOE_NJ):
                kq = slice(q * MOE_TN, (q + 1) * MOE_TN)
                if first:
                    wq = ring[(MOE_NJ + j) % MOE_RING, kq, :].astype(bf16)
                    ws_a[kq, :] = wq
                else:
                    wq = ws_a[kq, :]
                acc = acc + jnp.dot(abuf[q, rows, :], wq, preferred_element_type=f32)
            o_ref[rows, :] = _pack_bf16_pair(acc[:, :MOE_TN // 2], acc[:, MOE_TN // 2:])

        per_subtile(down)

    for jj in range(2 * MOE_NJ):
        @pl.when((j == jj) & (n > 0))
        def _():
            if jj < MOE_NJ:
                tile_refill(2 * jj)
                tile_refill(2 * jj + 1)
            else:
                tile_refill(MOE_NJ + jj)

    assert 2 * (MOE_NJ - 1) >= MOE_SUBS
    for k in range(MOE_NJ):
        @pl.when(j == MOE_NJ + k)
        def _():
            if k == 0:
                x_start(wnext, n_next, 0)
                x_start(wnext, n_next, 1)
            else:
                for s in (2 * k - 2, 2 * k - 1):
                    if s < MOE_SUBS:
                        x_finish(wnext, n_next, s)
                        if s + 2 < MOE_SUBS:
                            x_start(wnext, n_next, s + 2)


def _moe(we, wq, wn, tot, xs, w_gate, w_up, w_down, b_gate, b_up, b_down):
    blocks_per_expert = EXPERT_CAP // MOE_ROWS

    bias_spec = pl.BlockSpec((N_EXPERTS, MOE_NJ, MOE_TN), lambda w, j, we, wq, wn, tot: (0, 0, 0),
                             pipeline_mode=pl.Buffered(1))

    def o_map(w, j, we, wq, wn, tot):
        jj = jnp.where(wn[w] > 0, jnp.maximum(j - MOE_NJ, 0), MOE_NJ - 1)
        return (we[w] * blocks_per_expert + wq[w], jj)

    any_space = pl.BlockSpec(memory_space=pl.ANY)
    return pl.pallas_call(
        _moe_kernel,
        grid_spec=pltpu.PrefetchScalarGridSpec(
            num_scalar_prefetch=4,
            grid=(MOE_WORK, 2 * MOE_NJ),
            in_specs=[
                any_space, any_space, any_space, any_space,
                bias_spec, bias_spec, bias_spec,
            ],
            out_specs=pl.BlockSpec((MOE_ROWS, MOE_TN // 2), o_map),
            scratch_shapes=[
                pltpu.VMEM((2, MOE_SUB, D_MODEL), f32),
                pltpu.VMEM((MOE_ROWS, D_MODEL), bf16),
                pltpu.VMEM((MOE_NJ, MOE_ROWS, MOE_TN), bf16),
                pltpu.VMEM((D_MODEL, MOE_TN), bf16),
                pltpu.VMEM((D_MODEL, MOE_TN), bf16),
                pltpu.VMEM((MOE_RING, D_MODEL, MOE_TN), f32),
                pltpu.SemaphoreType.DMA((2,)),
                pltpu.SemaphoreType.DMA((MOE_RING,)),
            ],
        ),
        out_shape=jax.ShapeDtypeStruct((N_EXPERTS * EXPERT_CAP, D_MODEL // 2), jnp.uint32),
        compiler_params=_cparams(("arbitrary", "arbitrary")),
        name="moe_experts",
    )(we, wq, wn, tot, xs, w_gate, w_up, w_down,
      b_gate.reshape(N_EXPERTS, MOE_NJ, MOE_TN), b_up.reshape(N_EXPERTS, MOE_NJ, MOE_TN),
      b_down.reshape(N_EXPERTS, MOE_NJ, MOE_TN))


def _work_list(cnt_i):
    nq = (cnt_i + MOE_ROWS - 1) // MOE_ROWS
    ends = jnp.cumsum(nq)
    total = ends[-1]
    w = jnp.arange(MOE_WORK, dtype=i32)
    wc = jnp.minimum(w, total - 1)
    e = jnp.minimum(jnp.sum((ends[None, :] <= wc[:, None]).astype(i32), axis=1), N_EXPERTS - 1)
    q = wc - (ends[e] - nq[e])
    rows = jnp.clip(cnt_i[e] - q * MOE_ROWS, 0, MOE_ROWS)
    n = jnp.where(w < total, (rows + MOE_UNIT - 1) // MOE_UNIT, 0)
    return e.astype(i32), q.astype(i32), n.astype(i32), total.astype(i32).reshape(1)


def _combine_kernel(pos_ref, o_ref, gate_ref, h2_ref, fg_ref, out_ref, obuf, sem, *, tm):
    i = pl.program_id(0)

    t0 = pl.multiple_of(i * tm, tm)
    for t in range(tm):
        for k in range(TOP_K):
            p = pos_ref[t0 + (k * SEQ + t)]
            pltpu.make_async_copy(o_ref.at[pl.ds(p, 1)], obuf.at[pl.ds(k * tm + t, 1)],
                                  sem).start(priority=k % 2)
    pltpu.make_async_copy(obuf, obuf, sem).wait()

    gt = gate_ref[...]
    eye = (lax.broadcasted_iota(i32, (tm, tm), 0)
           == lax.broadcasted_iota(i32, (tm, tm), 1)).astype(bf16)
    g1 = gt.astype(bf16)
    r1 = gt - g1.astype(f32)
    g2 = r1.astype(bf16)
    g3 = (r1 - g2.astype(f32)).astype(bf16)
    gates = _nt_dot(eye, g1) + _nt_dot(eye, g2) + _nt_dot(eye, g3)
    half = MOE_TN // 2
    y_parts = [h2_ref[:, c * half:(c + 1) * half] for c in range(D_MODEL // half)]
    for k in range(TOP_K):
        gk = gates[:, k:k + 1]
        for b in range(D_MODEL // MOE_TN):
            lo, hi = _unpack_bf16_pair(obuf[k * tm:(k + 1) * tm, b * half:(b + 1) * half])
            y_parts[2 * b] = y_parts[2 * b] + gk * lo
            y_parts[2 * b + 1] = y_parts[2 * b + 1] + gk * hi
    ss = jnp.zeros((tm, 1), f32)
    for yp in y_parts:
        ss = ss + jnp.sum(yp * yp, axis=-1, keepdims=True)
    inv = lax.rsqrt(ss / float(D_MODEL) + NORM_EPS)
    for c, yp in enumerate(y_parts):
        cols = slice(c * half, (c + 1) * half)
        out_ref[:, cols] = yp * inv * fg_ref[:, cols]


def _combine(pos_flat, o, gates, h2, final_g, tm=256):
    m = h2.shape[0]
    return pl.pallas_call(
        functools.partial(_combine_kernel, tm=tm),
        grid_spec=pltpu.PrefetchScalarGridSpec(
            num_scalar_prefetch=1,
            grid=(m // tm,),
            in_specs=[
                pl.BlockSpec(memory_space=pl.ANY),
                pl.BlockSpec((GATE_ROWS, tm), lambda i, pos: (0, i)),
                pl.BlockSpec((tm, D_MODEL), lambda i, pos: (i, 0)),
                pl.BlockSpec((1, D_MODEL), lambda i, pos: (0, 0)),
            ],
            out_specs=pl.BlockSpec((tm, D_MODEL), lambda i, pos: (i, 0)),
            scratch_shapes=[
                pltpu.VMEM((TOP_K * tm, D_MODEL // 2), jnp.uint32),
                pltpu.SemaphoreType.DMA(()),
            ],
        ),
        out_shape=jax.ShapeDtypeStruct((m, D_MODEL), f32),
        compiler_params=_cparams(("arbitrary",)),
        name="combine",
    )(pos_flat, o, gates, h2, final_g)


def kernel(x, meta_tokens, norm1_g, w_in, pool_w, pool_scale, conv_w, conv_b, dt_bias, a_log,
           d_skip, ssm_norm_g, w_out, norm2_g, router_w, router_b, w_gate, b_gate, w_up, b_up,
           w_down, b_down, final_g):
    assert x.shape == (1, SEQ, D_MODEL)
    x2d = x.reshape(SEQ, D_MODEL)
    lead = CHUNK - N_META
    meta_chunk = jnp.concatenate([jnp.zeros((lead, D_MODEL), f32), meta_tokens.astype(f32)], 0)

    g1 = norm1_g[0].reshape(1, D_MODEL)
    w_main = w_in[0, :, :D_MAIN].astype(bf16)
    w_dt = w_in[0, :, D_MAIN:].astype(bf16)
    head_of_col = jnp.arange(D_SSM, dtype=i32) // HEAD_DIM
    expand = (head_of_col[None, :] == jnp.arange(N_HEADS, dtype=i32)[:, None]).astype(bf16)
    mix_params = (
        pool_w[0], pool_scale[0].reshape(1, D_POOL), conv_w[0], conv_b[0].reshape(1, D_CONV),
        dt_bias[0].reshape(1, N_HEADS), a_log[0].reshape(1, N_HEADS),
        jnp.repeat(d_skip[0], HEAD_DIM).reshape(1, D_SSM), ssm_norm_g[0].reshape(1, D_SSM),
        expand)

    proj_m, dt_m = _inproj(meta_chunk, g1, w_main, w_dt, tm=CHUNK)
    zeros_state = jnp.zeros((N_GROUPS, N_STATE, GROUP_COLS), f32)
    _, state_m, utail_m, ctail_m = _mixer(
        proj_m, dt_m, mix_params, zeros_state, jnp.zeros((POOL_HALO, D_POOL), f32),
        jnp.zeros((CONV_HALO, D_CONV), f32), tm=CHUNK, lead=lead)

    proj, dt_raw = _inproj(x2d, g1, w_main, w_dt, tm=512)
    ymix, _, _, _ = _mixer(proj, dt_raw, mix_params, state_m, utail_m, ctail_m, tm=256, lead=0)

    h2, hn2, pos, gates, cnt = _outproj(
        ymix, w_out[0].astype(bf16), x2d, norm2_g[0].reshape(1, D_MODEL),
        router_w[0].T.astype(bf16),
        router_b[0].reshape(N_EXPERTS, 1))

    pos_flat = pos[:TOP_K].reshape(TOP_K * SEQ)
    cnt_i = cnt.reshape(N_EXPERTS).astype(i32)
    xs = _scatter(pos_flat, cnt_i, hn2)
    we, wq, wn, tot = _work_list(cnt_i)
    o = _moe(we, wq, wn, tot, xs, w_gate[0], w_up[0], w_down[0], b_gate[0], b_up[0], b_down[0])
    out = _combine(pos_flat, o, gates, h2, final_g.reshape(1, D_MODEL))
    return out.reshape(1, SEQ, D_MODEL)
```

```python
import functools

import jax
import jax.numpy as jnp
from jax import lax
from jax.experimental import pallas as pl
from jax.experimental.pallas import tpu as pltpu

f32 = jnp.float32
bf16 = jnp.bfloat16
i32 = jnp.int32

D_MODEL = 2048
SEQ = 8192
CHUNK = 64
N_META = 16
D_POOL = 1024
POOL_WINDOWS = (2, 4, 8, 16)
POOL_GROUP = 256
D_SSM = 1024
HEAD_DIM = 64
N_HEADS = 16
N_GROUPS = 2
HEADS_PER_GROUP = N_HEADS // N_GROUPS
N_STATE = 128
GROUP_COLS = HEADS_PER_GROUP * HEAD_DIM
D_CONV = D_SSM + 2 * N_GROUPS * N_STATE
D_MAIN = D_POOL + D_SSM + D_CONV
N_EXPERTS = 32
TOP_K = 4
D_EXPERT = 2048
SWIGLU_LIMIT = 7.0
SWIGLU_ALPHA = 1.702
NORM_EPS = 1e-5

GATE_ROWS = 8
POOL_HALO = 16
CONV_HALO = 8

MOE_SUB = 256
MOE_UNIT = 128
MOE_ROWS = 1280
MOE_SUBS = MOE_ROWS // MOE_SUB
ZERO_ROWS = MOE_SUB + 8
EXPERT_CAP = -(-(SEQ + ZERO_ROWS) // MOE_ROWS) * MOE_ROWS
MOE_TN = 512
MOE_NJ = D_EXPERT // MOE_TN
MOE_WORK = (SEQ * TOP_K) // MOE_ROWS + N_EXPERTS
MOE_TILES = 3 * MOE_NJ
MOE_RING = 6
assert MOE_TILES % MOE_RING == 0

VMEM_LIMIT = 56 * 1024 * 1024


def _cparams(sem):
    return pltpu.CompilerParams(dimension_semantics=sem, vmem_limit_bytes=VMEM_LIMIT)


PROJ_TN = 512


def _split_dot(q, e):
    hi = q.astype(bf16)
    lo = (q - hi.astype(f32)).astype(bf16)
    return (jnp.dot(hi, e, preferred_element_type=f32)
            + jnp.dot(lo, e, preferred_element_type=f32))


def _nt_dot(a, b):
    return lax.dot_general(a, b, (((1,), (1,)), ((), ())), preferred_element_type=f32)


def _inproj_kernel(x_ref, g_ref, w_ref, wdt_ref, proj_ref, dt_ref):
    x = x_ref[...]
    y = x * lax.rsqrt(jnp.mean(x * x, axis=-1, keepdims=True) + NORM_EPS)
    hn = (y * g_ref[...]).astype(bf16)
    dt_ref[...] = jnp.dot(hn, wdt_ref[...], preferred_element_type=f32)
    for c in range(D_MAIN // PROJ_TN):
        cols = slice(c * PROJ_TN, (c + 1) * PROJ_TN)
        proj_ref[:, cols] = jnp.dot(hn, w_ref[:, cols], preferred_element_type=f32)


def _inproj(x2d, g, w_main, w_dt, tm):
    m = x2d.shape[0]
    resident = dict(pipeline_mode=pl.Buffered(1))
    return pl.pallas_call(
        _inproj_kernel,
        grid=(m // tm,),
        in_specs=[
            pl.BlockSpec((tm, D_MODEL), lambda i: (i, 0)),
            pl.BlockSpec((1, D_MODEL), lambda i: (0, 0)),
            pl.BlockSpec((D_MODEL, D_MAIN), lambda i: (0, 0), **resident),
            pl.BlockSpec((D_MODEL, N_HEADS), lambda i: (0, 0), **resident),
        ],
        out_specs=[
            pl.BlockSpec((tm, D_MAIN), lambda i: (i, 0)),
            pl.BlockSpec((tm, N_HEADS), lambda i: (i, 0)),
        ],
        out_shape=[
            jax.ShapeDtypeStruct((m, D_MAIN), f32),
            jax.ShapeDtypeStruct((m, N_HEADS), f32),
        ],
        compiler_params=_cparams(("arbitrary",)),
        name="inproj",
    )(x2d, g, w_main, w_dt)


def _mixer_kernel(u_ref, z_ref, xs_ref, bc_ref, dt_ref,
                  poolw_ref, pscale_ref, convw_ref, convb_ref, dtb_ref, alog_ref,
                  dskip_ref, ng_ref, expand_ref, state0_ref, uhalo0_ref, chalo0_ref,
                  y_ref, state_out_ref, utail_ref, ctail_ref,
                  state, uext, cext, *, tm, lead):
    i = pl.program_id(0)
    nch = tm // CHUNK

    @pl.when(i == 0)
    def _():
        state[...] = state0_ref[...]
        uext[0:POOL_HALO, :] = uhalo0_ref[...]
        cext[0:CONV_HALO, :] = chalo0_ref[...]

    uext[POOL_HALO:, :] = u_ref[...]
    cext[CONV_HALO:, 0:D_SSM] = xs_ref[...]
    cext[CONV_HALO:, D_SSM:] = bc_ref[...]

    for g, w in enumerate(POOL_WINDOWS):
        cols = slice(g * POOL_GROUP, (g + 1) * POOL_GROUP)
        ue = uext[:, cols]
        s = ue
        sh = 1
        while sh < w:
            s = s + pltpu.roll(s, sh, axis=0)
            sh *= 2
        d = s[POOL_HALO:, :] / float(w) - ue[POOL_HALO:, :]
        yp = jnp.dot(d.astype(bf16), poolw_ref[g].astype(bf16), preferred_element_type=f32)
        y_ref[:, cols] = (yp * pscale_ref[:, cols]).astype(bf16)

    ce = cext[...]
    cw = convw_ref[...]
    acc = (ce * cw[3:4, :] + pltpu.roll(ce, 1, axis=0) * cw[2:3, :]
           + pltpu.roll(ce, 2, axis=0) * cw[1:2, :] + pltpu.roll(ce, 3, axis=0) * cw[0:1, :])
    acc = acc[CONV_HALO:, :] + convb_ref[...]
    xbc = acc * jax.nn.sigmoid(acc)
    xs = xbc[:, 0:D_SSM]
    bm = xbc[:, D_SSM:D_SSM + N_GROUPS * N_STATE]
    cm = xbc[:, D_SSM + N_GROUPS * N_STATE:]

    uext[0:POOL_HALO, :] = uext[tm:tm + POOL_HALO, :]
    cext[0:CONV_HALO, :] = cext[tm:tm + CONV_HALO, :]

    v = dt_ref[...] + dtb_ref[...]
    dt = jnp.maximum(v, 0.0) + jnp.log1p(jnp.exp(-jnp.abs(v)))
    row = lax.broadcasted_iota(i32, (tm, N_HEADS), 0)
    if lead:
        dt = jnp.where(row >= lead, dt, 0.0)
    a = -jnp.exp(alog_ref[...])
    acs = dt * a
    rin = row % CHUNK
    sh = 1
    while sh < CHUNK:
        acs = acs + jnp.where(rin >= sh, pltpu.roll(acs, sh, axis=0), 0.0)
        sh *= 2
    alast = jnp.concatenate(
        [jnp.broadcast_to(acs[c * CHUNK + CHUNK - 1:c * CHUNK + CHUNK, :], (CHUNK, N_HEADS))
         for c in range(nch)], axis=0)
    dec_st = jnp.exp(alast - acs)
    dec_in = jnp.exp(acs)

    expand = expand_ref[...]
    dt_x = _split_dot(dt, expand)
    dtds_x = _split_dot(dt * dec_st, expand)
    decin_x = _split_dot(dec_in, expand)

    eye_h = (lax.broadcasted_iota(i32, (N_HEADS, N_HEADS), 0)
             == lax.broadcasted_iota(i32, (N_HEADS, N_HEADS), 1)).astype(bf16)
    p1 = acs.astype(bf16)
    r1 = acs - p1.astype(f32)
    p2 = r1.astype(bf16)
    p3 = (r1 - p2.astype(f32)).astype(bf16)
    acs_t = _nt_dot(eye_h, p1) + _nt_dot(eye_h, p2) + _nt_dot(eye_h, p3)

    xdt = (xs * dt_x).astype(bf16)
    xds = (xs * dtds_x).astype(bf16)

    eye_n = (lax.broadcasted_iota(i32, (N_STATE, N_STATE), 0)
             == lax.broadcasted_iota(i32, (N_STATE, N_STATE), 1)).astype(bf16)
    causal = (lax.broadcasted_iota(i32, (CHUNK, CHUNK), 0)
              >= lax.broadcasted_iota(i32, (CHUNK, CHUNK), 1))
    col_head = lax.broadcasted_iota(i32, (CHUNK, GROUP_COLS), 1) // HEAD_DIM

    y_chunks = []
    for c in range(nch):
        r0 = c * CHUNK
        y_groups = []
        for g in range(N_GROUPS):
            gc = slice(g * GROUP_COLS, (g + 1) * GROUP_COLS)
            cg = cm[r0:r0 + CHUNK, g * N_STATE:(g + 1) * N_STATE].astype(bf16)
            bg = bm[r0:r0 + CHUNK, g * N_STATE:(g + 1) * N_STATE].astype(bf16)
            gmat = _nt_dot(cg, bg)
            st = state[g]
            y_off = (jnp.dot(cg, st.astype(bf16), preferred_element_type=f32)
                     * decin_x[r0:r0 + CHUNK, gc])
            scores = []
            for h in range(HEADS_PER_GROUP):
                hh = g * HEADS_PER_GROUP + h
                a_col = acs[r0:r0 + CHUNK, hh:hh + 1]
                a_row = acs_t[hh:hh + 1, r0:r0 + CHUNK]
                lmat = jnp.exp(jnp.where(causal, a_col - a_row, -jnp.inf))
                scores.append((gmat * lmat).astype(bf16))
            sc = jnp.concatenate(scores, axis=0)
            full = jnp.dot(sc, xdt[r0:r0 + CHUNK, gc], preferred_element_type=f32)
            y_diag = jnp.zeros((CHUNK, GROUP_COLS), f32)
            for h in range(HEADS_PER_GROUP):
                y_diag = jnp.where(col_head == h, full[h * CHUNK:(h + 1) * CHUNK, :], y_diag)
            y_groups.append(y_diag + y_off)
            bg_t = _nt_dot(eye_n, bg).astype(bf16)
            upd = jnp.dot(bg_t, xds[r0:r0 + CHUNK, gc], preferred_element_type=f32)
            state[g] = st * decin_x[r0 + CHUNK - 1:r0 + CHUNK, gc] + upd
        y_chunks.append(jnp.concatenate(y_groups, axis=1))
    y = jnp.concatenate(y_chunks, axis=0) if nch > 1 else y_chunks[0]

    y = y + dskip_ref[...] * xs
    zz = z_ref[...]
    y = y * (zz * jax.nn.sigmoid(zz))
    for g in range(N_GROUPS):
        gc = slice(g * GROUP_COLS, (g + 1) * GROUP_COLS)
        yg = y[:, gc]
        yg = yg * lax.rsqrt(jnp.mean(yg * yg, axis=-1, keepdims=True) + NORM_EPS)
        y_ref[:, D_POOL + g * GROUP_COLS:D_POOL + (g + 1) * GROUP_COLS] = (
            yg * ng_ref[:, gc]).astype(bf16)

    state_out_ref[...] = state[...]
    utail_ref[...] = uext[0:POOL_HALO, :]
    ctail_ref[...] = cext[0:CONV_HALO, :]


def _mixer(proj, dt_raw, params, state0, uhalo0, chalo0, tm, lead):
    m = proj.shape[0]
    (pool_w, pool_scale, conv_w, conv_b, dt_bias, a_log, dskip_x, norm_g, expand) = params
    full = lambda shape: pl.BlockSpec(shape, lambda i: (0,) * len(shape))
    return pl.pallas_call(
        functools.partial(_mixer_kernel, tm=tm, lead=lead),
        grid=(m // tm,),
        in_specs=[
            pl.BlockSpec((tm, D_POOL), lambda i: (i, 0)),
            pl.BlockSpec((tm, D_SSM), lambda i: (i, 1)),
            pl.BlockSpec((tm, D_SSM), lambda i: (i, 2)),
            pl.BlockSpec((tm, 2 * N_GROUPS * N_STATE), lambda i: (i, 6)),
            pl.BlockSpec((tm, N_HEADS), lambda i: (i, 0)),
            full((len(POOL_WINDOWS), POOL_GROUP, POOL_GROUP)),
            full((1, D_POOL)),
            full((4, D_CONV)),
            full((1, D_CONV)),
            full((1, N_HEADS)),
            full((1, N_HEADS)),
            full((1, D_SSM)),
            full((1, D_SSM)),
            full((N_HEADS, D_SSM)),
            full((N_GROUPS, N_STATE, GROUP_COLS)),
            full((POOL_HALO, D_POOL)),
            full((CONV_HALO, D_CONV)),
        ],
        out_specs=[
            pl.BlockSpec((tm, D_MODEL), lambda i: (i, 0)),
            full((N_GROUPS, N_STATE, GROUP_COLS)),
            full((POOL_HALO, D_POOL)),
            full((CONV_HALO, D_CONV)),
        ],
        out_shape=[
            jax.ShapeDtypeStruct((m, D_MODEL), bf16),
            jax.ShapeDtypeStruct((N_GROUPS, N_STATE, GROUP_COLS), f32),
            jax.ShapeDtypeStruct((POOL_HALO, D_POOL), f32),
            jax.ShapeDtypeStruct((CONV_HALO, D_CONV), f32),
        ],
        scratch_shapes=[
            pltpu.VMEM((N_GROUPS, N_STATE, GROUP_COLS), f32),
            pltpu.VMEM((tm + POOL_HALO, D_POOL), f32),
            pltpu.VMEM((tm + CONV_HALO, D_CONV), f32),
        ],
        compiler_params=_cparams(("arbitrary",)),
        name="mixer",
    )(proj, proj, proj, proj, dt_raw, pool_w, pool_scale, conv_w, conv_b, dt_bias, a_log,
      dskip_x, norm_g, expand, state0, uhalo0, chalo0)


def _outproj_kernel(y_ref, w_ref, x_ref, g2_ref, rw_ref, rb_ref,
                    h2_ref, hn_ref, pos_ref, gate_ref, cnt_ref,
                    cnt, *, tm, tn):
    i = pl.program_id(0)
    nj = D_MODEL // tn

    @pl.when(i == 0)
    def _():
        cnt[...] = jnp.zeros_like(cnt)

    yv = y_ref[...]
    ss = jnp.zeros((tm, 1), f32)
    for q in range(nj):
        cols = slice(q * tn, (q + 1) * tn)
        h = x_ref[:, cols] + jnp.dot(yv, w_ref[:, cols], preferred_element_type=f32)
        h2_ref[:, cols] = h
        ss = ss + jnp.sum(h * h, axis=-1, keepdims=True)

    inv = lax.rsqrt(ss / float(D_MODEL) + NORM_EPS)
    logits = jnp.zeros((N_EXPERTS, tm), f32) + rb_ref[...]
    for q in range(nj):
        cols = slice(q * tn, (q + 1) * tn)
        hn = h2_ref[:, cols] * inv * g2_ref[:, cols]
        hn_ref[:, cols] = hn
        logits = logits + _nt_dot(rw_ref[:, cols], hn.astype(bf16))

    expert = lax.broadcasted_iota(i32, (N_EXPERTS, tm), 0).astype(f32)
    vals = logits
    picked = jnp.zeros((N_EXPERTS, tm), f32)
    tops, idxs = [], []
    for _ in range(TOP_K):
        mx = jnp.max(vals, axis=0, keepdims=True)
        idx = jnp.min(jnp.where(vals == mx, expert, float(N_EXPERTS)), axis=0, keepdims=True)
        hit = expert == idx
        tops.append(mx)
        idxs.append(idx)
        picked = picked + hit.astype(f32)
        vals = jnp.where(hit, -jnp.inf, vals)
    exps = [jnp.exp(t - tops[0]) for t in tops]
    denom = exps[0] + exps[1] + exps[2] + exps[3]

    before = (lax.broadcasted_iota(i32, (tm, tm), 0)
              < lax.broadcasted_iota(i32, (tm, tm), 1)).astype(bf16)
    cum = jnp.dot(picked.astype(bf16), before, preferred_element_type=f32) + cnt[...]
    slot = lax.broadcasted_iota(i32, (GATE_ROWS, tm), 0)
    pos = jnp.zeros((GATE_ROWS, tm), i32)
    gates = jnp.zeros((GATE_ROWS, tm), f32)
    for k in range(TOP_K):
        rank = jnp.sum(jnp.where(expert == idxs[k], cum, 0.0), axis=0, keepdims=True)
        p = idxs[k].astype(i32) * EXPERT_CAP + rank.astype(i32)
        pos = jnp.where(slot == k, p, pos)
        gates = jnp.where(slot == k, exps[k] / denom, gates)
    pos_ref[...] = pos
    gate_ref[...] = gates
    cnt[...] = cnt[...] + jnp.sum(picked, axis=1, keepdims=True)
    cnt_ref[...] = cnt[...]


def _outproj(ymix, w_out, x2d, g2, rw, rb, tm=512, tn=512):
    m = x2d.shape[0]
    resident = dict(pipeline_mode=pl.Buffered(1))
    return pl.pallas_call(
        functools.partial(_outproj_kernel, tm=tm, tn=tn),
        grid=(m // tm,),
        in_specs=[
            pl.BlockSpec((tm, D_MODEL), lambda i: (i, 0)),
            pl.BlockSpec((D_MODEL, D_MODEL), lambda i: (0, 0), **resident),
            pl.BlockSpec((tm, D_MODEL), lambda i: (i, 0)),
            pl.BlockSpec((1, D_MODEL), lambda i: (0, 0)),
            pl.BlockSpec((N_EXPERTS, D_MODEL), lambda i: (0, 0), **resident),
            pl.BlockSpec((N_EXPERTS, 1), lambda i: (0, 0)),
        ],
        out_specs=[
            pl.BlockSpec((tm, D_MODEL), lambda i: (i, 0)),
            pl.BlockSpec((tm, D_MODEL), lambda i: (i, 0)),
            pl.BlockSpec((GATE_ROWS, tm), lambda i: (0, i)),
            pl.BlockSpec((GATE_ROWS, tm), lambda i: (0, i)),
            pl.BlockSpec((N_EXPERTS, 1), lambda i: (0, 0)),
        ],
        out_shape=[
            jax.ShapeDtypeStruct((m, D_MODEL), f32),
            jax.ShapeDtypeStruct((m, D_MODEL), f32),
            jax.ShapeDtypeStruct((GATE_ROWS, m), i32),
            jax.ShapeDtypeStruct((GATE_ROWS, m), f32),
            jax.ShapeDtypeStruct((N_EXPERTS, 1), f32),
        ],
        scratch_shapes=[pltpu.VMEM((N_EXPERTS, 1), f32)],
        compiler_params=_cparams(("arbitrary",)),
        name="outproj_router",
    )(ymix, w_out, x2d, g2, rw, rb)


def _scatter_kernel(pos_ref, cnt_ref, hn_ref, xs_ref, zbuf, sem, zsem, *, tb):
    i = pl.program_id(0)

    @pl.when(i == 0)
    def _():
        zbuf[...] = jnp.zeros_like(zbuf)
        for e in range(N_EXPERTS):
            start = pl.multiple_of(e * EXPERT_CAP + (cnt_ref[e] // 8) * 8, 8)
            pltpu.make_async_copy(zbuf, xs_ref.at[pl.ds(start, ZERO_ROWS)], zsem).start()
        for e in range(N_EXPERTS):
            start = pl.multiple_of(e * EXPERT_CAP + (cnt_ref[e] // 8) * 8, 8)
            pltpu.make_async_copy(zbuf, xs_ref.at[pl.ds(start, ZERO_ROWS)], zsem).wait()

    t0 = pl.multiple_of(i * tb, tb)
    for t in range(tb):
        for k in range(TOP_K):
            p = pos_ref[t0 + (k * SEQ + t)]
            pltpu.make_async_copy(hn_ref.at[pl.ds(t, 1)], xs_ref.at[pl.ds(p, 1)],
                                  sem).start(priority=k % 2)
    done = xs_ref.at[pl.ds(0, tb * TOP_K)]
    pltpu.make_async_copy(done, done, sem).wait()


def _scatter(pos_flat, cnt_i, hn2, tb=256):
    m = hn2.shape[0]
    return pl.pallas_call(
        functools.partial(_scatter_kernel, tb=tb),
        grid_spec=pltpu.PrefetchScalarGridSpec(
            num_scalar_prefetch=2,
            grid=(m // tb,),
            in_specs=[pl.BlockSpec((tb, D_MODEL), lambda i, pos, cnt: (i, 0))],
            out_specs=pl.BlockSpec(memory_space=pl.ANY),
            scratch_shapes=[
                pltpu.VMEM((ZERO_ROWS, D_MODEL), f32),
                pltpu.SemaphoreType.DMA(()),
                pltpu.SemaphoreType.DMA(()),
            ],
        ),
        out_shape=jax.ShapeDtypeStruct((N_EXPERTS * EXPERT_CAP, D_MODEL), f32),
        compiler_params=_cparams(("arbitrary",)),
        name="scatter_rows",
    )(pos_flat, cnt_i, hn2)


def _pack_bf16_pair(lo, hi):
    lo_bits = lax.bitcast_convert_type(lo.astype(bf16).astype(f32), jnp.uint32)
    hi_bits = lax.bitcast_convert_type(hi.astype(bf16).astype(f32), jnp.uint32)
    return (lo_bits >> 16) | (hi_bits & jnp.uint32(0xFFFF0000))


def _unpack_bf16_pair(word):
    lo = lax.bitcast_convert_type(word << 16, f32)
    hi = lax.bitcast_convert_type(word & jnp.uint32(0xFFFF0000), f32)
    return lo, hi


def _moe_kernel(we_ref, wq_ref, wn_ref, tot_ref,
                xs_ref, wg_hbm, wu_hbm, wd_hbm, bg_ref, bu_ref, bd_ref,
                o_ref, stage, xb, abuf, ws_a, ws_b, ring, sem, wsem):
    w = pl.program_id(0)
    j = pl.program_id(1)
    e_cur = we_ref[w]
    units = wn_ref[w]
    n = (units + 1) // 2
    wnext = jnp.minimum(w + 1, MOE_WORK - 1)
    n_next = jnp.where(w + 1 < MOE_WORK, (wn_ref[wnext] + 1) // 2, 0)

    def tile_copy(item, r):
        e = we_ref[item]
        if r < 2 * MOE_NJ:
            src, blk = (wg_hbm if r % 2 == 0 else wu_hbm), r // 2
        else:
            src, blk = wd_hbm, r - 2 * MOE_NJ
        return pltpu.make_async_copy(src.at[e, :, pl.ds(blk * MOE_TN, MOE_TN)],
                                     ring.at[r % MOE_RING], wsem.at[r % MOE_RING])

    def tile_take(r, dst):
        tile_copy(w, r).wait()
        dst[...] = ring[r % MOE_RING].astype(bf16)
        r2 = r + MOE_RING
        if r2 < MOE_TILES:
            tile_copy(w, r2).start()
        else:
            @pl.when(w + 1 < tot_ref[0])
            def _():
                tile_copy(wnext, r2 - MOE_TILES).start()

    @pl.when((w == 0) & (j == 0))
    def _():
        for r in range(MOE_RING):
            tile_copy(w, r).start()

    def x_copy(item, s):
        base = we_ref[item] * EXPERT_CAP + wq_ref[item] * MOE_ROWS
        return pltpu.make_async_copy(
            xs_ref.at[pl.ds(pl.multiple_of(base + s * MOE_SUB, MOE_SUB), MOE_SUB)],
            stage.at[s % 2], sem.at[s % 2])

    def x_start(item, count, s):
        @pl.when(s < count)
        def _():
            x_copy(item, s).start()

    def x_finish(item, count, s):
        @pl.when(s < count)
        def _():
            x_copy(item, s).wait()
            xb[pl.ds(s * MOE_SUB, MOE_SUB), :] = stage[s % 2].astype(bf16)

    @pl.when((w == 0) & (j == 0))
    def _():
        x_start(w, n, 0)
        x_start(w, n, 1)
        for s in range(MOE_SUBS):
            x_finish(w, n, s)
            if s + 2 < MOE_SUBS:
                x_start(w, n, s + 2)

    def per_subtile(fn):
        def pair(p, carry):
            r0 = p * (2 * MOE_SUB)
            fn(r0, MOE_SUB)
            fn(r0 + MOE_SUB, MOE_SUB)
            return carry

        pairs = units // 4
        rem = units % 4
        lax.fori_loop(0, pairs, pair, 0)

        @pl.when(rem >= 2)
        def _():
            fn(pairs * (2 * MOE_SUB), MOE_SUB)

        @pl.when(rem % 2 == 1)
        def _():
            fn(pairs * (2 * MOE_SUB) + (rem // 2) * MOE_SUB, MOE_UNIT)

    for jj in range(2 * MOE_NJ):
        @pl.when((j == jj) & (n > 0))
        def _():
            if jj < MOE_NJ:
                tile_take(2 * jj, ws_a)
                tile_take(2 * jj + 1, ws_b)
            else:
                tile_take(MOE_NJ + jj, ws_a)

    @pl.when((j < MOE_NJ) & (n > 0))
    def _():
        bgv = bg_ref[e_cur, pl.ds(j, 1), :]
        buv = bu_ref[e_cur, pl.ds(j, 1), :]

        def gate_up(r0, nrows):
            rows = pl.ds(pl.multiple_of(r0, MOE_UNIT), nrows)
            x = xb[rows, :]
            g = jnp.dot(x, ws_a[...], preferred_element_type=f32) + bgv
            u = jnp.dot(x, ws_b[...], preferred_element_type=f32) + buv
            g = jnp.minimum(g, SWIGLU_LIMIT)
            u = jnp.clip(u, -SWIGLU_LIMIT, SWIGLU_LIMIT)
            act = (u + 1.0) * (g * jax.nn.sigmoid(SWIGLU_ALPHA * g))
            abuf[j, rows, :] = act.astype(bf16)

        per_subtile(gate_up)

    @pl.when((j >= MOE_NJ) & (n > 0))
    def _():
        bdv = bd_ref[e_cur, pl.ds(j - MOE_NJ, 1), :]

        def down(r0, nrows):
            rows = pl.ds(pl.multiple_of(r0, MOE_UNIT), nrows)
            acc = jnp.zeros((nrows, MOE_TN), f32) + bdv
            for q in range(MOE_NJ):
                acc = acc + jnp.dot(abuf[q, rows, :], ws_a[q * MOE_TN:(q + 1) * MOE_TN, :],
                                    preferred_element_type=f32)
            o_ref[rows, :] = _pack_bf16_pair(acc[:, :MOE_TN // 2], acc[:, MOE_TN // 2:])

        per_subtile(down)

    assert 2 * (MOE_NJ - 1) >= MOE_SUBS
    for k in range(MOE_NJ):
        @pl.when(j == MOE_NJ + k)
        def _():
            if k == 0:
                x_start(wnext, n_next, 0)
                x_start(wnext, n_next, 1)
            else:
                for s in (2 * k - 2, 2 * k - 1):
                    if s < MOE_SUBS:
                        x_finish(wnext, n_next, s)
                        if s + 2 < MOE_SUBS:
                            x_start(wnext, n_next, s + 2)


def _moe(we, wq, wn, tot, xs, w_gate, w_up, w_down, b_gate, b_up, b_down):
    blocks_per_expert = EXPERT_CAP // MOE_ROWS

    bias_spec = pl.BlockSpec((N_EXPERTS, MOE_NJ, MOE_TN), lambda w, j, we, wq, wn, tot: (0, 0, 0),
                             pipeline_mode=pl.Buffered(1))

    def o_map(w, j, we, wq, wn, tot):
        jj = jnp.where(wn[w] > 0, jnp.maximum(j - MOE_NJ, 0), MOE_NJ - 1)
        return (we[w] * blocks_per_expert + wq[w], jj)

    any_space = pl.BlockSpec(memory_space=pl.ANY)
    return pl.pallas_call(
        _moe_kernel,
        grid_spec=pltpu.PrefetchScalarGridSpec(
            num_scalar_prefetch=4,
            grid=(MOE_WORK, 2 * MOE_NJ),
            in_specs=[
                any_space, any_space, any_space, any_space,
                bias_spec, bias_spec, bias_spec,
            ],
            out_specs=pl.BlockSpec((MOE_ROWS, MOE_TN // 2), o_map),
            scratch_shapes=[
                pltpu.VMEM((2, MOE_SUB, D_MODEL), f32),
                pltpu.VMEM((MOE_ROWS, D_MODEL), bf16),
                pltpu.VMEM((MOE_NJ, MOE_ROWS, MOE_TN), bf16),
                pltpu.VMEM((D_MODEL, MOE_TN), bf16),
                pltpu.VMEM((D_MODEL, MOE_TN), bf16),
                pltpu.VMEM((MOE_RING, D_MODEL, MOE_TN), f32),
                pltpu.SemaphoreType.DMA((2,)),
                pltpu.SemaphoreType.DMA((MOE_RING,)),
            ],
        ),
        out_shape=jax.ShapeDtypeStruct((N_EXPERTS * EXPERT_CAP, D_MODEL // 2), jnp.uint32),
        compiler_params=_cparams(("arbitrary", "arbitrary")),
        name="moe_experts",
    )(we, wq, wn, tot, xs, w_gate, w_up, w_down,
      b_gate.reshape(N_EXPERTS, MOE_NJ, MOE_TN), b_up.reshape(N_EXPERTS, MOE_NJ, MOE_TN),
      b_down.reshape(N_EXPERTS, MOE_NJ, MOE_TN))


def _work_list(cnt_i):
    nq = (cnt_i + MOE_ROWS - 1) // MOE_ROWS
    ends = jnp.cumsum(nq)
    total = ends[-1]
    w = jnp.arange(MOE_WORK, dtype=i32)
    wc = jnp.minimum(w, total - 1)
    e = jnp.minimum(jnp.sum((ends[None, :] <= wc[:, None]).astype(i32), axis=1), N_EXPERTS - 1)
    q = wc - (ends[e] - nq[e])
    rows = jnp.clip(cnt_i[e] - q * MOE_ROWS, 0, MOE_ROWS)
    n = jnp.where(w < total, (rows + MOE_UNIT - 1) // MOE_UNIT, 0)
    return e.astype(i32), q.astype(i32), n.astype(i32), total.astype(i32).reshape(1)


def _combine_kernel(pos_ref, o_ref, gate_ref, h2_ref, fg_ref, out_ref, obuf, sem, *, tm):
    i = pl.program_id(0)
    t0 = pl.multiple_of(i * (2 * tm), 2 * tm)

    def issue(part):
        for t in range(tm):
            for k in range(TOP_K):
                p = pos_ref[t0 + (k * SEQ + part * tm + t)]
                pltpu.make_async_copy(o_ref.at[pl.ds(p, 1)],
                                      obuf.at[part, pl.ds(k * tm + t, 1)],
                                      sem.at[part]).start(priority=k % 2)

    def finish(part):
        pltpu.make_async_copy(obuf.at[part], obuf.at[part], sem.at[part]).wait()
        rows = slice(part * tm, (part + 1) * tm)
        gt = gate_ref[:, rows]
        eye = (lax.broadcasted_iota(i32, (tm, tm), 0)
               == lax.broadcasted_iota(i32, (tm, tm), 1)).astype(bf16)
        g1 = gt.astype(bf16)
        r1 = gt - g1.astype(f32)
        g2 = r1.astype(bf16)
        g3 = (r1 - g2.astype(f32)).astype(bf16)
        gates = _nt_dot(eye, g1) + _nt_dot(eye, g2) + _nt_dot(eye, g3)
        half = MOE_TN // 2
        y_parts = [h2_ref[rows, c * half:(c + 1) * half] for c in range(D_MODEL // half)]
        for k in range(TOP_K):
            gk = gates[:, k:k + 1]
            for b in range(D_MODEL // MOE_TN):
                lo, hi = _unpack_bf16_pair(
                    obuf[part, k * tm:(k + 1) * tm, b * half:(b + 1) * half])
                y_parts[2 * b] = y_parts[2 * b] + gk * lo
                y_parts[2 * b + 1] = y_parts[2 * b + 1] + gk * hi
        ss = jnp.zeros((tm, 1), f32)
        for yp in y_parts:
            ss = ss + jnp.sum(yp * yp, axis=-1, keepdims=True)
        inv = lax.rsqrt(ss / float(D_MODEL) + NORM_EPS)
        for c, yp in enumerate(y_parts):
            cols = slice(c * half, (c + 1) * half)
            out_ref[rows, cols] = yp * inv * fg_ref[:, cols]

    issue(0)
    issue(1)
    finish(0)
    finish(1)


def _combine(pos_flat, o, gates, h2, final_g, tm=256):
    m = h2.shape[0]
    return pl.pallas_call(
        functools.partial(_combine_kernel, tm=tm),
        grid_spec=pltpu.PrefetchScalarGridSpec(
            num_scalar_prefetch=1,
            grid=(m // (2 * tm),),
            in_specs=[
                pl.BlockSpec(memory_space=pl.ANY),
                pl.BlockSpec((GATE_ROWS, 2 * tm), lambda i, pos: (0, i)),
                pl.BlockSpec((2 * tm, D_MODEL), lambda i, pos: (i, 0)),
                pl.BlockSpec((1, D_MODEL), lambda i, pos: (0, 0)),
            ],
            out_specs=pl.BlockSpec((2 * tm, D_MODEL), lambda i, pos: (i, 0)),
            scratch_shapes=[
                pltpu.VMEM((2, TOP_K * tm, D_MODEL // 2), jnp.uint32),
                pltpu.SemaphoreType.DMA((2,)),
            ],
        ),
        out_shape=jax.ShapeDtypeStruct((m, D_MODEL), f32),
        compiler_params=_cparams(("arbitrary",)),
        name="combine",
    )(pos_flat, o, gates, h2, final_g)


def kernel(x, meta_tokens, norm1_g, w_in, pool_w, pool_scale, conv_w, conv_b, dt_bias, a_log,
           d_skip, ssm_norm_g, w_out, norm2_g, router_w, router_b, w_gate, b_gate, w_up, b_up,
           w_down, b_down, final_g):
    assert x.shape == (1, SEQ, D_MODEL)
    x2d = x.reshape(SEQ, D_MODEL)
    lead = CHUNK - N_META
    meta_chunk = jnp.concatenate([jnp.zeros((lead, D_MODEL), f32), meta_tokens.astype(f32)], 0)

    g1 = norm1_g[0].reshape(1, D_MODEL)
    w_main = w_in[0, :, :D_MAIN].astype(bf16)
    w_dt = w_in[0, :, D_MAIN:].astype(bf16)
    head_of_col = jnp.arange(D_SSM, dtype=i32) // HEAD_DIM
    expand = (head_of_col[None, :] == jnp.arange(N_HEADS, dtype=i32)[:, None]).astype(bf16)
    mix_params = (
        pool_w[0], pool_scale[0].reshape(1, D_POOL), conv_w[0], conv_b[0].reshape(1, D_CONV),
        dt_bias[0].reshape(1, N_HEADS), a_log[0].reshape(1, N_HEADS),
        jnp.repeat(d_skip[0], HEAD_DIM).reshape(1, D_SSM), ssm_norm_g[0].reshape(1, D_SSM),
        expand)

    proj_m, dt_m = _inproj(meta_chunk, g1, w_main, w_dt, tm=CHUNK)
    zeros_state = jnp.zeros((N_GROUPS, N_STATE, GROUP_COLS), f32)
    _, state_m, utail_m, ctail_m = _mixer(
        proj_m, dt_m, mix_params, zeros_state, jnp.zeros((POOL_HALO, D_POOL), f32),
        jnp.zeros((CONV_HALO, D_CONV), f32), tm=CHUNK, lead=lead)

    proj, dt_raw = _inproj(x2d, g1, w_main, w_dt, tm=512)
    ymix, _, _, _ = _mixer(proj, dt_raw, mix_params, state_m, utail_m, ctail_m, tm=256, lead=0)

    h2, hn2, pos, gates, cnt = _outproj(
        ymix, w_out[0].astype(bf16), x2d, norm2_g[0].reshape(1, D_MODEL),
        router_w[0].T.astype(bf16),
        router_b[0].reshape(N_EXPERTS, 1))

    pos_flat = pos[:TOP_K].reshape(TOP_K * SEQ)
    cnt_i = cnt.reshape(N_EXPERTS).astype(i32)
    xs = _scatter(pos_flat, cnt_i, hn2)
    we, wq, wn, tot = _work_list(cnt_i)
    o = _moe(we, wq, wn, tot, xs, w_gate[0], w_up[0], w_down[0], b_gate[0], b_up[0], b_down[0])
    out = _combine(pos_flat, o, gates, h2, final_g.reshape(1, D_MODEL))
    return out.reshape(1, SEQ, D_MODEL)
```

```python
import functools

import jax
import jax.numpy as jnp
from jax import lax
from jax.experimental import pallas as pl
from jax.experimental.pallas import tpu as pltpu

f32 = jnp.float32
bf16 = jnp.bfloat16
i32 = jnp.int32

D_MODEL = 2048
SEQ = 8192
CHUNK = 64
N_META = 16
D_POOL = 1024
POOL_WINDOWS = (2, 4, 8, 16)
POOL_GROUP = 256
D_SSM = 1024
HEAD_DIM = 64
N_HEADS = 16
N_GROUPS = 2
HEADS_PER_GROUP = N_HEADS // N_GROUPS
N_STATE = 128
GROUP_COLS = HEADS_PER_GROUP * HEAD_DIM
D_CONV = D_SSM + 2 * N_GROUPS * N_STATE
D_MAIN = D_POOL + D_SSM + D_CONV
N_EXPERTS = 32
TOP_K = 4
D_EXPERT = 2048
SWIGLU_LIMIT = 7.0
SWIGLU_ALPHA = 1.702
NORM_EPS = 1e-5

GATE_ROWS = 8
POOL_HALO = 16
CONV_HALO = 8

MOE_SUB = 256
MOE_UNIT = 128
MOE_ROWS = 1280
MOE_SUBS = MOE_ROWS // MOE_SUB
ZERO_ROWS = MOE_SUB + 8
EXPERT_CAP = -(-(SEQ + ZERO_ROWS) // MOE_ROWS) * MOE_ROWS
MOE_TN = 512
MOE_NJ = D_EXPERT // MOE_TN
MOE_WORK = (SEQ * TOP_K) // MOE_ROWS + N_EXPERTS
MOE_TILES = 3 * MOE_NJ
MOE_RING = 6
assert MOE_TILES % MOE_RING == 0

VMEM_LIMIT = 56 * 1024 * 1024


def _cparams(sem):
    return pltpu.CompilerParams(dimension_semantics=sem, vmem_limit_bytes=VMEM_LIMIT)


PROJ_TN = 512


def _split_dot(q, e):
    hi = q.astype(bf16)
    lo = (q - hi.astype(f32)).astype(bf16)
    return (jnp.dot(hi, e, preferred_element_type=f32)
            + jnp.dot(lo, e, preferred_element_type=f32))


def _nt_dot(a, b):
    return lax.dot_general(a, b, (((1,), (1,)), ((), ())), preferred_element_type=f32)


def _inproj_kernel(x_ref, g_ref, w_ref, wdt_ref, proj_ref, dt_ref):
    x = x_ref[...]
    y = x * lax.rsqrt(jnp.mean(x * x, axis=-1, keepdims=True) + NORM_EPS)
    hn = (y * g_ref[...]).astype(bf16)
    dt_ref[...] = jnp.dot(hn, wdt_ref[...], preferred_element_type=f32)
    for c in range(D_MAIN // PROJ_TN):
        cols = slice(c * PROJ_TN, (c + 1) * PROJ_TN)
        proj_ref[:, cols] = jnp.dot(hn, w_ref[:, cols], preferred_element_type=f32)


def _inproj(x2d, g, w_main, w_dt, tm):
    m = x2d.shape[0]
    resident = dict(pipeline_mode=pl.Buffered(1))
    return pl.pallas_call(
        _inproj_kernel,
        grid=(m // tm,),
        in_specs=[
            pl.BlockSpec((tm, D_MODEL), lambda i: (i, 0)),
            pl.BlockSpec((1, D_MODEL), lambda i: (0, 0)),
            pl.BlockSpec((D_MODEL, D_MAIN), lambda i: (0, 0), **resident),
            pl.BlockSpec((D_MODEL, N_HEADS), lambda i: (0, 0), **resident),
        ],
        out_specs=[
            pl.BlockSpec((tm, D_MAIN), lambda i: (i, 0)),
            pl.BlockSpec((tm, N_HEADS), lambda i: (i, 0)),
        ],
        out_shape=[
            jax.ShapeDtypeStruct((m, D_MAIN), f32),
            jax.ShapeDtypeStruct((m, N_HEADS), f32),
        ],
        compiler_params=_cparams(("arbitrary",)),
        name="inproj",
    )(x2d, g, w_main, w_dt)


def _mixer_kernel(u_ref, z_ref, xs_ref, bc_ref, dt_ref,
                  poolw_ref, pscale_ref, convw_ref, convb_ref, dtb_ref, alog_ref,
                  dskip_ref, ng_ref, expand_ref, state0_ref, uhalo0_ref, chalo0_ref,
                  y_ref, state_out_ref, utail_ref, ctail_ref,
                  state, uext, cext, *, tm, lead):
    i = pl.program_id(0)
    nch = tm // CHUNK

    @pl.when(i == 0)
    def _():
        state[...] = state0_ref[...]
        uext[0:POOL_HALO, :] = uhalo0_ref[...]
        cext[0:CONV_HALO, :] = chalo0_ref[...]

    uext[POOL_HALO:, :] = u_ref[...]
    cext[CONV_HALO:, 0:D_SSM] = xs_ref[...]
    cext[CONV_HALO:, D_SSM:] = bc_ref[...]

    for g, w in enumerate(POOL_WINDOWS):
        cols = slice(g * POOL_GROUP, (g + 1) * POOL_GROUP)
        ue = uext[:, cols]
        s = ue
        sh = 1
        while sh < w:
            s = s + pltpu.roll(s, sh, axis=0)
            sh *= 2
        d = s[POOL_HALO:, :] / float(w) - ue[POOL_HALO:, :]
        yp = jnp.dot(d.astype(bf16), poolw_ref[g].astype(bf16), preferred_element_type=f32)
        y_ref[:, cols] = (yp * pscale_ref[:, cols]).astype(bf16)

    ce = cext[...]
    cw = convw_ref[...]
    acc = (ce * cw[3:4, :] + pltpu.roll(ce, 1, axis=0) * cw[2:3, :]
           + pltpu.roll(ce, 2, axis=0) * cw[1:2, :] + pltpu.roll(ce, 3, axis=0) * cw[0:1, :])
    acc = acc[CONV_HALO:, :] + convb_ref[...]
    xbc = acc * jax.nn.sigmoid(acc)
    xs = xbc[:, 0:D_SSM]
    bm = xbc[:, D_SSM:D_SSM + N_GROUPS * N_STATE]
    cm = xbc[:, D_SSM + N_GROUPS * N_STATE:]

    uext[0:POOL_HALO, :] = uext[tm:tm + POOL_HALO, :]
    cext[0:CONV_HALO, :] = cext[tm:tm + CONV_HALO, :]

    v = dt_ref[...] + dtb_ref[...]
    dt = jnp.maximum(v, 0.0) + jnp.log1p(jnp.exp(-jnp.abs(v)))
    row = lax.broadcasted_iota(i32, (tm, N_HEADS), 0)
    if lead:
        dt = jnp.where(row >= lead, dt, 0.0)
    a = -jnp.exp(alog_ref[...])
    acs = dt * a
    rin = row % CHUNK
    sh = 1
    while sh < CHUNK:
        acs = acs + jnp.where(rin >= sh, pltpu.roll(acs, sh, axis=0), 0.0)
        sh *= 2
    alast = jnp.concatenate(
        [jnp.broadcast_to(acs[c * CHUNK + CHUNK - 1:c * CHUNK + CHUNK, :], (CHUNK, N_HEADS))
         for c in range(nch)], axis=0)
    dec_st = jnp.exp(alast - acs)
    dec_in = jnp.exp(acs)

    expand = expand_ref[...]
    dt_x = _split_dot(dt, expand)
    dtds_x = _split_dot(dt * dec_st, expand)
    decin_x = _split_dot(dec_in, expand)

    eye_h = (lax.broadcasted_iota(i32, (N_HEADS, N_HEADS), 0)
             == lax.broadcasted_iota(i32, (N_HEADS, N_HEADS), 1)).astype(bf16)
    p1 = acs.astype(bf16)
    r1 = acs - p1.astype(f32)
    p2 = r1.astype(bf16)
    p3 = (r1 - p2.astype(f32)).astype(bf16)
    acs_t = _nt_dot(eye_h, p1) + _nt_dot(eye_h, p2) + _nt_dot(eye_h, p3)

    xdt = (xs * dt_x).astype(bf16)
    xds = (xs * dtds_x).astype(bf16)

    eye_n = (lax.broadcasted_iota(i32, (N_STATE, N_STATE), 0)
             == lax.broadcasted_iota(i32, (N_STATE, N_STATE), 1)).astype(bf16)
    causal = (lax.broadcasted_iota(i32, (CHUNK, CHUNK), 0)
              >= lax.broadcasted_iota(i32, (CHUNK, CHUNK), 1))
    col_head = lax.broadcasted_iota(i32, (CHUNK, GROUP_COLS), 1) // HEAD_DIM

    y_chunks = []
    for c in range(nch):
        r0 = c * CHUNK
        y_groups = []
        for g in range(N_GROUPS):
            gc = slice(g * GROUP_COLS, (g + 1) * GROUP_COLS)
            cg = cm[r0:r0 + CHUNK, g * N_STATE:(g + 1) * N_STATE].astype(bf16)
            bg = bm[r0:r0 + CHUNK, g * N_STATE:(g + 1) * N_STATE].astype(bf16)
            gmat = _nt_dot(cg, bg)
            st = state[g]
            y_off = (jnp.dot(cg, st.astype(bf16), preferred_element_type=f32)
                     * decin_x[r0:r0 + CHUNK, gc])
            scores = []
            for h in range(HEADS_PER_GROUP):
                hh = g * HEADS_PER_GROUP + h
                a_col = acs[r0:r0 + CHUNK, hh:hh + 1]
                a_row = acs_t[hh:hh + 1, r0:r0 + CHUNK]
                lmat = jnp.exp(jnp.where(causal, a_col - a_row, -jnp.inf))
                scores.append((gmat * lmat).astype(bf16))
            sc = jnp.concatenate(scores, axis=0)
            full = jnp.dot(sc, xdt[r0:r0 + CHUNK, gc], preferred_element_type=f32)
            y_diag = jnp.zeros((CHUNK, GROUP_COLS), f32)
            for h in range(HEADS_PER_GROUP):
                y_diag = jnp.where(col_head == h, full[h * CHUNK:(h + 1) * CHUNK, :], y_diag)
            y_groups.append(y_diag + y_off)
            bg_t = _nt_dot(eye_n, bg).astype(bf16)
            upd = jnp.dot(bg_t, xds[r0:r0 + CHUNK, gc], preferred_element_type=f32)
            state[g] = st * decin_x[r0 + CHUNK - 1:r0 + CHUNK, gc] + upd
        y_chunks.append(jnp.concatenate(y_groups, axis=1))
    y = jnp.concatenate(y_chunks, axis=0) if nch > 1 else y_chunks[0]

    y = y + dskip_ref[...] * xs
    zz = z_ref[...]
    y = y * (zz * jax.nn.sigmoid(zz))
    for g in range(N_GROUPS):
        gc = slice(g * GROUP_COLS, (g + 1) * GROUP_COLS)
        yg = y[:, gc]
        yg = yg * lax.rsqrt(jnp.mean(yg * yg, axis=-1, keepdims=True) + NORM_EPS)
        y_ref[:, D_POOL + g * GROUP_COLS:D_POOL + (g + 1) * GROUP_COLS] = (
            yg * ng_ref[:, gc]).astype(bf16)

    state_out_ref[...] = state[...]
    utail_ref[...] = uext[0:POOL_HALO, :]
    ctail_ref[...] = cext[0:CONV_HALO, :]


def _mixer(proj, dt_raw, params, state0, uhalo0, chalo0, tm, lead):
    m = proj.shape[0]
    (pool_w, pool_scale, conv_w, conv_b, dt_bias, a_log, dskip_x, norm_g, expand) = params
    full = lambda shape: pl.BlockSpec(shape, lambda i: (0,) * len(shape))
    return pl.pallas_call(
        functools.partial(_mixer_kernel, tm=tm, lead=lead),
        grid=(m // tm,),
        in_specs=[
            pl.BlockSpec((tm, D_POOL), lambda i: (i, 0)),
            pl.BlockSpec((tm, D_SSM), lambda i: (i, 1)),
            pl.BlockSpec((tm, D_SSM), lambda i: (i, 2)),
            pl.BlockSpec((tm, 2 * N_GROUPS * N_STATE), lambda i: (i, 6)),
            pl.BlockSpec((tm, N_HEADS), lambda i: (i, 0)),
            full((len(POOL_WINDOWS), POOL_GROUP, POOL_GROUP)),
            full((1, D_POOL)),
            full((4, D_CONV)),
            full((1, D_CONV)),
            full((1, N_HEADS)),
            full((1, N_HEADS)),
            full((1, D_SSM)),
            full((1, D_SSM)),
            full((N_HEADS, D_SSM)),
            full((N_GROUPS, N_STATE, GROUP_COLS)),
            full((POOL_HALO, D_POOL)),
            full((CONV_HALO, D_CONV)),
        ],
        out_specs=[
            pl.BlockSpec((tm, D_MODEL), lambda i: (i, 0)),
            full((N_GROUPS, N_STATE, GROUP_COLS)),
            full((POOL_HALO, D_POOL)),
            full((CONV_HALO, D_CONV)),
        ],
        out_shape=[
            jax.ShapeDtypeStruct((m, D_MODEL), bf16),
            jax.ShapeDtypeStruct((N_GROUPS, N_STATE, GROUP_COLS), f32),
            jax.ShapeDtypeStruct((POOL_HALO, D_POOL), f32),
            jax.ShapeDtypeStruct((CONV_HALO, D_CONV), f32),
        ],
        scratch_shapes=[
            pltpu.VMEM((N_GROUPS, N_STATE, GROUP_COLS), f32),
            pltpu.VMEM((tm + POOL_HALO, D_POOL), f32),
            pltpu.VMEM((tm + CONV_HALO, D_CONV), f32),
        ],
        compiler_params=_cparams(("arbitrary",)),
        name="mixer",
    )(proj, proj, proj, proj, dt_raw, pool_w, pool_scale, conv_w, conv_b, dt_bias, a_log,
      dskip_x, norm_g, expand, state0, uhalo0, chalo0)


def _outproj_kernel(y_ref, w_ref, x_ref, g2_ref, rw_ref, rb_ref,
                    h2_ref, hn_ref, pos_ref, gate_ref, cnt_ref,
                    cnt, *, tm, tn):
    i = pl.program_id(0)
    nj = D_MODEL // tn

    @pl.when(i == 0)
    def _():
        cnt[...] = jnp.zeros_like(cnt)

    yv = y_ref[...]
    ss = jnp.zeros((tm, 1), f32)
    for q in range(nj):
        cols = slice(q * tn, (q + 1) * tn)
        h = x_ref[:, cols] + jnp.dot(yv, w_ref[:, cols], preferred_element_type=f32)
        h2_ref[:, cols] = h
        ss = ss + jnp.sum(h * h, axis=-1, keepdims=True)

    inv = lax.rsqrt(ss / float(D_MODEL) + NORM_EPS)
    logits = jnp.zeros((N_EXPERTS, tm), f32) + rb_ref[...]
    for q in range(nj):
        cols = slice(q * tn, (q + 1) * tn)
        hn = h2_ref[:, cols] * inv * g2_ref[:, cols]
        hn_ref[:, cols] = hn
        logits = logits + _nt_dot(rw_ref[:, cols], hn.astype(bf16))

    expert = lax.broadcasted_iota(i32, (N_EXPERTS, tm), 0).astype(f32)
    vals = logits
    picked = jnp.zeros((N_EXPERTS, tm), f32)
    tops, idxs = [], []
    for _ in range(TOP_K):
        mx = jnp.max(vals, axis=0, keepdims=True)
        idx = jnp.min(jnp.where(vals == mx, expert, float(N_EXPERTS)), axis=0, keepdims=True)
        hit = expert == idx
        tops.append(mx)
        idxs.append(idx)
        picked = picked + hit.astype(f32)
        vals = jnp.where(hit, -jnp.inf, vals)
    exps = [jnp.exp(t - tops[0]) for t in tops]
    denom = exps[0] + exps[1] + exps[2] + exps[3]

    before = (lax.broadcasted_iota(i32, (tm, tm), 0)
              < lax.broadcasted_iota(i32, (tm, tm), 1)).astype(bf16)
    cum = jnp.dot(picked.astype(bf16), before, preferred_element_type=f32) + cnt[...]
    slot = lax.broadcasted_iota(i32, (GATE_ROWS, tm), 0)
    pos = jnp.zeros((GATE_ROWS, tm), i32)
    gates = jnp.zeros((GATE_ROWS, tm), f32)
    for k in range(TOP_K):
        rank = jnp.sum(jnp.where(expert == idxs[k], cum, 0.0), axis=0, keepdims=True)
        p = idxs[k].astype(i32) * EXPERT_CAP + rank.astype(i32)
        pos = jnp.where(slot == k, p, pos)
        gates = jnp.where(slot == k, exps[k] / denom, gates)
    pos_ref[...] = pos
    gate_ref[...] = gates
    cnt[...] = cnt[...] + jnp.sum(picked, axis=1, keepdims=True)
    cnt_ref[...] = cnt[...]


def _outproj(ymix, w_out, x2d, g2, rw, rb, tm=512, tn=512):
    m = x2d.shape[0]
    resident = dict(pipeline_mode=pl.Buffered(1))
    return pl.pallas_call(
        functools.partial(_outproj_kernel, tm=tm, tn=tn),
        grid=(m // tm,),
        in_specs=[
            pl.BlockSpec((tm, D_MODEL), lambda i: (i, 0)),
            pl.BlockSpec((D_MODEL, D_MODEL), lambda i: (0, 0), **resident),
            pl.BlockSpec((tm, D_MODEL), lambda i: (i, 0)),
            pl.BlockSpec((1, D_MODEL), lambda i: (0, 0)),
            pl.BlockSpec((N_EXPERTS, D_MODEL), lambda i: (0, 0), **resident),
            pl.BlockSpec((N_EXPERTS, 1), lambda i: (0, 0)),
        ],
        out_specs=[
            pl.BlockSpec((tm, D_MODEL), lambda i: (i, 0)),
            pl.BlockSpec((tm, D_MODEL), lambda i: (i, 0)),
            pl.BlockSpec((GATE_ROWS, tm), lambda i: (0, i)),
            pl.BlockSpec((GATE_ROWS, tm), lambda i: (0, i)),
            pl.BlockSpec((N_EXPERTS, 1), lambda i: (0, 0)),
        ],
        out_shape=[
            jax.ShapeDtypeStruct((m, D_MODEL), f32),
            jax.ShapeDtypeStruct((m, D_MODEL), f32),
            jax.ShapeDtypeStruct((GATE_ROWS, m), i32),
            jax.ShapeDtypeStruct((GATE_ROWS, m), f32),
            jax.ShapeDtypeStruct((N_EXPERTS, 1), f32),
        ],
        scratch_shapes=[pltpu.VMEM((N_EXPERTS, 1), f32)],
        compiler_params=_cparams(("arbitrary",)),
        name="outproj_router",
    )(ymix, w_out, x2d, g2, rw, rb)


def _scatter_kernel(pos_ref, cnt_ref, hn_ref, xs_ref, zbuf, sem, zsem, *, tb):
    i = pl.program_id(0)

    @pl.when(i == 0)
    def _():
        zbuf[...] = jnp.zeros_like(zbuf)
        for e in range(N_EXPERTS):
            start = pl.multiple_of(e * EXPERT_CAP + (cnt_ref[e] // 8) * 8, 8)
            pltpu.make_async_copy(zbuf, xs_ref.at[pl.ds(start, ZERO_ROWS)], zsem).start()
        for e in range(N_EXPERTS):
            start = pl.multiple_of(e * EXPERT_CAP + (cnt_ref[e] // 8) * 8, 8)
            pltpu.make_async_copy(zbuf, xs_ref.at[pl.ds(start, ZERO_ROWS)], zsem).wait()

    t0 = pl.multiple_of(i * tb, tb)
    for t in range(tb):
        for k in range(TOP_K):
            p = pos_ref[t0 + (k * SEQ + t)]
            pltpu.make_async_copy(hn_ref.at[pl.ds(t, 1)], xs_ref.at[pl.ds(p, 1)],
                                  sem).start(priority=k % 2)
    done = xs_ref.at[pl.ds(0, tb * TOP_K)]
    pltpu.make_async_copy(done, done, sem).wait()


def _scatter(pos_flat, cnt_i, hn2, tb=256):
    m = hn2.shape[0]
    return pl.pallas_call(
        functools.partial(_scatter_kernel, tb=tb),
        grid_spec=pltpu.PrefetchScalarGridSpec(
            num_scalar_prefetch=2,
            grid=(m // tb,),
            in_specs=[pl.BlockSpec((tb, D_MODEL), lambda i, pos, cnt: (i, 0))],
            out_specs=pl.BlockSpec(memory_space=pl.ANY),
            scratch_shapes=[
                pltpu.VMEM((ZERO_ROWS, D_MODEL), f32),
                pltpu.SemaphoreType.DMA(()),
                pltpu.SemaphoreType.DMA(()),
            ],
        ),
        out_shape=jax.ShapeDtypeStruct((N_EXPERTS * EXPERT_CAP, D_MODEL), f32),
        compiler_params=_cparams(("arbitrary",)),
        name="scatter_rows",
    )(pos_flat, cnt_i, hn2)


def _pack_bf16_pair(lo, hi):
    lo_bits = lax.bitcast_convert_type(lo.astype(bf16).astype(f32), jnp.uint32)
    hi_bits = lax.bitcast_convert_type(hi.astype(bf16).astype(f32), jnp.uint32)
    return (lo_bits >> 16) | (hi_bits & jnp.uint32(0xFFFF0000))


def _unpack_bf16_pair(word):
    lo = lax.bitcast_convert_type(word << 16, f32)
    hi = lax.bitcast_convert_type(word & jnp.uint32(0xFFFF0000), f32)
    return lo, hi


def _moe_kernel(we_ref, wq_ref, wn_ref, tot_ref,
                xs_ref, wg_hbm, wu_hbm, wd_hbm, bg_ref, bu_ref, bd_ref,
                o_ref, stage, xb, abuf, ws_a, ws_b, ring, sem, wsem):
    w = pl.program_id(0)
    j = pl.program_id(1)
    e_cur = we_ref[w]
    units = wn_ref[w]
    n = (units + 1) // 2
    wnext = jnp.minimum(w + 1, MOE_WORK - 1)
    n_next = jnp.where(w + 1 < MOE_WORK, (wn_ref[wnext] + 1) // 2, 0)

    def tile_copy(item, r):
        e = we_ref[item]
        if r < 2 * MOE_NJ:
            src, blk = (wg_hbm if r % 2 == 0 else wu_hbm), r // 2
        else:
            src, blk = wd_hbm, r - 2 * MOE_NJ
        return pltpu.make_async_copy(src.at[e, :, pl.ds(blk * MOE_TN, MOE_TN)],
                                     ring.at[r % MOE_RING], wsem.at[r % MOE_RING])

    def tile_take(r, dst):
        tile_copy(w, r).wait()
        dst[...] = ring[r % MOE_RING].astype(bf16)
        r2 = r + MOE_RING
        if r2 < MOE_TILES:
            tile_copy(w, r2).start()
        else:
            @pl.when(w + 1 < tot_ref[0])
            def _():
                tile_copy(wnext, r2 - MOE_TILES).start()

    @pl.when((w == 0) & (j == 0))
    def _():
        for r in range(MOE_RING):
            tile_copy(w, r).start()

    def x_copy(item, s):
        base = we_ref[item] * EXPERT_CAP + wq_ref[item] * MOE_ROWS
        return pltpu.make_async_copy(
            xs_ref.at[pl.ds(pl.multiple_of(base + s * MOE_SUB, MOE_SUB), MOE_SUB)],
            stage.at[s % 2], sem.at[s % 2])

    def x_start(item, count, s):
        @pl.when(s < count)
        def _():
            x_copy(item, s).start()

    def x_finish(item, count, s):
        @pl.when(s < count)
        def _():
            x_copy(item, s).wait()
            xb[item % 2, pl.ds(s * MOE_SUB, MOE_SUB), :] = stage[s % 2].astype(bf16)

    @pl.when((w == 0) & (j == 0))
    def _():
        x_start(w, n, 0)
        x_start(w, n, 1)
        for s in range(MOE_SUBS):
            x_finish(w, n, s)
            if s + 2 < MOE_SUBS:
                x_start(w, n, s + 2)

    def per_subtile(fn):
        def pair(p, carry):
            r0 = p * (2 * MOE_SUB)
            fn(r0, MOE_SUB)
            fn(r0 + MOE_SUB, MOE_SUB)
            return carry

        pairs = units // 4
        rem = units % 4
        lax.fori_loop(0, pairs, pair, 0)

        @pl.when(rem >= 2)
        def _():
            fn(pairs * (2 * MOE_SUB), MOE_SUB)

        @pl.when(rem % 2 == 1)
        def _():
            fn(pairs * (2 * MOE_SUB) + (rem // 2) * MOE_SUB, MOE_UNIT)

    for jj in range(2 * MOE_NJ):
        @pl.when((j == jj) & (n > 0))
        def _():
            if jj < MOE_NJ:
                tile_take(2 * jj, ws_a)
                tile_take(2 * jj + 1, ws_b)
            else:
                tile_take(MOE_NJ + jj, ws_a)

    @pl.when((j < MOE_NJ) & (n > 0))
    def _():
        bgv = bg_ref[e_cur, pl.ds(j, 1), :]
        buv = bu_ref[e_cur, pl.ds(j, 1), :]

        def gate_up(r0, nrows):
            rows = pl.ds(pl.multiple_of(r0, MOE_UNIT), nrows)
            x = xb[w % 2, rows, :]
            g = jnp.dot(x, ws_a[...], preferred_element_type=f32) + bgv
            u = jnp.dot(x, ws_b[...], preferred_element_type=f32) + buv
            g = jnp.minimum(g, SWIGLU_LIMIT)
            u = jnp.clip(u, -SWIGLU_LIMIT, SWIGLU_LIMIT)
            act = (u + 1.0) * (g * jax.nn.sigmoid(SWIGLU_ALPHA * g))
            abuf[j, rows, :] = act.astype(bf16)

        per_subtile(gate_up)

    @pl.when((j >= MOE_NJ) & (n > 0))
    def _():
        bdv = bd_ref[e_cur, pl.ds(j - MOE_NJ, 1), :]

        def down(r0, nrows):
            rows = pl.ds(pl.multiple_of(r0, MOE_UNIT), nrows)
            acc = jnp.zeros((nrows, MOE_TN), f32) + bdv
            for q in range(MOE_NJ):
                acc = acc + jnp.dot(abuf[q, rows, :], ws_a[q * MOE_TN:(q + 1) * MOE_TN, :],
                                    preferred_element_type=f32)
            o_ref[rows, :] = _pack_bf16_pair(acc[:, :MOE_TN // 2], acc[:, MOE_TN // 2:])

        per_subtile(down)

    assert 2 * (MOE_NJ - 1) >= MOE_SUBS
    for k in range(MOE_NJ):
        @pl.when(j == k)
        def _():
            if k == 0:
                x_start(wnext, n_next, 0)
                x_start(wnext, n_next, 1)
            else:
                for s in (2 * k - 2, 2 * k - 1):
                    if s < MOE_SUBS:
                        x_finish(wnext, n_next, s)
                        if s + 2 < MOE_SUBS:
                            x_start(wnext, n_next, s + 2)


def _moe(we, wq, wn, tot, xs, w_gate, w_up, w_down, b_gate, b_up, b_down):
    blocks_per_expert = EXPERT_CAP // MOE_ROWS

    bias_spec = pl.BlockSpec((N_EXPERTS, MOE_NJ, MOE_TN), lambda w, j, we, wq, wn, tot: (0, 0, 0),
                             pipeline_mode=pl.Buffered(1))

    def o_map(w, j, we, wq, wn, tot):
        jj = jnp.where(wn[w] > 0, jnp.maximum(j - MOE_NJ, 0), MOE_NJ - 1)
        return (we[w] * blocks_per_expert + wq[w], jj)

    any_space = pl.BlockSpec(memory_space=pl.ANY)
    return pl.pallas_call(
        _moe_kernel,
        grid_spec=pltpu.PrefetchScalarGridSpec(
            num_scalar_prefetch=4,
            grid=(MOE_WORK, 2 * MOE_NJ),
            in_specs=[
                any_space, any_space, any_space, any_space,
                bias_spec, bias_spec, bias_spec,
            ],
            out_specs=pl.BlockSpec((MOE_ROWS, MOE_TN // 2), o_map),
            scratch_shapes=[
                pltpu.VMEM((2, MOE_SUB, D_MODEL), f32),
                pltpu.VMEM((2, MOE_ROWS, D_MODEL), bf16),
                pltpu.VMEM((MOE_NJ, MOE_ROWS, MOE_TN), bf16),
                pltpu.VMEM((D_MODEL, MOE_TN), bf16),
                pltpu.VMEM((D_MODEL, MOE_TN), bf16),
                pltpu.VMEM((MOE_RING, D_MODEL, MOE_TN), f32),
                pltpu.SemaphoreType.DMA((2,)),
                pltpu.SemaphoreType.DMA((MOE_RING,)),
            ],
        ),
        out_shape=jax.ShapeDtypeStruct((N_EXPERTS * EXPERT_CAP, D_MODEL // 2), jnp.uint32),
        compiler_params=_cparams(("arbitrary", "arbitrary")),
        name="moe_experts",
    )(we, wq, wn, tot, xs, w_gate, w_up, w_down,
      b_gate.reshape(N_EXPERTS, MOE_NJ, MOE_TN), b_up.reshape(N_EXPERTS, MOE_NJ, MOE_TN),
      b_down.reshape(N_EXPERTS, MOE_NJ, MOE_TN))


def _work_list(cnt_i):
    nq = (cnt_i + MOE_ROWS - 1) // MOE_ROWS
    ends = jnp.cumsum(nq)
    total = ends[-1]
    w = jnp.arange(MOE_WORK, dtype=i32)
    wc = jnp.minimum(w, total - 1)
    e = jnp.minimum(jnp.sum((ends[None, :] <= wc[:, None]).astype(i32), axis=1), N_EXPERTS - 1)
    q = wc - (ends[e] - nq[e])
    rows = jnp.clip(cnt_i[e] - q * MOE_ROWS, 0, MOE_ROWS)
    n = jnp.where(w < total, (rows + MOE_UNIT - 1) // MOE_UNIT, 0)
    return e.astype(i32), q.astype(i32), n.astype(i32), total.astype(i32).reshape(1)


def _combine_kernel(pos_ref, o_ref, gate_ref, h2_ref, fg_ref, out_ref, obuf, sem, *, tm):
    i = pl.program_id(0)
    t0 = pl.multiple_of(i * (2 * tm), 2 * tm)

    def issue(part):
        for t in range(tm):
            for k in range(TOP_K):
                p = pos_ref[t0 + (k * SEQ + part * tm + t)]
                pltpu.make_async_copy(o_ref.at[pl.ds(p, 1)],
                                      obuf.at[part, pl.ds(k * tm + t, 1)],
                                      sem.at[part]).start(priority=k % 2)

    def finish(part):
        pltpu.make_async_copy(obuf.at[part], obuf.at[part], sem.at[part]).wait()
        rows = slice(part * tm, (part + 1) * tm)
        gt = gate_ref[:, rows]
        eye = (lax.broadcasted_iota(i32, (tm, tm), 0)
               == lax.broadcasted_iota(i32, (tm, tm), 1)).astype(bf16)
        g1 = gt.astype(bf16)
        r1 = gt - g1.astype(f32)
        g2 = r1.astype(bf16)
        g3 = (r1 - g2.astype(f32)).astype(bf16)
        gates = _nt_dot(eye, g1) + _nt_dot(eye, g2) + _nt_dot(eye, g3)
        half = MOE_TN // 2
        y_parts = [h2_ref[rows, c * half:(c + 1) * half] for c in range(D_MODEL // half)]
        for k in range(TOP_K):
            gk = gates[:, k:k + 1]
            for b in range(D_MODEL // MOE_TN):
                lo, hi = _unpack_bf16_pair(
                    obuf[part, k * tm:(k + 1) * tm, b * half:(b + 1) * half])
                y_parts[2 * b] = y_parts[2 * b] + gk * lo
                y_parts[2 * b + 1] = y_parts[2 * b + 1] + gk * hi
        ss = jnp.zeros((tm, 1), f32)
        for yp in y_parts:
            ss = ss + jnp.sum(yp * yp, axis=-1, keepdims=True)
        inv = lax.rsqrt(ss / float(D_MODEL) + NORM_EPS)
        for c, yp in enumerate(y_parts):
            cols = slice(c * half, (c + 1) * half)
            out_ref[rows, cols] = yp * inv * fg_ref[:, cols]

    issue(0)
    issue(1)
    finish(0)
    finish(1)


def _combine(pos_flat, o, gates, h2, final_g, tm=256):
    m = h2.shape[0]
    return pl.pallas_call(
        functools.partial(_combine_kernel, tm=tm),
        grid_spec=pltpu.PrefetchScalarGridSpec(
            num_scalar_prefetch=1,
            grid=(m // (2 * tm),),
            in_specs=[
                pl.BlockSpec(memory_space=pl.ANY),
                pl.BlockSpec((GATE_ROWS, 2 * tm), lambda i, pos: (0, i)),
                pl.BlockSpec((2 * tm, D_MODEL), lambda i, pos: (i, 0)),
                pl.BlockSpec((1, D_MODEL), lambda i, pos: (0, 0)),
            ],
            out_specs=pl.BlockSpec((2 * tm, D_MODEL), lambda i, pos: (i, 0)),
            scratch_shapes=[
                pltpu.VMEM((2, TOP_K * tm, D_MODEL // 2), jnp.uint32),
                pltpu.SemaphoreType.DMA((2,)),
            ],
        ),
        out_shape=jax.ShapeDtypeStruct((m, D_MODEL), f32),
        compiler_params=_cparams(("arbitrary",)),
        name="combine",
    )(pos_flat, o, gates, h2, final_g)


def kernel(x, meta_tokens, norm1_g, w_in, pool_w, pool_scale, conv_w, conv_b, dt_bias, a_log,
           d_skip, ssm_norm_g, w_out, norm2_g, router_w, router_b, w_gate, b_gate, w_up, b_up,
           w_down, b_down, final_g):
    assert x.shape == (1, SEQ, D_MODEL)
    x2d = x.reshape(SEQ, D_MODEL)
    lead = CHUNK - N_META
    meta_chunk = jnp.concatenate([jnp.zeros((lead, D_MODEL), f32), meta_tokens.astype(f32)], 0)

    g1 = norm1_g[0].reshape(1, D_MODEL)
    w_main = w_in[0, :, :D_MAIN].astype(bf16)
    w_dt = w_in[0, :, D_MAIN:].astype(bf16)
    head_of_col = jnp.arange(D_SSM, dtype=i32) // HEAD_DIM
    expand = (head_of_col[None, :] == jnp.arange(N_HEADS, dtype=i32)[:, None]).astype(bf16)
    mix_params = (
        pool_w[0], pool_scale[0].reshape(1, D_POOL), conv_w[0], conv_b[0].reshape(1, D_CONV),
        dt_bias[0].reshape(1, N_HEADS), a_log[0].reshape(1, N_HEADS),
        jnp.repeat(d_skip[0], HEAD_DIM).reshape(1, D_SSM), ssm_norm_g[0].reshape(1, D_SSM),
        expand)

    proj_m, dt_m = _inproj(meta_chunk, g1, w_main, w_dt, tm=CHUNK)
    zeros_state = jnp.zeros((N_GROUPS, N_STATE, GROUP_COLS), f32)
    _, state_m, utail_m, ctail_m = _mixer(
        proj_m, dt_m, mix_params, zeros_state, jnp.zeros((POOL_HALO, D_POOL), f32),
        jnp.zeros((CONV_HALO, D_CONV), f32), tm=CHUNK, lead=lead)

    proj, dt_raw = _inproj(x2d, g1, w_main, w_dt, tm=512)
    ymix, _, _, _ = _mixer(proj, dt_raw, mix_params, state_m, utail_m, ctail_m, tm=256, lead=0)

    h2, hn2, pos, gates, cnt = _outproj(
        ymix, w_out[0].astype(bf16), x2d, norm2_g[0].reshape(1, D_MODEL),
        router_w[0].T.astype(bf16),
        router_b[0].reshape(N_EXPERTS, 1))

    pos_flat = pos[:TOP_K].reshape(TOP_K * SEQ)
    cnt_i = cnt.reshape(N_EXPERTS).astype(i32)
    xs = _scatter(pos_flat, cnt_i, hn2)
    we, wq, wn, tot = _work_list(cnt_i)
    o = _moe(we, wq, wn, tot, xs, w_gate[0], w_up[0], w_down[0], b_gate[0], b_up[0], b_down[0])
    out = _combine(pos_flat, o, gates, h2, final_g.reshape(1, D_MODEL))
    return out.reshape(1, SEQ, D_MODEL)
```
